```python
import math
import jax, jax.numpy as jnp
from jax import lax
import numpy as np

D_MODEL = 1024
BATCH = 8
SEQ = 2048
DEPTH = 2
DEC_BATCH = 128
DEC_SEQ = 8
PAST_LEN = 16384
PAGE_SIZE = 128

N_MIXERS = 2
N_LAYERS_A = (DEPTH + N_MIXERS - 1) // N_MIXERS
N_LAYERS_B = DEPTH // N_MIXERS

A_HEADS = 8
A_DK = 128
A_DV = 128
A_QK = A_HEADS * A_DK
A_VW = A_HEADS * A_DV
A_CONV_CH = 2 * A_QK + A_VW
CONV_W = 4
A_PROJ = A_CONV_CH + A_VW + 2 * A_HEADS

B_HEADS = 4
B_DK = 256
B_DV = 512
B_QK = B_HEADS * B_DK
B_VW = B_HEADS * B_DV
B_PROJ = 2 * B_QK + 2 * B_VW

CHUNK = 64
ROPE_BASE = 10000.0
EPS = 1e-6
F32 = jnp.float32

kernel_name = 'hybrid_gdn_retention_step'


def _rmsnorm(x, w):
    xf = x.astype(F32)
    return xf * lax.rsqrt(jnp.mean(xf * xf, -1, keepdims=True) + EPS) * w.astype(F32)


def _l2norm(x):
    return x * lax.rsqrt(jnp.sum(x * x, -1, keepdims=True) + EPS)


def _chunk_size(L):
    return L if L <= CHUNK else math.gcd(L, CHUNK)


def _to_chunks(t, C):
    B, L = t.shape[:2]
    t = t.reshape((B, L // C, C) + t.shape[2:])
    return jnp.swapaxes(jnp.moveaxis(t, 1, 0), 2, 3)


def _from_chunks(o):
    n, B, H, C, d = o.shape
    return jnp.transpose(o, (1, 0, 3, 2, 4)).reshape(B, n * C, H, d)


def _short_conv(x, buf, w):
    L = x.shape[1]
    xe = jnp.concatenate([buf.astype(F32), x], axis=1)
    out = sum(xe[:, j:j + L] * w[j] for j in range(CONV_W))
    return jax.nn.silu(out), xe[:, L:]


def _gated_delta_chunked(q, k, v, beta, g, S0):
    L = q.shape[1]
    C = _chunk_size(L)
    dv = v.shape[-1]
    idx = jnp.arange(C)
    strict = idx[:, None] > idx[None, :]
    incl = idx[:, None] >= idx[None, :]
    eye = jnp.eye(C, dtype=F32)

    def step(S, inp):
        qc, kc, vc, bc, gc = inp
        G = jnp.cumsum(gc, axis=-1)
        diff = G[..., :, None] - G[..., None, :]
        dec = jnp.where(incl, jnp.exp(jnp.where(incl, diff, 0.0)), 0.0)
        kk = jnp.einsum('bhid,bhjd->bhij', kc, kc)
        low = jnp.where(strict, bc[..., :, None] * kk * dec, 0.0)
        eG = jnp.exp(G)
        rhs = jnp.concatenate([bc[..., None] * vc, (bc * eG)[..., None] * kc], axis=-1)
        sol = lax.linalg.triangular_solve(low + eye, rhs, left_side=True, lower=True,
                                          unit_diagonal=True)
        u = sol[..., :dv] - jnp.einsum('bhck,bhkv->bhcv', sol[..., dv:], S)
        qk = jnp.einsum('bhid,bhjd->bhij', qc, kc) * dec
        o = eG[..., None] * jnp.einsum('bhck,bhkv->bhcv', qc, S) + jnp.einsum('bhij,bhjv->bhiv', qk, u)
        S_new = (jnp.exp(G[..., -1])[..., None, None] * S
                 + jnp.einsum('bhck,bhcv->bhkv', kc * jnp.exp(G[..., -1:] - G)[..., None], u))
        return S_new, o

    xs = tuple(_to_chunks(t.astype(F32), C) for t in (q, k, v, beta, g))
    S, o = lax.scan(step, S0.astype(F32), xs)
    return _from_chunks(o), S


def _retention_chunked(q, k, v, S0):
    L = q.shape[1]
    C = _chunk_size(L)
    lg = jnp.log(1.0 - 2.0 ** (-5.0 - jnp.arange(B_HEADS, dtype=F32)))
    idx = jnp.arange(C, dtype=F32)
    diff = idx[:, None] - idx[None, :]
    D = jnp.where(diff >= 0, jnp.exp(lg[:, None, None] * jnp.maximum(diff, 0.0)), 0.0)
    q_dec = jnp.exp(lg[:, None] * (idx + 1.0))
    k_dec = jnp.exp(lg[:, None] * (C - 1.0 - idx))
    s_dec = jnp.exp(lg * C)

    def step(S, inp):
        qc, kc, vc = inp
        qk = jnp.einsum('bhid,bhjd->bhij', qc, kc) * D
        o = (jnp.einsum('bhij,bhjv->bhiv', qk, vc)
             + q_dec[..., None] * jnp.einsum('bhck,bhkv->bhcv', qc, S))
        S_new = s_dec[:, None, None] * S + jnp.einsum('bhck,bhcv->bhkv', kc * k_dec[..., None], vc)
        return S_new, o

    xs = tuple(_to_chunks(t.astype(F32), C) for t in (q, k, v))
    S, o = lax.scan(step, S0.astype(F32), xs)
    return _from_chunks(o), S


def _rotary(x, pos0):
    L, d = x.shape[1], x.shape[-1]
    half = d // 2
    inv = 1.0 / (ROPE_BASE ** jnp.linspace(0.0, 1.0, half, dtype=F32))
    pos = pos0 + jnp.arange(L, dtype=F32)
    ang = pos[:, None] * inv[None, :]
    cos = jnp.cos(ang)[None, :, None, :]
    sin = jnp.sin(ang)[None, :, None, :]
    x1, x2 = x[..., :half], x[..., half:]
    return jnp.concatenate([x1 * cos - x2 * sin, x1 * sin + x2 * cos], axis=-1)


def _gdn_layer(x, S0, buf, norm_w, w_in, conv_w, a_log, dt_bias, onorm_w, w_out):
    B, L, _ = x.shape
    p = _rmsnorm(x, norm_w) @ w_in.astype(F32)
    mixed, z, b_raw, a_raw = jnp.split(
        p, [A_CONV_CH, A_CONV_CH + A_VW, A_CONV_CH + A_VW + A_HEADS], axis=-1)
    mixed, new_buf = _short_conv(mixed, buf, conv_w.astype(F32))
    q, k, v = jnp.split(mixed, [A_QK, 2 * A_QK], axis=-1)
    q = _l2norm(q.reshape(B, L, A_HEADS, A_DK)) * (A_DK ** -0.5)
    k = _l2norm(k.reshape(B, L, A_HEADS, A_DK))
    v = v.reshape(B, L, A_HEADS, A_DV)
    beta = jax.nn.sigmoid(b_raw)
    g = -jnp.exp(a_log.astype(F32)) * jax.nn.softplus(a_raw + dt_bias.astype(F32))
    o, S = _gated_delta_chunked(q, k, v, beta, g, S0)
    o = _rmsnorm(o, onorm_w) * jax.nn.silu(z.reshape(B, L, A_HEADS, A_DV))
    y = x + (o.reshape(B, L, A_VW) @ w_out.astype(F32)).astype(x.dtype)
    return y, S.astype(x.dtype), new_buf.astype(x.dtype)


def _ret_layer(x, pos0, S0, norm_w, w_in, onorm_w, w_out):
    B, L, _ = x.shape
    p = _rmsnorm(x, norm_w) @ w_in.astype(F32)
    q, k, v, gate = jnp.split(p, [B_QK, 2 * B_QK, 2 * B_QK + B_VW], axis=-1)
    q = _rotary(q.reshape(B, L, B_HEADS, B_DK), pos0)
    k = _rotary(k.reshape(B, L, B_HEADS, B_DK), pos0) * (B_DK ** -0.5)
    v = v.reshape(B, L, B_HEADS, B_DV)
    o, S = _retention_chunked(q, k, v, S0)
    o = _rmsnorm(o, onorm_w) * jax.nn.silu(gate.reshape(B, L, B_HEADS, B_DV))
    y = x + (o.reshape(B, L, B_VW) @ w_out.astype(F32)).astype(x.dtype)
    return y, S.astype(x.dtype)


def _trunk(x, pos0, gdn_S, gdn_conv, ret_S, norm_w, w_in_a, conv_w_a, a_log_a, dt_bias_a,
           onorm_a, w_out_a, w_in_b, onorm_b, w_out_b, final_norm_w):
    sa, ca, sb = [], [], []
    for i in range(DEPTH):
        j = i // N_MIXERS
        if i % N_MIXERS == 0:
            x, s, c = _gdn_layer(x, gdn_S[j], gdn_conv[j], norm_w[i], w_in_a[j], conv_w_a[j],
                                 a_log_a[j], dt_bias_a[j], onorm_a[j], w_out_a[j])
            sa.append(s)
            ca.append(c)
        else:
            x, s = _ret_layer(x, pos0, ret_S[j], norm_w[i], w_in_b[j], onorm_b[j], w_out_b[j])
            sb.append(s)
    y = _rmsnorm(x, final_norm_w).astype(x.dtype)
    return y, jnp.stack(sa), jnp.stack(ca), jnp.stack(sb)


def setup_inputs(seed: int = 0) -> dict:
    key = jax.random.key(seed)
    ks = jax.random.split(key, 16)
    nrm = jax.random.normal
    dt = jnp.exp(jax.random.uniform(ks[8], (N_LAYERS_A, A_HEADS), F32,
                                    math.log(1e-3), math.log(1e-1)))
    return {
        'x_prompt': nrm(ks[0], (BATCH, SEQ, D_MODEL), F32),
        'x_sample': nrm(ks[1], (DEC_BATCH, DEC_SEQ, D_MODEL), F32),
        'state_gdn_ssm': 0.1 * nrm(ks[2], (N_LAYERS_A, DEC_BATCH, A_HEADS, A_DK, A_DV), F32),
        'state_gdn_conv': nrm(ks[3], (N_LAYERS_A, DEC_BATCH, CONV_W - 1, A_CONV_CH), F32),
        'state_ret': 0.1 * nrm(ks[4], (N_LAYERS_B, DEC_BATCH, B_HEADS, B_DK, B_DV), F32),
        'norm_w': 1.0 + 0.02 * nrm(ks[5], (DEPTH, D_MODEL), F32),
        'w_in_a': nrm(ks[6], (N_LAYERS_A, D_MODEL, A_PROJ), F32) * D_MODEL ** -0.5,
        'conv_w_a': nrm(ks[7], (N_LAYERS_A, CONV_W, A_CONV_CH), F32) * CONV_W ** -0.5,
        'a_log_a': jnp.log(jax.random.uniform(ks[9], (N_LAYERS_A, A_HEADS), F32, 1.0, 16.0)),
        'dt_bias_a': dt + jnp.log(-jnp.expm1(-dt)),
        'onorm_a': 1.0 + 0.02 * nrm(ks[10], (N_LAYERS_A, A_DV), F32),
        'w_out_a': nrm(ks[11], (N_LAYERS_A, A_VW, D_MODEL), F32) * A_VW ** -0.5,
        'w_in_b': nrm(ks[12], (N_LAYERS_B, D_MODEL, B_PROJ), F32) * D_MODEL ** -0.5,
        'onorm_b': 1.0 + 0.02 * nrm(ks[13], (N_LAYERS_B, B_HEADS, B_DV), F32),
        'w_out_b': nrm(ks[14], (N_LAYERS_B, B_VW, D_MODEL), F32) * B_VW ** -0.5,
        'final_norm_w': 1.0 + 0.02 * nrm(ks[15], (D_MODEL,), F32),
    }


def reference(x_prompt, x_sample, state_gdn_ssm, state_gdn_conv, state_ret, norm_w, w_in_a,
              conv_w_a, a_log_a, dt_bias_a, onorm_a, w_out_a, w_in_b, onorm_b, w_out_b,
              final_norm_w):
    bp = x_prompt.shape[0]
    dtp = x_prompt.dtype
    z_sa = jnp.zeros((N_LAYERS_A, bp, A_HEADS, A_DK, A_DV), dtp)
    z_ca = jnp.zeros((N_LAYERS_A, bp, CONV_W - 1, A_CONV_CH), dtp)
    z_sb = jnp.zeros((N_LAYERS_B, bp, B_HEADS, B_DK, B_DV), dtp)
    y_prompt, sa_p, ca_p, sb_p = _trunk(x_prompt, 0.0, z_sa, z_ca, z_sb, norm_w, w_in_a,
                                        conv_w_a, a_log_a, dt_bias_a, onorm_a, w_out_a,
                                        w_in_b, onorm_b, w_out_b, final_norm_w)
    y_sample, sa_s, ca_s, sb_s = _trunk(x_sample, float(PAST_LEN), state_gdn_ssm, state_gdn_conv,
                                        state_ret, norm_w, w_in_a, conv_w_a, a_log_a,
                                        dt_bias_a, onorm_a, w_out_a, w_in_b, onorm_b,
                                        w_out_b, final_norm_w)
    return (y_prompt, y_sample, sa_p, ca_p, sb_p, sa_s, ca_s, sb_s)
```

```python
import functools

import jax
import jax.numpy as jnp
from jax import lax
from jax.experimental import pallas as pl
from jax.experimental.pallas import tpu as pltpu

F32 = jnp.float32
BF16 = jnp.bfloat16
EPS = 1e-6
CHUNK = 64
CONV_W = 4
ROPE_BASE = 10000.0
LANES = 128
SUBLANES = 8
VMEM_LIMIT_BYTES = 56 * 1024 * 1024
HIGHEST = lax.Precision.HIGHEST


def _dot(a, b, precision=None):
    return jnp.dot(a, b, preferred_element_type=F32, precision=precision)


def _dot_nt(a, b, precision=None):
    return lax.dot_general(a, b, (((1,), (1,)), ((), ())), preferred_element_type=F32,
                           precision=precision)


def _dot_tn(a, b, precision=None):
    return lax.dot_general(a, b, (((0,), (0,)), ((), ())), preferred_element_type=F32,
                           precision=precision)


def _sigmoid(x):
    return 1.0 / (1.0 + jnp.exp(-x))


def _silu(x):
    return x * _sigmoid(x)


def _softplus(x):
    return jnp.maximum(x, 0.0) + jnp.log1p(jnp.exp(-jnp.abs(x)))


def _params(*sem):
    return pltpu.CompilerParams(dimension_semantics=sem, vmem_limit_bytes=VMEM_LIMIT_BYTES)


def _in_proj_kernel(x_ref, nw_ref, *refs):
    n = len(refs) // 2
    w_refs, o_refs = refs[:n], refs[n:]
    x = x_ref[...]
    ms = jnp.mean(x * x, axis=-1, keepdims=True)
    xn = (x * lax.rsqrt(ms + EPS) * nw_ref[...]).astype(BF16)
    for w_ref, o_ref in zip(w_refs, o_refs):
        o_ref[...] = _dot(xn, w_ref[...])


def _in_proj(x, norm_w, weights, tm):
    t, d = x.shape
    assert t % tm == 0
    in_specs = [pl.BlockSpec((tm, d), lambda i: (i, 0)),
                pl.BlockSpec((1, d), lambda i: (0, 0))]
    in_specs += [pl.BlockSpec(w.shape, lambda i: (0, 0)) for w in weights]
    out_specs = [pl.BlockSpec((tm, w.shape[1]), lambda i: (i, 0)) for w in weights]
    out_shape = [jax.ShapeDtypeStruct((t, w.shape[1]), F32) for w in weights]
    return pl.pallas_call(
        _in_proj_kernel, grid=(t // tm,), in_specs=in_specs, out_specs=out_specs,
        out_shape=out_shape, compiler_params=_params("parallel"), name="in_proj",
    )(x, norm_w.reshape(1, d), *weights)


def _out_proj_kernel(o_ref, x_ref, w_ref, fw_ref, y_ref, *, final_norm):
    y = x_ref[...] + _dot(o_ref[...].astype(BF16), w_ref[...])
    if final_norm:
        ms = jnp.mean(y * y, axis=-1, keepdims=True)
        y = y * lax.rsqrt(ms + EPS) * fw_ref[...]
    y_ref[...] = y


def _out_proj(o, x, w, final_w, tm, final_norm):
    t, k = o.shape
    d = x.shape[1]
    assert t % tm == 0
    return pl.pallas_call(
        functools.partial(_out_proj_kernel, final_norm=final_norm),
        grid=(t // tm,),
        in_specs=[pl.BlockSpec((tm, k), lambda i: (i, 0)),
                  pl.BlockSpec((tm, d), lambda i: (i, 0)),
                  pl.BlockSpec((k, d), lambda i: (0, 0)),
                  pl.BlockSpec((1, d), lambda i: (0, 0))],
        out_specs=pl.BlockSpec((tm, d), lambda i: (i, 0)),
        out_shape=jax.ShapeDtypeStruct((t, d), F32),
        compiler_params=_params("parallel"), name="out_proj",
    )(o, x, w, final_w.reshape(1, d))


def _unit_lower_inverse(low, c):
    row = lax.broadcasted_iota(jnp.int32, (c, c), 0)
    col = lax.broadcasted_iota(jnp.int32, (c, c), 1)
    eye = (row == col).astype(F32)
    n = -low
    inv = eye + n
    power, p = n, 1
    while 2 * p < c:
        power = _dot(power, power, HIGHEST)
        p *= 2
        inv = inv + _dot(inv, power, HIGHEST)
    return inv


def _gdn_core_kernel(p_ref, ba_ref, s0_ref, c0_ref, cw_ref, alog_ref, dtb_ref, onw_ref,
                     o_ref, s_ref, c_ref, s_scr, xe_scr, *, heads, dk, dv):
    c = p_ref.shape[1]
    qk_w = heads * dk
    conv_ch = 2 * qk_w + heads * dv
    hist0 = SUBLANES - (CONV_W - 1)
    t = pl.program_id(1)

    @pl.when(t == 0)
    def _():
        s_scr[...] = s0_ref[0]
        xe_scr[hist0:SUBLANES, :] = c0_ref[0]

    xe_scr[SUBLANES:SUBLANES + c, :] = p_ref[0, :, 0:conv_ch]
    conv = xe_scr[hist0:hist0 + c, :] * cw_ref[0:1, :]
    for j in range(1, CONV_W):
        conv = conv + xe_scr[hist0 + j:hist0 + j + c, :] * cw_ref[j:j + 1, :]
    new_hist = xe_scr[c + hist0:c + SUBLANES, :]
    xe_scr[hist0:SUBLANES, :] = new_hist
    mixed = _silu(conv)

    beta = _sigmoid(ba_ref[0, :, 0:LANES])
    g = -jnp.exp(alog_ref[...]) * _softplus(ba_ref[0, :, LANES:2 * LANES] + dtb_ref[...])

    row = lax.broadcasted_iota(jnp.int32, (c, c), 0)
    col = lax.broadcasted_iota(jnp.int32, (c, c), 1)
    incl = row >= col
    strict = row > col
    gcum = _dot(incl.astype(F32), g, HIGHEST)
    sel = (lax.broadcasted_iota(jnp.int32, (SUBLANES, LANES), 0)
           == lax.broadcasted_iota(jnp.int32, (SUBLANES, LANES), 1)).astype(F32)
    gcum_t = _dot_nt(sel, gcum, HIGHEST)

    for h in range(heads):
        q = mixed[:, h * dk:(h + 1) * dk]
        k = mixed[:, qk_w + h * dk:qk_w + (h + 1) * dk]
        v = mixed[:, 2 * qk_w + h * dv:2 * qk_w + (h + 1) * dv]
        q = q * lax.rsqrt(jnp.sum(q * q, axis=-1, keepdims=True) + EPS) * (dk ** -0.5)
        k = k * lax.rsqrt(jnp.sum(k * k, axis=-1, keepdims=True) + EPS)
        gc = gcum[:, h:h + 1]
        gr = gcum_t[h:h + 1, :]
        bh = beta[:, h:h + 1]
        dec = jnp.where(incl, jnp.exp(jnp.where(incl, gc - gr, 0.0)), 0.0)
        eg = jnp.exp(gc)
        qb = q.astype(BF16)
        kb = k.astype(BF16)
        kk = _dot_nt(kb, kb)
        low = jnp.where(strict, bh * kk * dec, 0.0)
        inv = _unit_lower_inverse(low, c)
        rhs = jnp.concatenate([bh * v, (bh * eg) * k], axis=-1).astype(BF16)
        sol = _dot(inv.astype(BF16), rhs)
        s_h = s_scr[h]
        s_b = s_h.astype(BF16)
        u = sol[:, :dv] - _dot(sol[:, dv:].astype(BF16), s_b)
        ub = u.astype(BF16)
        qk = _dot_nt(qb, kb) * dec
        o = eg * _dot(qb, s_b) + _dot(qk.astype(BF16), ub)
        g_last = gcum[c - 1:c, h:h + 1]
        kd = (k * jnp.exp(g_last - gc)).astype(BF16)
        s_scr[h] = jnp.exp(g_last) * s_h + _dot_tn(kd, ub)
        on = o * lax.rsqrt(jnp.mean(o * o, axis=-1, keepdims=True) + EPS) * onw_ref[...]
        z = p_ref[0, :, conv_ch + h * dv:conv_ch + (h + 1) * dv]
        o_ref[0, :, h * dv:(h + 1) * dv] = on * _silu(z)

    @pl.when(t == pl.num_programs(1) - 1)
    def _():
        s_ref[0] = s_scr[...]
        c_ref[0] = new_hist


def _gdn_core(p, ba, s0, c0, conv_w, a_log, dt_bias, onorm_w, c):
    b, l, pw = p.shape
    heads, dk, dv = s0.shape[1:]
    conv_ch = c0.shape[2]
    assert l % c == 0 and c % SUBLANES == 0 and heads <= SUBLANES
    assert dk == LANES and dv == LANES and pw == conv_ch + heads * dv

    def lane_pad(vec):
        return jnp.pad(vec.astype(F32), (0, LANES - vec.shape[0])).reshape(1, LANES)

    kern = functools.partial(_gdn_core_kernel, heads=heads, dk=dk, dv=dv)
    return pl.pallas_call(
        kern, grid=(b, l // c),
        in_specs=[pl.BlockSpec((1, c, pw), lambda i, t: (i, t, 0)),
                  pl.BlockSpec((1, c, 2 * LANES), lambda i, t: (i, t, 0)),
                  pl.BlockSpec((1, heads, dk, dv), lambda i, t: (i, 0, 0, 0)),
                  pl.BlockSpec((1, CONV_W - 1, conv_ch), lambda i, t: (i, 0, 0)),
                  pl.BlockSpec((CONV_W, conv_ch), lambda i, t: (0, 0)),
                  pl.BlockSpec((1, LANES), lambda i, t: (0, 0)),
                  pl.BlockSpec((1, LANES), lambda i, t: (0, 0)),
                  pl.BlockSpec((1, dv), lambda i, t: (0, 0))],
        out_specs=[pl.BlockSpec((1, c, heads * dv), lambda i, t: (i, t, 0)),
                   pl.BlockSpec((1, heads, dk, dv), lambda i, t: (i, 0, 0, 0)),
                   pl.BlockSpec((1, CONV_W - 1, conv_ch), lambda i, t: (i, 0, 0))],
        out_shape=[jax.ShapeDtypeStruct((b, l, heads * dv), F32),
                   jax.ShapeDtypeStruct(s0.shape, F32),
                   jax.ShapeDtypeStruct(c0.shape, F32)],
        scratch_shapes=[pltpu.VMEM((heads, dk, dv), F32),
                        pltpu.VMEM((SUBLANES + c, conv_ch), F32)],
        compiler_params=_params("parallel", "arbitrary"), name="gdn_core",
    )(p, ba, s0, c0, conv_w, lane_pad(a_log), lane_pad(dt_bias), onorm_w.reshape(1, dv))


def _ret_core_kernel(p_ref, cos_ref, sin_ref, s0_ref, onw_ref, o_ref, s_ref, s_scr,
                     *, heads, dk, dv):
    c = p_ref.shape[1]
    qk_w = heads * dk
    half = dk // 2
    t = pl.program_id(1)

    @pl.when(t == 0)
    def _():
        s_scr[...] = s0_ref[0]

    cos = cos_ref[...]
    sin = sin_ref[...]
    row = lax.broadcasted_iota(jnp.int32, (c, c), 0)
    col = lax.broadcasted_iota(jnp.int32, (c, c), 1)
    diff = (row - col).astype(F32)
    idx = lax.broadcasted_iota(jnp.int32, (c, 1), 0).astype(F32)

    def rotary(x):
        x1, x2 = x[:, :half], x[:, half:]
        return jnp.concatenate([x1 * cos - x2 * sin, x1 * sin + x2 * cos], axis=-1)

    for h in range(heads):
        lg = jnp.log(jnp.full((1, 1), 1.0 - 2.0 ** (-5.0 - h), F32))
        decay = jnp.where(diff >= 0, jnp.exp(lg * jnp.maximum(diff, 0.0)), 0.0)
        q_dec = jnp.exp(lg * (idx + 1.0))
        k_dec = jnp.exp(lg * (c - 1.0 - idx))
        s_dec = jnp.exp(lg * c)
        q = rotary(p_ref[0, :, h * dk:(h + 1) * dk])
        k = rotary(p_ref[0, :, qk_w + h * dk:qk_w + (h + 1) * dk]) * (dk ** -0.5)
        v = p_ref[0, :, 2 * qk_w + h * dv:2 * qk_w + (h + 1) * dv]
        qb = q.astype(BF16)
        kb = k.astype(BF16)
        vb = v.astype(BF16)
        s_h = s_scr[h]
        qk = _dot_nt(qb, kb) * decay
        o = _dot(qk.astype(BF16), vb) + q_dec * _dot(qb, s_h.astype(BF16))
        s_scr[h] = s_dec * s_h + _dot_tn((k * k_dec).astype(BF16), vb)
        on = o * lax.rsqrt(jnp.mean(o * o, axis=-1, keepdims=True) + EPS) * onw_ref[h:h + 1, :]
        gate = p_ref[0, :, 2 * qk_w + heads * dv + h * dv:2 * qk_w + heads * dv + (h + 1) * dv]
        o_ref[0, :, h * dv:(h + 1) * dv] = on * _silu(gate)

    @pl.when(t == pl.num_programs(1) - 1)
    def _():
        s_ref[0] = s_scr[...]


def _ret_core(p, cos, sin, s0, onorm_w, c):
    b, l, pw = p.shape
    heads, dk, dv = s0.shape[1:]
    half = dk // 2
    assert l % c == 0 and c % SUBLANES == 0 and half == LANES
    assert pw == 2 * heads * dk + 2 * heads * dv
    kern = functools.partial(_ret_core_kernel, heads=heads, dk=dk, dv=dv)
    return pl.pallas_call(
        kern, grid=(b, l // c),
        in_specs=[pl.BlockSpec((1, c, pw), lambda i, t: (i, t, 0)),
                  pl.BlockSpec((c, half), lambda i, t: (t, 0)),
                  pl.BlockSpec((c, half), lambda i, t: (t, 0)),
                  pl.BlockSpec((1, heads, dk, dv), lambda i, t: (i, 0, 0, 0)),
                  pl.BlockSpec((heads, dv), lambda i, t: (0, 0))],
        out_specs=[pl.BlockSpec((1, c, heads * dv), lambda i, t: (i, t, 0)),
                   pl.BlockSpec((1, heads, dk, dv), lambda i, t: (i, 0, 0, 0))],
        out_shape=[jax.ShapeDtypeStruct((b, l, heads * dv), F32),
                   jax.ShapeDtypeStruct(s0.shape, F32)],
        scratch_shapes=[pltpu.VMEM((heads, dk, dv), F32)],
        compiler_params=_params("parallel", "arbitrary"), name="ret_core",
    )(p, cos, sin, s0, onorm_w)


def _chunk_size(l):
    return l if l <= CHUNK else CHUNK


def _row_tile(t):
    for tm in (256, 128, 64, 32, 16, 8):
        if t % tm == 0:
            return tm
    raise ValueError(f"token count {t} is not a multiple of {SUBLANES}")


def _rope_tables(l, half, pos0):
    inv = 1.0 / (ROPE_BASE ** jnp.linspace(0.0, 1.0, half, dtype=F32))
    pos = pos0 + jnp.arange(l, dtype=F32)
    ang = pos[:, None] * inv[None, :]
    return jnp.cos(ang), jnp.sin(ang)


def _trunk(x, pos0, gdn_s, gdn_conv, ret_s, norm_w, w_in_a, conv_w_a, a_log_a, dt_bias_a,
           onorm_a, w_out_a, w_in_b, onorm_b, w_out_b, final_norm_w):
    b, l, d = x.shape
    assert l <= CHUNK or l % CHUNK == 0
    c = _chunk_size(l)
    t = b * l
    tm = _row_tile(t)
    heads_a, dk_a, dv_a = gdn_s.shape[2:]
    main_a = 2 * heads_a * dk_a + 2 * heads_a * dv_a
    x2 = x.reshape(t, d)

    w_main = w_in_a[0][:, :main_a].astype(BF16)
    w_tail = jnp.zeros((d, 2 * LANES), F32)
    w_tail = w_tail.at[:, 0:heads_a].set(w_in_a[0][:, main_a:main_a + heads_a])
    w_tail = w_tail.at[:, LANES:LANES + heads_a].set(w_in_a[0][:, main_a + heads_a:])
    p, ba = _in_proj(x2, norm_w[0], [w_main, w_tail.astype(BF16)], tm)
    o, s_a, c_a = _gdn_core(p.reshape(b, l, main_a), ba.reshape(b, l, 2 * LANES), gdn_s[0],
                            gdn_conv[0], conv_w_a[0], a_log_a[0], dt_bias_a[0], onorm_a[0], c)
    x2 = _out_proj(o.reshape(t, -1), x2, w_out_a[0].astype(BF16), final_norm_w, tm, False)

    heads_b, dk_b, dv_b = ret_s.shape[2:]
    (p,) = _in_proj(x2, norm_w[1], [w_in_b[0].astype(BF16)], tm)
    cos, sin = _rope_tables(l, dk_b // 2, pos0)
    o, s_b = _ret_core(p.reshape(b, l, -1), cos, sin, ret_s[0], onorm_b[0], c)
    y = _out_proj(o.reshape(t, -1), x2, w_out_b[0].astype(BF16), final_norm_w, tm, True)
    return y.reshape(b, l, d), s_a[None], c_a[None], s_b[None]


def kernel(x_prompt, x_sample, state_gdn_ssm, state_gdn_conv, state_ret, norm_w, w_in_a,
           conv_w_a, a_log_a, dt_bias_a, onorm_a, w_out_a, w_in_b, onorm_b, w_out_b,
           final_norm_w):
    assert state_gdn_ssm.shape[0] == 1 and state_ret.shape[0] == 1 and norm_w.shape[0] == 2
    bp = x_prompt.shape[0]
    weights = (norm_w, w_in_a, conv_w_a, a_log_a, dt_bias_a, onorm_a, w_out_a, w_in_b, onorm_b,
               w_out_b, final_norm_w)
    z_sa = jnp.zeros((1, bp) + state_gdn_ssm.shape[2:], F32)
    z_ca = jnp.zeros((1, bp) + state_gdn_conv.shape[2:], F32)
    z_sb = jnp.zeros((1, bp) + state_ret.shape[2:], F32)
    past_len = 16384.0
    y_p, sa_p, ca_p, sb_p = _trunk(x_prompt, 0.0, z_sa, z_ca, z_sb, *weights)
    y_s, sa_s, ca_s, sb_s = _trunk(x_sample, past_len, state_gdn_ssm, state_gdn_conv, state_ret,
                                   *weights)
    return (y_p, y_s, sa_p, ca_p, sb_p, sa_s, ca_s, sb_s)
```

```python
import functools

import jax
import jax.numpy as jnp
from jax import lax
from jax.experimental import pallas as pl
from jax.experimental.pallas import tpu as pltpu

F32 = jnp.float32
BF16 = jnp.bfloat16
EPS = 1e-6
CHUNK = 64
CONV_W = 4
ROPE_BASE = 10000.0
PAST_LEN = 16384.0
LANES = 128
SUBLANES = 8
VMEM_LIMIT_BYTES = 56 * 1024 * 1024
GDN_CHUNKS_PER_STEP = 4


def _dot(a, b):
    return jnp.dot(a, b, preferred_element_type=F32)


def _dot_nt(a, b):
    return lax.dot_general(a, b, (((1,), (1,)), ((), ())), preferred_element_type=F32)


def _dot_tn(a, b):
    return lax.dot_general(a, b, (((0,), (0,)), ((), ())), preferred_element_type=F32)


def _split3(x):
    hi = x.astype(BF16)
    r = x - hi.astype(F32)
    mid = r.astype(BF16)
    lo = (r - mid.astype(F32)).astype(BF16)
    return hi, mid, lo


def _split2(x):
    hi = x.astype(BF16)
    return hi, (x - hi.astype(F32)).astype(BF16)


def _dot_x3(a, b):
    a_hi, a_lo = _split2(a)
    b_hi, b_lo = _split2(b)
    return _dot(a_hi, b_hi) + (_dot(a_hi, b_lo) + _dot(a_lo, b_hi))


def _sigmoid(x):
    return 1.0 / (1.0 + jnp.exp(-x))


def _silu(x):
    return x * _sigmoid(x)


def _softplus(x):
    return jnp.maximum(x, 0.0) + jnp.log1p(jnp.exp(-jnp.abs(x)))


def _params(*sem):
    return pltpu.CompilerParams(dimension_semantics=sem, vmem_limit_bytes=VMEM_LIMIT_BYTES)


def _in_proj_kernel(x_ref, nw_ref, *refs):
    n = len(refs) // 2
    w_refs, o_refs = refs[:n], refs[n:]
    x = x_ref[...]
    ms = jnp.mean(x * x, axis=-1, keepdims=True)
    xn = (x * lax.rsqrt(ms + EPS) * nw_ref[...]).astype(BF16)
    for w_ref, o_ref in zip(w_refs, o_refs):
        o_ref[...] = _dot(xn, w_ref[...])


def _in_proj(x, norm_w, weights, tm):
    t, d = x.shape
    assert t % tm == 0
    in_specs = [pl.BlockSpec((tm, d), lambda i: (i, 0)),
                pl.BlockSpec((1, d), lambda i: (0, 0))]
    in_specs += [pl.BlockSpec(w.shape, lambda i: (0, 0)) for w in weights]
    out_specs = [pl.BlockSpec((tm, w.shape[1]), lambda i: (i, 0)) for w in weights]
    out_shape = [jax.ShapeDtypeStruct((t, w.shape[1]), F32) for w in weights]
    return pl.pallas_call(
        _in_proj_kernel, grid=(t // tm,), in_specs=in_specs, out_specs=out_specs,
        out_shape=out_shape, compiler_params=_params("parallel"), name="in_proj",
    )(x, norm_w.reshape(1, d), *weights)


def _out_proj_kernel(o_ref, x_ref, w_ref, fw_ref, y_ref, *, final_norm):
    y = x_ref[...] + _dot(o_ref[...].astype(BF16), w_ref[...])
    if final_norm:
        ms = jnp.mean(y * y, axis=-1, keepdims=True)
        y = y * lax.rsqrt(ms + EPS) * fw_ref[...]
    y_ref[...] = y


def _out_proj(o, x, w, final_w, tm, final_norm):
    t, k = o.shape
    d = x.shape[1]
    assert t % tm == 0
    return pl.pallas_call(
        functools.partial(_out_proj_kernel, final_norm=final_norm),
        grid=(t // tm,),
        in_specs=[pl.BlockSpec((tm, k), lambda i: (i, 0)),
                  pl.BlockSpec((tm, d), lambda i: (i, 0)),
                  pl.BlockSpec((k, d), lambda i: (0, 0)),
                  pl.BlockSpec((1, d), lambda i: (0, 0))],
        out_specs=pl.BlockSpec((tm, d), lambda i: (i, 0)),
        out_shape=jax.ShapeDtypeStruct((t, d), F32),
        compiler_params=_params("parallel"), name="out_proj",
    )(o, x, w, final_w.reshape(1, d))


def _unit_lower_inverses(lows, eye, cl):
    powers = [(-low).astype(BF16) for low in lows]
    invs = [eye - low for low in lows]
    p = 1
    while 2 * p < cl:
        sq = [_dot(pw, pw) for pw in powers]
        p *= 2
        powers = [s.astype(BF16) for s in sq]
        invs = [inv + _dot(inv.astype(BF16), pw) for inv, pw in zip(invs, powers)]
    return invs


def _gdn_core_kernel(p_ref, ba_ref, s0_ref, c0_ref, cw_ref, alog_ref, dtb_ref, onw_ref,
                     o_ref, s_ref, c_ref, s_scr, xe_scr, *, heads, dk, dv, nb, cl, nc):
    c = nb * cl
    seq_rows = cl * nc
    qk_w = heads * dk
    conv_ch = 2 * qk_w + heads * dv
    hist0 = SUBLANES - (CONV_W - 1)
    t = pl.program_id(1)

    @pl.when(t == 0)
    def _():
        s_scr[...] = s0_ref[...]
        for j in range(nb):
            xe_scr[j, hist0:SUBLANES, :] = c0_ref[j]

    for j in range(nb):
        xe_scr[j, SUBLANES:SUBLANES + seq_rows, :] = p_ref[0, j * seq_rows:(j + 1) * seq_rows,
                                                            0:conv_ch]

    row = lax.broadcasted_iota(jnp.int32, (c, c), 0)
    col = lax.broadcasted_iota(jnp.int32, (c, c), 1)
    if nb > 1:
        shift = cl.bit_length() - 1
        same = lax.shift_right_logical(row, shift) == lax.shift_right_logical(col, shift)
        incl = same & (row >= col)
        strict = same & (row > col)
    else:
        same = row >= 0
        incl = row >= col
        strict = row > col
    eye = (row == col).astype(F32)
    cum_masks = jnp.concatenate([incl.astype(BF16), same.astype(BF16)], axis=0)
    sel = (lax.broadcasted_iota(jnp.int32, (SUBLANES, LANES), 0)
           == lax.broadcasted_iota(jnp.int32, (SUBLANES, LANES), 1)).astype(BF16)
    neg_a = -jnp.exp(alog_ref[...])

    def chunk_rows(ref, ci, lo, hi):
        parts = [ref[0, j * seq_rows + ci * cl:j * seq_rows + (ci + 1) * cl, lo:hi]
                 for j in range(nb)]
        return parts[0] if nb == 1 else jnp.concatenate(parts, axis=0)

    units = [(ci, h) for ci in range(nc) for h in range(heads)]
    seqs = range(nb)

    def stack(parts):
        return parts[0] if nb == 1 else jnp.concatenate(parts, axis=0)

    mixed, beta, gcum, gtot, gcum_t = [], [], [], [], []
    for ci in range(nc):
        parts = []
        for j in seqs:
            base = hist0 + ci * cl
            acc = xe_scr[j, base:base + cl, :] * cw_ref[0:1, :]
            for tap in range(1, CONV_W):
                acc = acc + xe_scr[j, base + tap:base + tap + cl, :] * cw_ref[tap:tap + 1, :]
            parts.append(acc)
        mixed.append(_silu(stack(parts)))
        beta.append(_sigmoid(chunk_rows(ba_ref, ci, 0, LANES)))
        g = neg_a * _softplus(chunk_rows(ba_ref, ci, LANES, 2 * LANES) + dtb_ref[...])
        g3 = _split3(g)
        gg = _dot(cum_masks, g3[0]) + (_dot(cum_masks, g3[1]) + _dot(cum_masks, g3[2]))
        gcum.append(gg[:c])
        gtot.append(gg[c:])
        gc3 = _split3(gg[:c])
        gcum_t.append(_dot_nt(sel, gc3[0]) + (_dot_nt(sel, gc3[1]) + _dot_nt(sel, gc3[2])))

    q_l, eg_l, lows, rhs_l, qk_l, kd_l = [], [], [], [], [], []
    for ci, h in units:
        q = mixed[ci][:, h * dk:(h + 1) * dk]
        k = mixed[ci][:, qk_w + h * dk:qk_w + (h + 1) * dk]
        v = mixed[ci][:, 2 * qk_w + h * dv:2 * qk_w + (h + 1) * dv]
        q = q * lax.rsqrt(jnp.sum(q * q, axis=-1, keepdims=True) + EPS) * (dk ** -0.5)
        k = k * lax.rsqrt(jnp.sum(k * k, axis=-1, keepdims=True) + EPS)
        gc = gcum[ci][:, h:h + 1]
        gr = gcum_t[ci][h:h + 1, :]
        bh = beta[ci][:, h:h + 1]
        dec = jnp.where(incl, jnp.exp(jnp.where(incl, gc - gr, 0.0)), 0.0)
        eg = jnp.exp(gc)
        qb = q.astype(BF16)
        kb = k.astype(BF16)
        lows.append(jnp.where(strict, bh * _dot_nt(kb, kb) * dec, 0.0))
        qk_l.append((_dot_nt(qb, kb) * dec).astype(BF16))
        rhs_l.append(jnp.concatenate([bh * v, (bh * eg) * k], axis=-1).astype(BF16))
        kd_l.append(k * jnp.exp(gtot[ci][:, h:h + 1] - gc))
        q_l.append(q)
        eg_l.append(eg)
    invs = _unit_lower_inverses(lows, eye, cl)
    sols = [_dot(inv.astype(BF16), rhs) for inv, rhs in zip(invs, rhs_l)]

    for ci in range(nc):
        idx = [ci * heads + h for h in range(heads)]
        s_old = [[s_scr[j, h] for j in seqs] for h in range(heads)]
        res = [[_dot(jnp.concatenate([sols[i][j * cl:(j + 1) * cl, dv:],
                                      q_l[i][j * cl:(j + 1) * cl]], axis=0).astype(BF16),
                     s_old[h][j].astype(BF16)) for j in seqs]
               for h, i in enumerate(idx)]
        u = [stack([sols[i][j * cl:(j + 1) * cl, :dv] - res[h][j][:cl] for j in seqs])
             for h, i in enumerate(idx)]
        o = [eg_l[i] * stack([res[h][j][cl:] for j in seqs]) + _dot(qk_l[i], u[h].astype(BF16))
             for h, i in enumerate(idx)]
        for h, i in enumerate(idx):
            for j in seqs:
                r0, r1 = j * cl, (j + 1) * cl
                s_scr[j, h] = (jnp.exp(gtot[ci][r0:r0 + 1, h:h + 1]) * s_old[h][j]
                               + _dot_tn(kd_l[i][r0:r1].astype(BF16), u[h][r0:r1].astype(BF16)))
        for h in range(heads):
            on = (o[h] * lax.rsqrt(jnp.mean(o[h] * o[h], axis=-1, keepdims=True) + EPS)
                  * onw_ref[...])
            out = on * _silu(chunk_rows(p_ref, ci, conv_ch + h * dv, conv_ch + (h + 1) * dv))
            for j in seqs:
                o_ref[0, j * seq_rows + ci * cl:j * seq_rows + (ci + 1) * cl,
                      h * dv:(h + 1) * dv] = out[j * cl:(j + 1) * cl]

    new_hist = [xe_scr[j, seq_rows + hist0:seq_rows + SUBLANES, :] for j in range(nb)]
    for j in range(nb):
        xe_scr[j, hist0:SUBLANES, :] = new_hist[j]

    @pl.when(t == pl.num_programs(1) - 1)
    def _():
        s_ref[...] = s_scr[...]
        for j in range(nb):
            c_ref[j] = new_hist[j]


def _gdn_core(p, ba, s0, c0, conv_w, a_log, dt_bias, onorm_w):
    b, l, pw = p.shape
    heads, dk, dv = s0.shape[1:]
    conv_ch = c0.shape[2]
    assert heads <= SUBLANES and dk == LANES and dv == LANES and pw == conv_ch + heads * dv
    if l <= CHUNK:
        assert l % SUBLANES == 0 and l & (l - 1) == 0
        cl, nc = l, 1
        nb = max(n for n in range(1, CHUNK // l + 1) if b % n == 0)
    else:
        assert l % CHUNK == 0
        cl, nb = CHUNK, 1
        nc = GDN_CHUNKS_PER_STEP if l % (CHUNK * GDN_CHUNKS_PER_STEP) == 0 else 1
    groups, rows = b // nb, nb * cl * nc
    steps = l // (cl * nc)

    def lane_pad(vec):
        return jnp.pad(vec.astype(F32), (0, LANES - vec.shape[0])).reshape(1, LANES)

    kern = functools.partial(_gdn_core_kernel, heads=heads, dk=dk, dv=dv, nb=nb, cl=cl, nc=nc)
    o, s, cs = pl.pallas_call(
        kern, grid=(groups, steps),
        in_specs=[pl.BlockSpec((1, rows, pw), lambda i, t: (i, t, 0)),
                  pl.BlockSpec((1, rows, 2 * LANES), lambda i, t: (i, t, 0)),
                  pl.BlockSpec((nb, heads, dk, dv), lambda i, t: (i, 0, 0, 0)),
                  pl.BlockSpec((nb, CONV_W - 1, conv_ch), lambda i, t: (i, 0, 0)),
                  pl.BlockSpec((CONV_W, conv_ch), lambda i, t: (0, 0)),
                  pl.BlockSpec((1, LANES), lambda i, t: (0, 0)),
                  pl.BlockSpec((1, LANES), lambda i, t: (0, 0)),
                  pl.BlockSpec((1, dv), lambda i, t: (0, 0))],
        out_specs=[pl.BlockSpec((1, rows, heads * dv), lambda i, t: (i, t, 0)),
                   pl.BlockSpec((nb, heads, dk, dv), lambda i, t: (i, 0, 0, 0)),
                   pl.BlockSpec((nb, CONV_W - 1, conv_ch), lambda i, t: (i, 0, 0))],
        out_shape=[jax.ShapeDtypeStruct((groups, nb * l, heads * dv), F32),
                   jax.ShapeDtypeStruct(s0.shape, F32),
                   jax.ShapeDtypeStruct(c0.shape, F32)],
        scratch_shapes=[pltpu.VMEM((nb, heads, dk, dv), F32),
                        pltpu.VMEM((nb, SUBLANES + cl * nc, conv_ch), F32)],
        compiler_params=_params("parallel", "arbitrary"), name="gdn_core",
    )(p.reshape(groups, nb * l, pw), ba.reshape(groups, nb * l, 2 * LANES), s0, c0, conv_w,
      lane_pad(a_log), lane_pad(dt_bias), onorm_w.reshape(1, dv))
    return o.reshape(b, l, heads * dv), s, cs


def _ret_core_kernel(p_ref, cos_ref, sin_ref, s0_ref, onw_ref, o_ref, s_ref, s_scr,
                     *, heads, dk, dv):
    c = p_ref.shape[1]
    qk_w = heads * dk
    half = dk // 2
    t = pl.program_id(1)

    @pl.when(t == 0)
    def _():
        s_scr[...] = s0_ref[0]

    cos = cos_ref[...]
    sin = sin_ref[...]
    row = lax.broadcasted_iota(jnp.int32, (c, c), 0)
    col = lax.broadcasted_iota(jnp.int32, (c, c), 1)
    diff = (row - col).astype(F32)
    idx = lax.broadcasted_iota(jnp.int32, (c, 1), 0).astype(F32)

    def rotary(x):
        x1, x2 = x[:, :half], x[:, half:]
        return jnp.concatenate([x1 * cos - x2 * sin, x1 * sin + x2 * cos], axis=-1)

    for h in range(heads):
        lg = jnp.log(jnp.full((1, 1), 1.0 - 2.0 ** (-5.0 - h), F32))
        decay = jnp.where(diff >= 0, jnp.exp(lg * jnp.maximum(diff, 0.0)), 0.0)
        q_dec = jnp.exp(lg * (idx + 1.0))
        k_dec = jnp.exp(lg * (c - 1.0 - idx))
        s_dec = jnp.exp(lg * c)
        q = rotary(p_ref[0, :, h * dk:(h + 1) * dk])
        k = rotary(p_ref[0, :, qk_w + h * dk:qk_w + (h + 1) * dk]) * (dk ** -0.5)
        v = p_ref[0, :, 2 * qk_w + h * dv:2 * qk_w + (h + 1) * dv]
        qb = q.astype(BF16)
        kb = k.astype(BF16)
        vb = v.astype(BF16)
        s_h = s_scr[h]
        qk = _dot_nt(qb, kb) * decay
        o = _dot(qk.astype(BF16), vb) + q_dec * _dot(qb, s_h.astype(BF16))
        s_scr[h] = s_dec * s_h + _dot_tn((k * k_dec).astype(BF16), vb)
        on = o * lax.rsqrt(jnp.mean(o * o, axis=-1, keepdims=True) + EPS) * onw_ref[h:h + 1, :]
        gate = p_ref[0, :, 2 * qk_w + heads * dv + h * dv:2 * qk_w + heads * dv + (h + 1) * dv]
        o_ref[0, :, h * dv:(h + 1) * dv] = on * _silu(gate)

    @pl.when(t == pl.num_programs(1) - 1)
    def _():
        s_ref[0] = s_scr[...]


def _ret_core(p, cos, sin, s0, onorm_w, c):
    b, l, pw = p.shape
    heads, dk, dv = s0.shape[1:]
    half = dk // 2
    assert l % c == 0 and c % SUBLANES == 0 and half == LANES
    assert pw == 2 * heads * dk + 2 * heads * dv
    kern = functools.partial(_ret_core_kernel, heads=heads, dk=dk, dv=dv)
    return pl.pallas_call(
        kern, grid=(b, l // c),
        in_specs=[pl.BlockSpec((1, c, pw), lambda i, t: (i, t, 0)),
                  pl.BlockSpec((c, half), lambda i, t: (t, 0)),
                  pl.BlockSpec((c, half), lambda i, t: (t, 0)),
                  pl.BlockSpec((1, heads, dk, dv), lambda i, t: (i, 0, 0, 0)),
                  pl.BlockSpec((heads, dv), lambda i, t: (0, 0))],
        out_specs=[pl.BlockSpec((1, c, heads * dv), lambda i, t: (i, t, 0)),
                   pl.BlockSpec((1, heads, dk, dv), lambda i, t: (i, 0, 0, 0))],
        out_shape=[jax.ShapeDtypeStruct((b, l, heads * dv), F32),
                   jax.ShapeDtypeStruct(s0.shape, F32)],
        scratch_shapes=[pltpu.VMEM((heads, dk, dv), F32)],
        compiler_params=_params("parallel", "arbitrary"), name="ret_core",
    )(p, cos, sin, s0, onorm_w)


def _chunk_size(l):
    return l if l <= CHUNK else CHUNK


def _row_tile(t):
    for tm in (256, 128, 64, 32, 16, 8):
        if t % tm == 0:
            return tm
    raise ValueError(f"token count {t} is not a multiple of {SUBLANES}")


def _rope_tables(l, half, pos0):
    inv = 1.0 / (ROPE_BASE ** jnp.linspace(0.0, 1.0, half, dtype=F32))
    pos = pos0 + jnp.arange(l, dtype=F32)
    ang = pos[:, None] * inv[None, :]
    return jnp.cos(ang), jnp.sin(ang)


def _trunk(x, pos0, gdn_s, gdn_conv, ret_s, norm_w, w_in_a, conv_w_a, a_log_a, dt_bias_a,
           onorm_a, w_out_a, w_in_b, onorm_b, w_out_b, final_norm_w):
    b, l, d = x.shape
    assert l <= CHUNK or l % CHUNK == 0
    c = _chunk_size(l)
    t = b * l
    tm = _row_tile(t)
    heads_a, dk_a, dv_a = gdn_s.shape[2:]
    main_a = 2 * heads_a * dk_a + 2 * heads_a * dv_a
    x2 = x.reshape(t, d)

    w_main = w_in_a[0][:, :main_a].astype(BF16)
    w_tail = jnp.zeros((d, 2 * LANES), F32)
    w_tail = w_tail.at[:, 0:heads_a].set(w_in_a[0][:, main_a:main_a + heads_a])
    w_tail = w_tail.at[:, LANES:LANES + heads_a].set(w_in_a[0][:, main_a + heads_a:])
    p, ba = _in_proj(x2, norm_w[0], [w_main, w_tail.astype(BF16)], tm)
    o, s_a, c_a = _gdn_core(p.reshape(b, l, main_a), ba.reshape(b, l, 2 * LANES), gdn_s[0],
                            gdn_conv[0], conv_w_a[0], a_log_a[0], dt_bias_a[0], onorm_a[0])
    x2 = _out_proj(o.reshape(t, -1), x2, w_out_a[0].astype(BF16), final_norm_w, tm, False)

    heads_b, dk_b, dv_b = ret_s.shape[2:]
    (p,) = _in_proj(x2, norm_w[1], [w_in_b[0].astype(BF16)], tm)
    cos, sin = _rope_tables(l, dk_b // 2, pos0)
    o, s_b = _ret_core(p.reshape(b, l, -1), cos, sin, ret_s[0], onorm_b[0], c)
    y = _out_proj(o.reshape(t, -1), x2, w_out_b[0].astype(BF16), final_norm_w, tm, True)
    return y.reshape(b, l, d), s_a[None], c_a[None], s_b[None]


def kernel(x_prompt, x_sample, state_gdn_ssm, state_gdn_conv, state_ret, norm_w, w_in_a,
           conv_w_a, a_log_a, dt_bias_a, onorm_a, w_out_a, w_in_b, onorm_b, w_out_b,
           final_norm_w):
    assert state_gdn_ssm.shape[0] == 1 and state_ret.shape[0] == 1 and norm_w.shape[0] == 2
    bp = x_prompt.shape[0]
    weights = (norm_w, w_in_a, conv_w_a, a_log_a, dt_bias_a, onorm_a, w_out_a, w_in_b, onorm_b,
               w_out_b, final_norm_w)
    z_sa = jnp.zeros((1, bp) + state_gdn_ssm.shape[2:], F32)
    z_ca = jnp.zeros((1, bp) + state_gdn_conv.shape[2:], F32)
    z_sb = jnp.zeros((1, bp) + state_ret.shape[2:], F32)
    y_p, sa_p, ca_p, sb_p = _trunk(x_prompt, 0.0, z_sa, z_ca, z_sb, *weights)
    y_s, sa_s, ca_s, sb_s = _trunk(x_sample, PAST_LEN, state_gdn_ssm, state_gdn_conv, state_ret,
                                   *weights)
    return (y_p, y_s, sa_p, ca_p, sb_p, sa_s, ca_s, sb_s)
```

```python
import functools

import jax
import jax.numpy as jnp
from jax import lax
from jax.experimental import pallas as pl
from jax.experimental.pallas import tpu as pltpu

F32 = jnp.float32
BF16 = jnp.bfloat16
EPS = 1e-6
CHUNK = 64
CONV_W = 4
ROPE_BASE = 10000.0
PAST_LEN = 16384.0
LANES = 128
SUBLANES = 8
VMEM_LIMIT_BYTES = 56 * 1024 * 1024
GDN_CHUNKS_PER_STEP = 4
RET_CHUNK = 256
RET_SEQS_PER_STEP = 2


def _dot(a, b):
    return jnp.dot(a, b, preferred_element_type=F32)


def _dot_nt(a, b):
    return lax.dot_general(a, b, (((1,), (1,)), ((), ())), preferred_element_type=F32)


def _dot_tn(a, b):
    return lax.dot_general(a, b, (((0,), (0,)), ((), ())), preferred_element_type=F32)


def _split3(x):
    hi = x.astype(BF16)
    r = x - hi.astype(F32)
    mid = r.astype(BF16)
    lo = (r - mid.astype(F32)).astype(BF16)
    return hi, mid, lo


def _split2(x):
    hi = x.astype(BF16)
    return hi, (x - hi.astype(F32)).astype(BF16)


def _dot_x3(a, b):
    a_hi, a_lo = _split2(a)
    b_hi, b_lo = _split2(b)
    return _dot(a_hi, b_hi) + (_dot(a_hi, b_lo) + _dot(a_lo, b_hi))


def _sigmoid(x):
    return 1.0 / (1.0 + jnp.exp(-x))


def _silu(x):
    return x * _sigmoid(x)


def _softplus(x):
    return jnp.maximum(x, 0.0) + jnp.log1p(jnp.exp(-jnp.abs(x)))


def _params(*sem):
    return pltpu.CompilerParams(dimension_semantics=sem, vmem_limit_bytes=VMEM_LIMIT_BYTES)


def _in_proj_kernel(x_ref, nw_ref, *refs):
    n = len(refs) // 2
    w_refs, o_refs = refs[:n], refs[n:]
    x = x_ref[...]
    ms = jnp.mean(x * x, axis=-1, keepdims=True)
    xn = (x * lax.rsqrt(ms + EPS) * nw_ref[...]).astype(BF16)
    for w_ref, o_ref in zip(w_refs, o_refs):
        o_ref[...] = _dot(xn, w_ref[...])


def _in_proj(x, norm_w, weights, tm):
    t, d = x.shape
    assert t % tm == 0
    in_specs = [pl.BlockSpec((tm, d), lambda i: (i, 0)),
                pl.BlockSpec((1, d), lambda i: (0, 0))]
    in_specs += [pl.BlockSpec(w.shape, lambda i: (0, 0)) for w in weights]
    out_specs = [pl.BlockSpec((tm, w.shape[1]), lambda i: (i, 0)) for w in weights]
    out_shape = [jax.ShapeDtypeStruct((t, w.shape[1]), F32) for w in weights]
    return pl.pallas_call(
        _in_proj_kernel, grid=(t // tm,), in_specs=in_specs, out_specs=out_specs,
        out_shape=out_shape, compiler_params=_params("parallel"), name="in_proj",
    )(x, norm_w.reshape(1, d), *weights)


def _out_proj_kernel(o_ref, x_ref, w_ref, fw_ref, y_ref, *, final_norm):
    y = x_ref[...] + _dot(o_ref[...].astype(BF16), w_ref[...])
    if final_norm:
        ms = jnp.mean(y * y, axis=-1, keepdims=True)
        y = y * lax.rsqrt(ms + EPS) * fw_ref[...]
    y_ref[...] = y


def _out_proj(o, x, w, final_w, tm, final_norm):
    t, k = o.shape
    d = x.shape[1]
    assert t % tm == 0
    return pl.pallas_call(
        functools.partial(_out_proj_kernel, final_norm=final_norm),
        grid=(t // tm,),
        in_specs=[pl.BlockSpec((tm, k), lambda i: (i, 0)),
                  pl.BlockSpec((tm, d), lambda i: (i, 0)),
                  pl.BlockSpec((k, d), lambda i: (0, 0)),
                  pl.BlockSpec((1, d), lambda i: (0, 0))],
        out_specs=pl.BlockSpec((tm, d), lambda i: (i, 0)),
        out_shape=jax.ShapeDtypeStruct((t, d), F32),
        compiler_params=_params("parallel"), name="out_proj",
    )(o, x, w, final_w.reshape(1, d))


def _unit_lower_inverses(lows, eye, cl):
    powers = [(-low).astype(BF16) for low in lows]
    invs = [eye - low for low in lows]
    p = 1
    while 2 * p < cl:
        sq = [_dot(pw, pw) for pw in powers]
        p *= 2
        powers = [s.astype(BF16) for s in sq]
        invs = [inv + _dot(inv.astype(BF16), pw) for inv, pw in zip(invs, powers)]
    return invs


def _gdn_core_kernel(p_ref, ba_ref, s0_ref, c0_ref, cw_ref, alog_ref, dtb_ref, onw_ref,
                     o_ref, s_ref, c_ref, s_scr, xe_scr, *, heads, dk, dv, nb, cl, nc):
    c = nb * cl
    seq_rows = cl * nc
    qk_w = heads * dk
    conv_ch = 2 * qk_w + heads * dv
    hist0 = SUBLANES - (CONV_W - 1)
    t = pl.program_id(1)

    @pl.when(t == 0)
    def _():
        s_scr[...] = s0_ref[...]
        for j in range(nb):
            xe_scr[j, hist0:SUBLANES, :] = c0_ref[j]

    for j in range(nb):
        xe_scr[j, SUBLANES:SUBLANES + seq_rows, :] = p_ref[0, j * seq_rows:(j + 1) * seq_rows,
                                                            0:conv_ch]

    row = lax.broadcasted_iota(jnp.int32, (c, c), 0)
    col = lax.broadcasted_iota(jnp.int32, (c, c), 1)
    if nb > 1:
        shift = cl.bit_length() - 1
        same = lax.shift_right_logical(row, shift) == lax.shift_right_logical(col, shift)
        incl = same & (row >= col)
        strict = same & (row > col)
    else:
        same = row >= 0
        incl = row >= col
        strict = row > col
    eye = (row == col).astype(F32)
    cum_masks = jnp.concatenate([incl.astype(BF16), same.astype(BF16)], axis=0)
    sel = (lax.broadcasted_iota(jnp.int32, (SUBLANES, LANES), 0)
           == lax.broadcasted_iota(jnp.int32, (SUBLANES, LANES), 1)).astype(BF16)
    neg_a = -jnp.exp(alog_ref[...])

    def chunk_rows(ref, ci, lo, hi):
        parts = [ref[0, j * seq_rows + ci * cl:j * seq_rows + (ci + 1) * cl, lo:hi]
                 for j in range(nb)]
        return parts[0] if nb == 1 else jnp.concatenate(parts, axis=0)

    units = [(ci, h) for ci in range(nc) for h in range(heads)]
    seqs = range(nb)

    def stack(parts):
        return parts[0] if nb == 1 else jnp.concatenate(parts, axis=0)

    mixed, beta, gcum, gtot, gcum_t = [], [], [], [], []
    for ci in range(nc):
        parts = []
        for j in seqs:
            base = hist0 + ci * cl
            acc = xe_scr[j, base:base + cl, :] * cw_ref[0:1, :]
            for tap in range(1, CONV_W):
                acc = acc + xe_scr[j, base + tap:base + tap + cl, :] * cw_ref[tap:tap + 1, :]
            parts.append(acc)
        mixed.append(_silu(stack(parts)))
        beta.append(_sigmoid(chunk_rows(ba_ref, ci, 0, LANES)))
        g = neg_a * _softplus(chunk_rows(ba_ref, ci, LANES, 2 * LANES) + dtb_ref[...])
        g3 = _split3(g)
        gg = _dot(cum_masks, g3[0]) + (_dot(cum_masks, g3[1]) + _dot(cum_masks, g3[2]))
        gcum.append(gg[:c])
        gtot.append(gg[c:])
        gc3 = _split3(gg[:c])
        gcum_t.append(_dot_nt(sel, gc3[0]) + (_dot_nt(sel, gc3[1]) + _dot_nt(sel, gc3[2])))

    q_l, eg_l, lows, rhs_l, qk_l, kd_l = [], [], [], [], [], []
    for ci, h in units:
        q = mixed[ci][:, h * dk:(h + 1) * dk]
        k = mixed[ci][:, qk_w + h * dk:qk_w + (h + 1) * dk]
        v = mixed[ci][:, 2 * qk_w + h * dv:2 * qk_w + (h + 1) * dv]
        q = q * lax.rsqrt(jnp.sum(q * q, axis=-1, keepdims=True) + EPS) * (dk ** -0.5)
        k = k * lax.rsqrt(jnp.sum(k * k, axis=-1, keepdims=True) + EPS)
        gc = gcum[ci][:, h:h + 1]
        gr = gcum_t[ci][h:h + 1, :]
        bh = beta[ci][:, h:h + 1]
        dec = jnp.where(incl, jnp.exp(jnp.where(incl, gc - gr, 0.0)), 0.0)
        eg = jnp.exp(gc)
        qb = q.astype(BF16)
        kb = k.astype(BF16)
        lows.append(jnp.where(strict, bh * _dot_nt(kb, kb) * dec, 0.0))
        qk_l.append((_dot_nt(qb, kb) * dec).astype(BF16))
        rhs_l.append(jnp.concatenate([bh * v, (bh * eg) * k], axis=-1).astype(BF16))
        kd_l.append(k * jnp.exp(gtot[ci][:, h:h + 1] - gc))
        q_l.append(q)
        eg_l.append(eg)
    invs = _unit_lower_inverses(lows, eye, cl)
    sols = [_dot(inv.astype(BF16), rhs) for inv, rhs in zip(invs, rhs_l)]

    for ci in range(nc):
        idx = [ci * heads + h for h in range(heads)]
        s_old = [[s_scr[j, h] for j in seqs] for h in range(heads)]
        res = [[_dot(jnp.concatenate([sols[i][j * cl:(j + 1) * cl, dv:],
                                      q_l[i][j * cl:(j + 1) * cl]], axis=0).astype(BF16),
                     s_old[h][j].astype(BF16)) for j in seqs]
               for h, i in enumerate(idx)]
        u = [stack([sols[i][j * cl:(j + 1) * cl, :dv] - res[h][j][:cl] for j in seqs])
             for h, i in enumerate(idx)]
        o = [eg_l[i] * stack([res[h][j][cl:] for j in seqs]) + _dot(qk_l[i], u[h].astype(BF16))
             for h, i in enumerate(idx)]
        for h, i in enumerate(idx):
            for j in seqs:
                r0, r1 = j * cl, (j + 1) * cl
                s_scr[j, h] = (jnp.exp(gtot[ci][r0:r0 + 1, h:h + 1]) * s_old[h][j]
                               + _dot_tn(kd_l[i][r0:r1].astype(BF16), u[h][r0:r1].astype(BF16)))
        for h in range(heads):
            on = (o[h] * lax.rsqrt(jnp.mean(o[h] * o[h], axis=-1, keepdims=True) + EPS)
                  * onw_ref[...])
            out = on * _silu(chunk_rows(p_ref, ci, conv_ch + h * dv, conv_ch + (h + 1) * dv))
            for j in seqs:
                o_ref[0, j * seq_rows + ci * cl:j * seq_rows + (ci + 1) * cl,
                      h * dv:(h + 1) * dv] = out[j * cl:(j + 1) * cl]

    new_hist = [xe_scr[j, seq_rows + hist0:seq_rows + SUBLANES, :] for j in range(nb)]
    for j in range(nb):
        xe_scr[j, hist0:SUBLANES, :] = new_hist[j]

    @pl.when(t == pl.num_programs(1) - 1)
    def _():
        s_ref[...] = s_scr[...]
        for j in range(nb):
            c_ref[j] = new_hist[j]


def _gdn_core(p, ba, s0, c0, conv_w, a_log, dt_bias, onorm_w):
    b, l, pw = p.shape
    heads, dk, dv = s0.shape[1:]
    conv_ch = c0.shape[2]
    assert heads <= SUBLANES and dk == LANES and dv == LANES and pw == conv_ch + heads * dv
    if l <= CHUNK:
        assert l % SUBLANES == 0 and l & (l - 1) == 0
        cl, nc = l, 1
        nb = max(n for n in range(1, CHUNK // l + 1) if b % n == 0)
    else:
        assert l % CHUNK == 0
        cl, nb = CHUNK, 1
        nc = GDN_CHUNKS_PER_STEP if l % (CHUNK * GDN_CHUNKS_PER_STEP) == 0 else 1
    groups, rows = b // nb, nb * cl * nc
    steps = l // (cl * nc)

    def lane_pad(vec):
        return jnp.pad(vec.astype(F32), (0, LANES - vec.shape[0])).reshape(1, LANES)

    kern = functools.partial(_gdn_core_kernel, heads=heads, dk=dk, dv=dv, nb=nb, cl=cl, nc=nc)
    o, s, cs = pl.pallas_call(
        kern, grid=(groups, steps),
        in_specs=[pl.BlockSpec((1, rows, pw), lambda i, t: (i, t, 0)),
                  pl.BlockSpec((1, rows, 2 * LANES), lambda i, t: (i, t, 0)),
                  pl.BlockSpec((nb, heads, dk, dv), lambda i, t: (i, 0, 0, 0)),
                  pl.BlockSpec((nb, CONV_W - 1, conv_ch), lambda i, t: (i, 0, 0)),
                  pl.BlockSpec((CONV_W, conv_ch), lambda i, t: (0, 0)),
                  pl.BlockSpec((1, LANES), lambda i, t: (0, 0)),
                  pl.BlockSpec((1, LANES), lambda i, t: (0, 0)),
                  pl.BlockSpec((1, dv), lambda i, t: (0, 0))],
        out_specs=[pl.BlockSpec((1, rows, heads * dv), lambda i, t: (i, t, 0)),
                   pl.BlockSpec((nb, heads, dk, dv), lambda i, t: (i, 0, 0, 0)),
                   pl.BlockSpec((nb, CONV_W - 1, conv_ch), lambda i, t: (i, 0, 0))],
        out_shape=[jax.ShapeDtypeStruct((groups, nb * l, heads * dv), F32),
                   jax.ShapeDtypeStruct(s0.shape, F32),
                   jax.ShapeDtypeStruct(c0.shape, F32)],
        scratch_shapes=[pltpu.VMEM((nb, heads, dk, dv), F32),
                        pltpu.VMEM((nb, SUBLANES + cl * nc, conv_ch), F32)],
        compiler_params=_params("parallel", "arbitrary"), name="gdn_core",
    )(p.reshape(groups, nb * l, pw), ba.reshape(groups, nb * l, 2 * LANES), s0, c0, conv_w,
      lane_pad(a_log), lane_pad(dt_bias), onorm_w.reshape(1, dv))
    return o.reshape(b, l, heads * dv), s, cs


def _ret_core_kernel(p_ref, cos_ref, sin_ref, s0_ref, onw_ref, o_ref, s_ref,
                     *, heads, dk, dv, nb, c, single_step):
    qk_w = heads * dk
    g_off = 2 * qk_w + heads * dv
    half = dk // 2

    if not single_step:
        @pl.when(pl.program_id(1) == 0)
        def _():
            s_ref[...] = s0_ref[...]
    state_ref = s0_ref if single_step else s_ref

    cos = cos_ref[...]
    sin = sin_ref[...]
    row = lax.broadcasted_iota(jnp.int32, (c, c), 0)
    col = lax.broadcasted_iota(jnp.int32, (c, c), 1)
    diff = (row - col).astype(F32)
    idx = lax.broadcasted_iota(jnp.int32, (c, 1), 0).astype(F32)

    def rotary(x):
        x1, x2 = x[:, :half], x[:, half:]
        return jnp.concatenate([x1 * cos - x2 * sin, x1 * sin + x2 * cos], axis=-1)

    decay, q_dec, k_dec, s_dec = [], [], [], []
    for h in range(heads):
        lg = jnp.log(jnp.full((1, 1), 1.0 - 2.0 ** (-5.0 - h), F32))
        decay.append(jnp.where(diff >= 0, jnp.exp(lg * jnp.maximum(diff, 0.0)), 0.0))
        q_dec.append(jnp.exp(lg * (idx + 1.0)))
        k_dec.append(jnp.exp(lg * (c - 1.0 - idx)))
        s_dec.append(jnp.exp(lg * c))

    units = [(j, h) for j in range(nb) for h in range(heads)]
    qb, kb, kdb, vb = [], [], [], []
    for j, h in units:
        r0, r1 = j * c, (j + 1) * c
        q = rotary(p_ref[0, r0:r1, h * dk:(h + 1) * dk])
        k = rotary(p_ref[0, r0:r1, qk_w + h * dk:qk_w + (h + 1) * dk]) * (dk ** -0.5)
        qb.append(q.astype(BF16))
        kb.append(k.astype(BF16))
        kdb.append((k * k_dec[h]).astype(BF16))
        vb.append(p_ref[0, r0:r1, 2 * qk_w + h * dv:2 * qk_w + (h + 1) * dv].astype(BF16))
    qk = [(_dot_nt(qb[i], kb[i]) * decay[h]).astype(BF16) for i, (j, h) in enumerate(units)]
    s_old = [state_ref[j, h] for j, h in units]
    qs = [_dot(qb[i], s_old[i].astype(BF16)) for i in range(len(units))]
    o = [_dot(qk[i], vb[i]) + q_dec[h] * qs[i] for i, (j, h) in enumerate(units)]
    for i, (j, h) in enumerate(units):
        s_ref[j, h] = s_dec[h] * s_old[i] + _dot_tn(kdb[i], vb[i])
    for i, (j, h) in enumerate(units):
        r0, r1 = j * c, (j + 1) * c
        on = (o[i] * lax.rsqrt(jnp.mean(o[i] * o[i], axis=-1, keepdims=True) + EPS)
              * onw_ref[h:h + 1, :])
        gate = p_ref[0, r0:r1, g_off + h * dv:g_off + (h + 1) * dv]
        o_ref[0, r0:r1, h * dv:(h + 1) * dv] = on * _silu(gate)


def _ret_core(p, cos, sin, s0, onorm_w):
    b, l, pw = p.shape
    heads, dk, dv = s0.shape[1:]
    half = dk // 2
    assert half == LANES and pw == 2 * heads * dk + 2 * heads * dv
    if l <= CHUNK:
        assert l % SUBLANES == 0
        c = l
        nb = RET_SEQS_PER_STEP if b % RET_SEQS_PER_STEP == 0 else 1
    else:
        c = RET_CHUNK if l % RET_CHUNK == 0 else CHUNK
        assert l % c == 0
        nb = 1
    groups, steps = b // nb, l // c
    kern = functools.partial(_ret_core_kernel, heads=heads, dk=dk, dv=dv, nb=nb, c=c,
                             single_step=steps == 1)
    o, s = pl.pallas_call(
        kern, grid=(groups, steps),
        in_specs=[pl.BlockSpec((1, nb * c, pw), lambda i, t: (i, t, 0)),
                  pl.BlockSpec((c, half), lambda i, t: (t, 0)),
                  pl.BlockSpec((c, half), lambda i, t: (t, 0)),
                  pl.BlockSpec((nb, heads, dk, dv), lambda i, t: (i, 0, 0, 0)),
                  pl.BlockSpec((heads, dv), lambda i, t: (0, 0))],
        out_specs=[pl.BlockSpec((1, nb * c, heads * dv), lambda i, t: (i, t, 0)),
                   pl.BlockSpec((nb, heads, dk, dv), lambda i, t: (i, 0, 0, 0))],
        out_shape=[jax.ShapeDtypeStruct((groups, nb * l, heads * dv), F32),
                   jax.ShapeDtypeStruct(s0.shape, F32)],
        compiler_params=_params("parallel", "arbitrary"), name="ret_core",
    )(p.reshape(groups, nb * l, pw), cos, sin, s0, onorm_w)
    return o.reshape(b, l, heads * dv), s


def _row_tile(t):
    for tm in (256, 128, 64, 32, 16, 8):
        if t % tm == 0:
            return tm
    raise ValueError(f"token count {t} is not a multiple of {SUBLANES}")


def _rope_tables(l, half, pos0):
    inv = 1.0 / (ROPE_BASE ** jnp.linspace(0.0, 1.0, half, dtype=F32))
    pos = pos0 + jnp.arange(l, dtype=F32)
    ang = pos[:, None] * inv[None, :]
    return jnp.cos(ang), jnp.sin(ang)


def _trunk(x, pos0, gdn_s, gdn_conv, ret_s, norm_w, w_in_a, conv_w_a, a_log_a, dt_bias_a,
           onorm_a, w_out_a, w_in_b, onorm_b, w_out_b, final_norm_w):
    b, l, d = x.shape
    t = b * l
    tm = _row_tile(t)
    heads_a, dk_a, dv_a = gdn_s.shape[2:]
    main_a = 2 * heads_a * dk_a + 2 * heads_a * dv_a
    x2 = x.reshape(t, d)

    w_main = w_in_a[0][:, :main_a].astype(BF16)
    w_tail = jnp.zeros((d, 2 * LANES), F32)
    w_tail = w_tail.at[:, 0:heads_a].set(w_in_a[0][:, main_a:main_a + heads_a])
    w_tail = w_tail.at[:, LANES:LANES + heads_a].set(w_in_a[0][:, main_a + heads_a:])
    p, ba = _in_proj(x2, norm_w[0], [w_main, w_tail.astype(BF16)], tm)
    o, s_a, c_a = _gdn_core(p.reshape(b, l, main_a), ba.reshape(b, l, 2 * LANES), gdn_s[0],
                            gdn_conv[0], conv_w_a[0], a_log_a[0], dt_bias_a[0], onorm_a[0])
    x2 = _out_proj(o.reshape(t, -1), x2, w_out_a[0].astype(BF16), final_norm_w, tm, False)

    heads_b, dk_b, dv_b = ret_s.shape[2:]
    (p,) = _in_proj(x2, norm_w[1], [w_in_b[0].astype(BF16)], tm)
    cos, sin = _rope_tables(l, dk_b // 2, pos0)
    o, s_b = _ret_core(p.reshape(b, l, -1), cos, sin, ret_s[0], onorm_b[0])
    y = _out_proj(o.reshape(t, -1), x2, w_out_b[0].astype(BF16), final_norm_w, tm, True)
    return y.reshape(b, l, d), s_a[None], c_a[None], s_b[None]


def kernel(x_prompt, x_sample, state_gdn_ssm, state_gdn_conv, state_ret, norm_w, w_in_a,
           conv_w_a, a_log_a, dt_bias_a, onorm_a, w_out_a, w_in_b, onorm_b, w_out_b,
           final_norm_w):
    assert state_gdn_ssm.shape[0] == 1 and state_ret.shape[0] == 1 and norm_w.shape[0] == 2
    bp = x_prompt.shape[0]
    weights = (norm_w, w_in_a, conv_w_a, a_log_a, dt_bias_a, onorm_a, w_out_a, w_in_b, onorm_b,
               w_out_b, final_norm_w)
    z_sa = jnp.zeros((1, bp) + state_gdn_ssm.shape[2:], F32)
    z_ca = jnp.zeros((1, bp) + state_gdn_conv.shape[2:], F32)
    z_sb = jnp.zeros((1, bp) + state_ret.shape[2:], F32)
    y_p, sa_p, ca_p, sb_p = _trunk(x_prompt, 0.0, z_sa, z_ca, z_sb, *weights)
    y_s, sa_s, ca_s, sb_s = _trunk(x_sample, PAST_LEN, state_gdn_ssm, state_gdn_conv, state_ret,
                                   *weights)
    return (y_p, y_s, sa_p, ca_p, sb_p, sa_s, ca_s, sb_s)
```

```python
import functools

import jax
import jax.numpy as jnp
from jax import lax
from jax.experimental import pallas as pl
from jax.experimental.pallas import tpu as pltpu

F32 = jnp.float32
BF16 = jnp.bfloat16
EPS = 1e-6
CHUNK = 64
CONV_W = 4
ROPE_BASE = 10000.0
PAST_LEN = 16384.0
LANES = 128
SUBLANES = 8
VMEM_LIMIT_BYTES = 56 * 1024 * 1024
GDN_CHUNKS_PER_STEP = 4
RET_CHUNK = 256
RET_SEQS_PER_STEP = 2


def _dot(a, b):
    return jnp.dot(a, b, preferred_element_type=F32)


def _dot_nt(a, b):
    return lax.dot_general(a, b, (((1,), (1,)), ((), ())), preferred_element_type=F32)


def _dot_tn(a, b):
    return lax.dot_general(a, b, (((0,), (0,)), ((), ())), preferred_element_type=F32)


def _split3(x):
    hi = x.astype(BF16)
    r = x - hi.astype(F32)
    mid = r.astype(BF16)
    lo = (r - mid.astype(F32)).astype(BF16)
    return hi, mid, lo


def _split2(x):
    hi = x.astype(BF16)
    return hi, (x - hi.astype(F32)).astype(BF16)


def _dot_x3(a, b):
    a_hi, a_lo = _split2(a)
    b_hi, b_lo = _split2(b)
    return _dot(a_hi, b_hi) + (_dot(a_hi, b_lo) + _dot(a_lo, b_hi))


def _sigmoid(x):
    return 1.0 / (1.0 + jnp.exp(-x))


def _silu(x):
    return x * _sigmoid(x)


def _softplus(x):
    return jnp.maximum(x, 0.0) + jnp.log1p(jnp.exp(-jnp.abs(x)))


def _params(*sem):
    return pltpu.CompilerParams(dimension_semantics=sem, vmem_limit_bytes=VMEM_LIMIT_BYTES)


def _in_proj_kernel(x_ref, nw_ref, *refs):
    n = len(refs) // 2
    w_refs, o_refs = refs[:n], refs[n:]
    x = x_ref[...]
    ms = jnp.mean(x * x, axis=-1, keepdims=True)
    xn = (x * lax.rsqrt(ms + EPS) * nw_ref[...]).astype(BF16)
    for w_ref, o_ref in zip(w_refs, o_refs):
        o_ref[...] = _dot(xn, w_ref[...])


def _in_proj(x, norm_w, weights, tm):
    t, d = x.shape
    assert t % tm == 0
    in_specs = [pl.BlockSpec((tm, d), lambda i: (i, 0)),
                pl.BlockSpec((1, d), lambda i: (0, 0))]
    in_specs += [pl.BlockSpec(w.shape, lambda i: (0, 0)) for w in weights]
    out_specs = [pl.BlockSpec((tm, w.shape[1]), lambda i: (i, 0)) for w in weights]
    out_shape = [jax.ShapeDtypeStruct((t, w.shape[1]), F32) for w in weights]
    return pl.pallas_call(
        _in_proj_kernel, grid=(t // tm,), in_specs=in_specs, out_specs=out_specs,
        out_shape=out_shape, compiler_params=_params("parallel"), name="in_proj",
    )(x, norm_w.reshape(1, d), *weights)


def _out_proj_kernel(o_ref, x_ref, w_ref, fw_ref, y_ref, *, final_norm):
    y = x_ref[...] + _dot(o_ref[...].astype(BF16), w_ref[...])
    if final_norm:
        ms = jnp.mean(y * y, axis=-1, keepdims=True)
        y = y * lax.rsqrt(ms + EPS) * fw_ref[...]
    y_ref[...] = y


def _out_proj(o, x, w, final_w, tm, final_norm):
    t, k = o.shape
    d = x.shape[1]
    assert t % tm == 0
    return pl.pallas_call(
        functools.partial(_out_proj_kernel, final_norm=final_norm),
        grid=(t // tm,),
        in_specs=[pl.BlockSpec((tm, k), lambda i: (i, 0)),
                  pl.BlockSpec((tm, d), lambda i: (i, 0)),
                  pl.BlockSpec((k, d), lambda i: (0, 0)),
                  pl.BlockSpec((1, d), lambda i: (0, 0))],
        out_specs=pl.BlockSpec((tm, d), lambda i: (i, 0)),
        out_shape=jax.ShapeDtypeStruct((t, d), F32),
        compiler_params=_params("parallel"), name="out_proj",
    )(o, x, w, final_w.reshape(1, d))


def _unit_lower_inverses(lows, eye, cl):
    powers = [(-low).astype(BF16) for low in lows]
    invs = [eye - low for low in lows]
    p = 1
    while 2 * p < cl:
        sq = [_dot(pw, pw) for pw in powers]
        p *= 2
        powers = [s.astype(BF16) for s in sq]
        invs = [inv + _dot(inv.astype(BF16), pw) for inv, pw in zip(invs, powers)]
    return invs


def _gdn_core_kernel(p_ref, ba_ref, s0_ref, c0_ref, cw_ref, alog_ref, dtb_ref, onw_ref,
                     o_ref, s_ref, c_ref, s_scr, xe_scr, *, heads, dk, dv, nb, cl, nc):
    c = nb * cl
    seq_rows = cl * nc
    qk_w = heads * dk
    conv_ch = 2 * qk_w + heads * dv
    hist0 = SUBLANES - (CONV_W - 1)
    t = pl.program_id(1)

    @pl.when(t == 0)
    def _():
        s_scr[...] = s0_ref[...]
        for j in range(nb):
            xe_scr[j, hist0:SUBLANES, :] = c0_ref[j]

    for j in range(nb):
        xe_scr[j, SUBLANES:SUBLANES + seq_rows, :] = p_ref[0, j * seq_rows:(j + 1) * seq_rows,
                                                            0:conv_ch]

    row = lax.broadcasted_iota(jnp.int32, (c, c), 0)
    col = lax.broadcasted_iota(jnp.int32, (c, c), 1)
    if nb > 1:
        shift = cl.bit_length() - 1
        same = lax.shift_right_logical(row, shift) == lax.shift_right_logical(col, shift)
        incl = same & (row >= col)
        strict = same & (row > col)
    else:
        same = row >= 0
        incl = row >= col
        strict = row > col
    eye = (row == col).astype(F32)
    cum_masks = jnp.concatenate([incl.astype(BF16), same.astype(BF16)], axis=0)
    sel = (lax.broadcasted_iota(jnp.int32, (SUBLANES, LANES), 0)
           == lax.broadcasted_iota(jnp.int32, (SUBLANES, LANES), 1)).astype(BF16)
    neg_a = -jnp.exp(alog_ref[...])

    def chunk_rows(ref, ci, lo, hi):
        parts = [ref[0, j * seq_rows + ci * cl:j * seq_rows + (ci + 1) * cl, lo:hi]
                 for j in range(nb)]
        return parts[0] if nb == 1 else jnp.concatenate(parts, axis=0)

    units = [(ci, h) for ci in range(nc) for h in range(heads)]
    seqs = range(nb)

    def stack(parts):
        return parts[0] if nb == 1 else jnp.concatenate(parts, axis=0)

    mixed, beta, gcum, gtot, gcum_t = [], [], [], [], []
    for ci in range(nc):
        parts = []
        for j in seqs:
            base = hist0 + ci * cl
            acc = xe_scr[j, base:base + cl, :] * cw_ref[0:1, :]
            for tap in range(1, CONV_W):
                acc = acc + xe_scr[j, base + tap:base + tap + cl, :] * cw_ref[tap:tap + 1, :]
            parts.append(acc)
        mixed.append(_silu(stack(parts)))
        beta.append(_sigmoid(chunk_rows(ba_ref, ci, 0, LANES)))
        g = neg_a * _softplus(chunk_rows(ba_ref, ci, LANES, 2 * LANES) + dtb_ref[...])
        g3 = _split3(g)
        gg = _dot(cum_masks, g3[0]) + (_dot(cum_masks, g3[1]) + _dot(cum_masks, g3[2]))
        gcum.append(gg[:c])
        gtot.append(gg[c:])
        gc3 = _split3(gg[:c])
        gcum_t.append(_dot_nt(sel, gc3[0]) + (_dot_nt(sel, gc3[1]) + _dot_nt(sel, gc3[2])))

    q_l, eg_l, lows, rhs_l, qk_l, kd_l = [], [], [], [], [], []
    for ci, h in units:
        q = mixed[ci][:, h * dk:(h + 1) * dk]
        k = mixed[ci][:, qk_w + h * dk:qk_w + (h + 1) * dk]
        v = mixed[ci][:, 2 * qk_w + h * dv:2 * qk_w + (h + 1) * dv]
        q = q * lax.rsqrt(jnp.sum(q * q, axis=-1, keepdims=True) + EPS) * (dk ** -0.5)
        k = k * lax.rsqrt(jnp.sum(k * k, axis=-1, keepdims=True) + EPS)
        gc = gcum[ci][:, h:h + 1]
        gr = gcum_t[ci][h:h + 1, :]
        bh = beta[ci][:, h:h + 1]
        dec = jnp.where(incl, jnp.exp(jnp.where(incl, gc - gr, 0.0)), 0.0)
        eg = jnp.exp(gc)
        qb = q.astype(BF16)
        kb = k.astype(BF16)
        lows.append(jnp.where(strict, bh * _dot_nt(kb, kb) * dec, 0.0))
        qk_l.append((_dot_nt(qb, kb) * dec).astype(BF16))
        rhs_l.append(jnp.concatenate([bh * v, (bh * eg) * k], axis=-1).astype(BF16))
        kd_l.append(k * jnp.exp(gtot[ci][:, h:h + 1] - gc))
        q_l.append(q)
        eg_l.append(eg)
    invs = _unit_lower_inverses(lows, eye, cl)
    sols = [_dot(inv.astype(BF16), rhs) for inv, rhs in zip(invs, rhs_l)]

    for ci in range(nc):
        idx = [ci * heads + h for h in range(heads)]
        s_old = [[s_scr[j, h] for j in seqs] for h in range(heads)]
        res = [[_dot(jnp.concatenate([sols[i][j * cl:(j + 1) * cl, dv:],
                                      q_l[i][j * cl:(j + 1) * cl]], axis=0).astype(BF16),
                     s_old[h][j].astype(BF16)) for j in seqs]
               for h, i in enumerate(idx)]
        u = [stack([sols[i][j * cl:(j + 1) * cl, :dv] - res[h][j][:cl] for j in seqs])
             for h, i in enumerate(idx)]
        o = [eg_l[i] * stack([res[h][j][cl:] for j in seqs]) + _dot(qk_l[i], u[h].astype(BF16))
             for h, i in enumerate(idx)]
        for h, i in enumerate(idx):
            for j in seqs:
                r0, r1 = j * cl, (j + 1) * cl
                s_scr[j, h] = (jnp.exp(gtot[ci][r0:r0 + 1, h:h + 1]) * s_old[h][j]
                               + _dot_tn(kd_l[i][r0:r1].astype(BF16), u[h][r0:r1].astype(BF16)))
        for h in range(heads):
            on = (o[h] * lax.rsqrt(jnp.mean(o[h] * o[h], axis=-1, keepdims=True) + EPS)
                  * onw_ref[...])
            out = on * _silu(chunk_rows(p_ref, ci, conv_ch + h * dv, conv_ch + (h + 1) * dv))
            for j in seqs:
                o_ref[0, j * seq_rows + ci * cl:j * seq_rows + (ci + 1) * cl,
                      h * dv:(h + 1) * dv] = out[j * cl:(j + 1) * cl]

    new_hist = [xe_scr[j, seq_rows + hist0:seq_rows + SUBLANES, :] for j in range(nb)]
    for j in range(nb):
        xe_scr[j, hist0:SUBLANES, :] = new_hist[j]

    @pl.when(t == pl.num_programs(1) - 1)
    def _():
        s_ref[...] = s_scr[...]
        for j in range(nb):
            c_ref[j] = new_hist[j]


def _gdn_core(p, ba, s0, c0, conv_w, a_log, dt_bias, onorm_w):
    b, l, pw = p.shape
    heads, dk, dv = s0.shape[1:]
    conv_ch = c0.shape[2]
    assert heads <= SUBLANES and dk == LANES and dv == LANES and pw == conv_ch + heads * dv
    if l <= CHUNK:
        assert l % SUBLANES == 0 and l & (l - 1) == 0
        cl, nc = l, 1
        nb = max(n for n in range(1, CHUNK // l + 1) if b % n == 0)
    else:
        assert l % CHUNK == 0
        cl, nb = CHUNK, 1
        nc = GDN_CHUNKS_PER_STEP if l % (CHUNK * GDN_CHUNKS_PER_STEP) == 0 else 1
    groups, rows = b // nb, nb * cl * nc
    steps = l // (cl * nc)

    def lane_pad(vec):
        return jnp.pad(vec.astype(F32), (0, LANES - vec.shape[0])).reshape(1, LANES)

    kern = functools.partial(_gdn_core_kernel, heads=heads, dk=dk, dv=dv, nb=nb, cl=cl, nc=nc)
    o, s, cs = pl.pallas_call(
        kern, grid=(groups, steps),
        in_specs=[pl.BlockSpec((1, rows, pw), lambda i, t: (i, t, 0)),
                  pl.BlockSpec((1, rows, 2 * LANES), lambda i, t: (i, t, 0)),
                  pl.BlockSpec((nb, heads, dk, dv), lambda i, t: (i, 0, 0, 0)),
                  pl.BlockSpec((nb, CONV_W - 1, conv_ch), lambda i, t: (i, 0, 0)),
                  pl.BlockSpec((CONV_W, conv_ch), lambda i, t: (0, 0)),
                  pl.BlockSpec((1, LANES), lambda i, t: (0, 0)),
                  pl.BlockSpec((1, LANES), lambda i, t: (0, 0)),
                  pl.BlockSpec((1, dv), lambda i, t: (0, 0))],
        out_specs=[pl.BlockSpec((1, rows, heads * dv), lambda i, t: (i, t, 0)),
                   pl.BlockSpec((nb, heads, dk, dv), lambda i, t: (i, 0, 0, 0)),
                   pl.BlockSpec((nb, CONV_W - 1, conv_ch), lambda i, t: (i, 0, 0))],
        out_shape=[jax.ShapeDtypeStruct((groups, nb * l, heads * dv), F32),
                   jax.ShapeDtypeStruct(s0.shape, F32),
                   jax.ShapeDtypeStruct(c0.shape, F32)],
        scratch_shapes=[pltpu.VMEM((nb, heads, dk, dv), F32),
                        pltpu.VMEM((nb, SUBLANES + cl * nc, conv_ch), F32)],
        compiler_params=_params("parallel", "arbitrary"), name="gdn_core",
    )(p.reshape(groups, nb * l, pw), ba.reshape(groups, nb * l, 2 * LANES), s0, c0, conv_w,
      lane_pad(a_log), lane_pad(dt_bias), onorm_w.reshape(1, dv))
    return o.reshape(b, l, heads * dv), s, cs


def _ret_chunk(cols, put_out, cos_ref, sin_ref, s0_ref, onw_ref, s_ref,
               *, heads, dk, dv, nb, c, single_step):
    qk_w = heads * dk
    g_off = 2 * qk_w + heads * dv
    half = dk // 2

    if not single_step:
        @pl.when(pl.program_id(1) == 0)
        def _():
            s_ref[...] = s0_ref[...]
    state_ref = s0_ref if single_step else s_ref

    cos = cos_ref[...]
    sin = sin_ref[...]
    row = lax.broadcasted_iota(jnp.int32, (c, c), 0)
    col = lax.broadcasted_iota(jnp.int32, (c, c), 1)
    diff = (row - col).astype(F32)
    idx = lax.broadcasted_iota(jnp.int32, (c, 1), 0).astype(F32)

    def rotary(x):
        x1, x2 = x[:, :half], x[:, half:]
        return jnp.concatenate([x1 * cos - x2 * sin, x1 * sin + x2 * cos], axis=-1)

    decay, q_dec, k_dec, s_dec = [], [], [], []
    for h in range(heads):
        lg = jnp.log(jnp.full((1, 1), 1.0 - 2.0 ** (-5.0 - h), F32))
        decay.append(jnp.where(diff >= 0, jnp.exp(lg * jnp.maximum(diff, 0.0)), 0.0))
        q_dec.append(jnp.exp(lg * (idx + 1.0)))
        k_dec.append(jnp.exp(lg * (c - 1.0 - idx)))
        s_dec.append(jnp.exp(lg * c))

    units = [(j, h) for j in range(nb) for h in range(heads)]
    qb, kb, kdb, vb = [], [], [], []
    for j, h in units:
        q = rotary(cols(j, h * dk, (h + 1) * dk))
        k = rotary(cols(j, qk_w + h * dk, qk_w + (h + 1) * dk)) * (dk ** -0.5)
        qb.append(q.astype(BF16))
        kb.append(k.astype(BF16))
        kdb.append((k * k_dec[h]).astype(BF16))
        vb.append(cols(j, 2 * qk_w + h * dv, 2 * qk_w + (h + 1) * dv).astype(BF16))
    qk = [(_dot_nt(qb[i], kb[i]) * decay[h]).astype(BF16) for i, (j, h) in enumerate(units)]
    s_old = [state_ref[j, h] for j, h in units]
    qs = [_dot(qb[i], s_old[i].astype(BF16)) for i in range(len(units))]
    o = [_dot(qk[i], vb[i]) + q_dec[h] * qs[i] for i, (j, h) in enumerate(units)]
    for i, (j, h) in enumerate(units):
        s_ref[j, h] = s_dec[h] * s_old[i] + _dot_tn(kdb[i], vb[i])
    for i, (j, h) in enumerate(units):
        on = (o[i] * lax.rsqrt(jnp.mean(o[i] * o[i], axis=-1, keepdims=True) + EPS)
              * onw_ref[h:h + 1, :])
        put_out(j, h, on * _silu(cols(j, g_off + h * dv, g_off + (h + 1) * dv)))


def _ret_core_kernel(p_ref, cos_ref, sin_ref, s0_ref, onw_ref, o_ref, s_ref,
                     *, heads, dk, dv, nb, c, single_step):
    def cols(j, lo, hi):
        return p_ref[0, j * c:(j + 1) * c, lo:hi]

    def put_out(j, h, value):
        o_ref[0, j * c:(j + 1) * c, h * dv:(h + 1) * dv] = value

    _ret_chunk(cols, put_out, cos_ref, sin_ref, s0_ref, onw_ref, s_ref,
               heads=heads, dk=dk, dv=dv, nb=nb, c=c, single_step=single_step)


def _ret_layer_kernel(x_ref, nw_ref, win_ref, cos_ref, sin_ref, s0_ref, onw_ref, wout_ref,
                      fw_ref, y_ref, s_ref, o_scr, *, heads, dk, dv, c, single_step):
    x = x_ref[0]
    ms = jnp.mean(x * x, axis=-1, keepdims=True)
    xn = (x * lax.rsqrt(ms + EPS) * nw_ref[...]).astype(BF16)
    p = _dot(xn, win_ref[...])

    def cols(j, lo, hi):
        return p[:, lo:hi]

    def put_out(j, h, value):
        o_scr[:, h * dv:(h + 1) * dv] = value.astype(BF16)

    _ret_chunk(cols, put_out, cos_ref, sin_ref, s0_ref, onw_ref, s_ref,
               heads=heads, dk=dk, dv=dv, nb=1, c=c, single_step=single_step)
    y = x + _dot(o_scr[...], wout_ref[...])
    ms = jnp.mean(y * y, axis=-1, keepdims=True)
    y_ref[0] = y * lax.rsqrt(ms + EPS) * fw_ref[...]


def _ret_layer(x, norm_w, w_in, cos, sin, s0, onorm_w, w_out, final_w):
    b, l, d = x.shape
    heads, dk, dv = s0.shape[1:]
    pw = w_in.shape[1]
    c = RET_CHUNK
    assert l % c == 0 and dk // 2 == LANES and pw == 2 * heads * dk + 2 * heads * dv
    resident = dict(pipeline_mode=pl.Buffered(1))
    kern = functools.partial(_ret_layer_kernel, heads=heads, dk=dk, dv=dv, c=c,
                             single_step=l == c)
    return pl.pallas_call(
        kern, grid=(b, l // c),
        in_specs=[pl.BlockSpec((1, c, d), lambda i, t: (i, t, 0)),
                  pl.BlockSpec((1, d), lambda i, t: (0, 0)),
                  pl.BlockSpec((d, pw), lambda i, t: (0, 0), **resident),
                  pl.BlockSpec((c, dk // 2), lambda i, t: (t, 0)),
                  pl.BlockSpec((c, dk // 2), lambda i, t: (t, 0)),
                  pl.BlockSpec((1, heads, dk, dv), lambda i, t: (i, 0, 0, 0)),
                  pl.BlockSpec((heads, dv), lambda i, t: (0, 0)),
                  pl.BlockSpec((heads * dv, d), lambda i, t: (0, 0), **resident),
                  pl.BlockSpec((1, d), lambda i, t: (0, 0))],
        out_specs=[pl.BlockSpec((1, c, d), lambda i, t: (i, t, 0)),
                   pl.BlockSpec((1, heads, dk, dv), lambda i, t: (i, 0, 0, 0))],
        out_shape=[jax.ShapeDtypeStruct((b, l, d), F32),
                   jax.ShapeDtypeStruct(s0.shape, F32)],
        scratch_shapes=[pltpu.VMEM((c, heads * dv), BF16)],
        compiler_params=_params("parallel", "arbitrary"), name="ret_layer",
    )(x, norm_w.reshape(1, d), w_in, cos, sin, s0, onorm_w, w_out, final_w.reshape(1, d))


def _ret_core(p, cos, sin, s0, onorm_w):
    b, l, pw = p.shape
    heads, dk, dv = s0.shape[1:]
    half = dk // 2
    assert half == LANES and pw == 2 * heads * dk + 2 * heads * dv
    if l <= CHUNK:
        assert l % SUBLANES == 0
        c = l
        nb = RET_SEQS_PER_STEP if b % RET_SEQS_PER_STEP == 0 else 1
    else:
        c = RET_CHUNK if l % RET_CHUNK == 0 else CHUNK
        assert l % c == 0
        nb = 1
    groups, steps = b // nb, l // c
    kern = functools.partial(_ret_core_kernel, heads=heads, dk=dk, dv=dv, nb=nb, c=c,
                             single_step=steps == 1)
    o, s = pl.pallas_call(
        kern, grid=(groups, steps),
        in_specs=[pl.BlockSpec((1, nb * c, pw), lambda i, t: (i, t, 0)),
                  pl.BlockSpec((c, half), lambda i, t: (t, 0)),
                  pl.BlockSpec((c, half), lambda i, t: (t, 0)),
                  pl.BlockSpec((nb, heads, dk, dv), lambda i, t: (i, 0, 0, 0)),
                  pl.BlockSpec((heads, dv), lambda i, t: (0, 0))],
        out_specs=[pl.BlockSpec((1, nb * c, heads * dv), lambda i, t: (i, t, 0)),
                   pl.BlockSpec((nb, heads, dk, dv), lambda i, t: (i, 0, 0, 0))],
        out_shape=[jax.ShapeDtypeStruct((groups, nb * l, heads * dv), F32),
                   jax.ShapeDtypeStruct(s0.shape, F32)],
        compiler_params=_params("parallel", "arbitrary"), name="ret_core",
    )(p.reshape(groups, nb * l, pw), cos, sin, s0, onorm_w)
    return o.reshape(b, l, heads * dv), s


def _row_tile(t):
    for tm in (256, 128, 64, 32, 16, 8):
        if t % tm == 0:
            return tm
    raise ValueError(f"token count {t} is not a multiple of {SUBLANES}")


def _rope_tables(l, half, pos0):
    inv = 1.0 / (ROPE_BASE ** jnp.linspace(0.0, 1.0, half, dtype=F32))
    pos = pos0 + jnp.arange(l, dtype=F32)
    ang = pos[:, None] * inv[None, :]
    return jnp.cos(ang), jnp.sin(ang)


def _trunk(x, pos0, gdn_s, gdn_conv, ret_s, norm_w, w_in_a, conv_w_a, a_log_a, dt_bias_a,
           onorm_a, w_out_a, w_in_b, onorm_b, w_out_b, final_norm_w):
    b, l, d = x.shape
    t = b * l
    tm = _row_tile(t)
    heads_a, dk_a, dv_a = gdn_s.shape[2:]
    main_a = 2 * heads_a * dk_a + 2 * heads_a * dv_a
    x2 = x.reshape(t, d)

    w_main = w_in_a[0][:, :main_a].astype(BF16)
    w_tail = jnp.zeros((d, 2 * LANES), F32)
    w_tail = w_tail.at[:, 0:heads_a].set(w_in_a[0][:, main_a:main_a + heads_a])
    w_tail = w_tail.at[:, LANES:LANES + heads_a].set(w_in_a[0][:, main_a + heads_a:])
    p, ba = _in_proj(x2, norm_w[0], [w_main, w_tail.astype(BF16)], tm)
    o, s_a, c_a = _gdn_core(p.reshape(b, l, main_a), ba.reshape(b, l, 2 * LANES), gdn_s[0],
                            gdn_conv[0], conv_w_a[0], a_log_a[0], dt_bias_a[0], onorm_a[0])
    x2 = _out_proj(o.reshape(t, -1), x2, w_out_a[0].astype(BF16), final_norm_w, tm, False)

    heads_b, dk_b, dv_b = ret_s.shape[2:]
    cos, sin = _rope_tables(l, dk_b // 2, pos0)
    if l % RET_CHUNK == 0:
        y, s_b = _ret_layer(x2.reshape(b, l, d), norm_w[1], w_in_b[0].astype(BF16), cos, sin,
                            ret_s[0], onorm_b[0], w_out_b[0].astype(BF16), final_norm_w)
    else:
        (p,) = _in_proj(x2, norm_w[1], [w_in_b[0].astype(BF16)], tm)
        o, s_b = _ret_core(p.reshape(b, l, -1), cos, sin, ret_s[0], onorm_b[0])
        y = _out_proj(o.reshape(t, -1), x2, w_out_b[0].astype(BF16), final_norm_w, tm, True)
    return y.reshape(b, l, d), s_a[None], c_a[None], s_b[None]


def kernel(x_prompt, x_sample, state_gdn_ssm, state_gdn_conv, state_ret, norm_w, w_in_a,
           conv_w_a, a_log_a, dt_bias_a, onorm_a, w_out_a, w_in_b, onorm_b, w_out_b,
           final_norm_w):
    assert state_gdn_ssm.shape[0] == 1 and state_ret.shape[0] == 1 and norm_w.shape[0] == 2
    bp = x_prompt.shape[0]
    weights = (norm_w, w_in_a, conv_w_a, a_log_a, dt_bias_a, onorm_a, w_out_a, w_in_b, onorm_b,
               w_out_b, final_norm_w)
    z_sa = jnp.zeros((1, bp) + state_gdn_ssm.shape[2:], F32)
    z_ca = jnp.zeros((1, bp) + state_gdn_conv.shape[2:], F32)
    z_sb = jnp.zeros((1, bp) + state_ret.shape[2:], F32)
    y_p, sa_p, ca_p, sb_p = _trunk(x_prompt, 0.0, z_sa, z_ca, z_sb, *weights)
    y_s, sa_s, ca_s, sb_s = _trunk(x_sample, PAST_LEN, state_gdn_ssm, state_gdn_conv, state_ret,
                                   *weights)
    return (y_p, y_s, sa_p, ca_p, sb_p, sa_s, ca_s, sb_s)
```

```python
import functools

import jax
import jax.numpy as jnp
from jax import lax
from jax.experimental import pallas as pl
from jax.experimental.pallas import tpu as pltpu

F32 = jnp.float32
BF16 = jnp.bfloat16
EPS = 1e-6
CHUNK = 64
CONV_W = 4
ROPE_BASE = 10000.0
PAST_LEN = 16384.0
LANES = 128
SUBLANES = 8
VMEM_LIMIT_BYTES = 56 * 1024 * 1024
GDN_CHUNKS_PER_STEP = 4
RET_CHUNK = 256
RET_SEQS_PER_STEP = 2


def _dot(a, b):
    return jnp.dot(a, b, preferred_element_type=F32)


def _dot_nt(a, b):
    return lax.dot_general(a, b, (((1,), (1,)), ((), ())), preferred_element_type=F32)


def _dot_tn(a, b):
    return lax.dot_general(a, b, (((0,), (0,)), ((), ())), preferred_element_type=F32)


def _split3(x):
    hi = x.astype(BF16)
    r = x - hi.astype(F32)
    mid = r.astype(BF16)
    lo = (r - mid.astype(F32)).astype(BF16)
    return hi, mid, lo


def _split2(x):
    hi = x.astype(BF16)
    return hi, (x - hi.astype(F32)).astype(BF16)


def _dot_x3(a, b):
    a_hi, a_lo = _split2(a)
    b_hi, b_lo = _split2(b)
    return _dot(a_hi, b_hi) + (_dot(a_hi, b_lo) + _dot(a_lo, b_hi))


def _sigmoid(x):
    return 1.0 / (1.0 + jnp.exp(-x))


def _silu(x):
    return x * _sigmoid(x)


def _softplus(x):
    return jnp.maximum(x, 0.0) + jnp.log1p(jnp.exp(-jnp.abs(x)))


def _params(*sem):
    return pltpu.CompilerParams(dimension_semantics=sem, vmem_limit_bytes=VMEM_LIMIT_BYTES)


def _in_proj_kernel(x_ref, nw_ref, *refs):
    n = len(refs) // 2
    w_refs, o_refs = refs[:n], refs[n:]
    x = x_ref[...]
    ms = jnp.mean(x * x, axis=-1, keepdims=True)
    xn = (x * lax.rsqrt(ms + EPS) * nw_ref[...]).astype(BF16)
    for w_ref, o_ref in zip(w_refs, o_refs):
        o_ref[...] = _dot(xn, w_ref[...])


def _in_proj(x, norm_w, weights, tm):
    t, d = x.shape
    assert t % tm == 0
    in_specs = [pl.BlockSpec((tm, d), lambda i: (i, 0)),
                pl.BlockSpec((1, d), lambda i: (0, 0))]
    in_specs += [pl.BlockSpec(w.shape, lambda i: (0, 0)) for w in weights]
    out_specs = [pl.BlockSpec((tm, w.shape[1]), lambda i: (i, 0)) for w in weights]
    out_shape = [jax.ShapeDtypeStruct((t, w.shape[1]), F32) for w in weights]
    return pl.pallas_call(
        _in_proj_kernel, grid=(t // tm,), in_specs=in_specs, out_specs=out_specs,
        out_shape=out_shape, compiler_params=_params("parallel"), name="in_proj",
    )(x, norm_w.reshape(1, d), *weights)


def _out_proj_kernel(o_ref, x_ref, w_ref, fw_ref, y_ref, *, final_norm):
    y = x_ref[...] + _dot(o_ref[...].astype(BF16), w_ref[...])
    if final_norm:
        ms = jnp.mean(y * y, axis=-1, keepdims=True)
        y = y * lax.rsqrt(ms + EPS) * fw_ref[...]
    y_ref[...] = y


def _out_proj(o, x, w, final_w, tm, final_norm):
    t, k = o.shape
    d = x.shape[1]
    assert t % tm == 0
    return pl.pallas_call(
        functools.partial(_out_proj_kernel, final_norm=final_norm),
        grid=(t // tm,),
        in_specs=[pl.BlockSpec((tm, k), lambda i: (i, 0)),
                  pl.BlockSpec((tm, d), lambda i: (i, 0)),
                  pl.BlockSpec((k, d), lambda i: (0, 0)),
                  pl.BlockSpec((1, d), lambda i: (0, 0))],
        out_specs=pl.BlockSpec((tm, d), lambda i: (i, 0)),
        out_shape=jax.ShapeDtypeStruct((t, d), F32),
        compiler_params=_params("parallel"), name="out_proj",
    )(o, x, w, final_w.reshape(1, d))


def _unit_lower_inverses(lows, eye, cl):
    powers = [(-low).astype(BF16) for low in lows]
    invs = [eye - low for low in lows]
    p = 1
    while 2 * p < cl:
        sq = [_dot(pw, pw) for pw in powers]
        p *= 2
        powers = [s.astype(BF16) for s in sq]
        invs = [inv + _dot(inv.astype(BF16), pw) for inv, pw in zip(invs, powers)]
    return invs


def _gdn_step(p_rows, ba_rows, put_out, s0_ref, c0_ref, cw_ref, alog_ref, dtb_ref, onw_ref,
              s_ref, c_ref, s_scr, xe_scr, *, heads, dk, dv, nb, cl, nc):
    c = nb * cl
    seq_rows = cl * nc
    qk_w = heads * dk
    conv_ch = 2 * qk_w + heads * dv
    hist0 = SUBLANES - (CONV_W - 1)
    t = pl.program_id(1)

    @pl.when(t == 0)
    def _():
        s_scr[...] = s0_ref[...]
        for j in range(nb):
            xe_scr[j, 0:SUBLANES, :] = jnp.zeros((SUBLANES, conv_ch), F32)
            xe_scr[j, hist0:SUBLANES, :] = c0_ref[j]

    for j in range(nb):
        xe_scr[j, SUBLANES:SUBLANES + seq_rows, :] = p_rows(j * seq_rows, (j + 1) * seq_rows,
                                                            0, conv_ch)

    row = lax.broadcasted_iota(jnp.int32, (c, c), 0)
    col = lax.broadcasted_iota(jnp.int32, (c, c), 1)
    if nb > 1:
        shift = cl.bit_length() - 1
        same = lax.shift_right_logical(row, shift) == lax.shift_right_logical(col, shift)
        incl = same & (row >= col)
        strict = same & (row > col)
    else:
        same = row >= 0
        incl = row >= col
        strict = row > col
    eye = (row == col).astype(F32)
    cum_masks = jnp.concatenate([incl.astype(BF16), same.astype(BF16)], axis=0)
    sel = (lax.broadcasted_iota(jnp.int32, (SUBLANES, LANES), 0)
           == lax.broadcasted_iota(jnp.int32, (SUBLANES, LANES), 1)).astype(BF16)
    neg_a = -jnp.exp(alog_ref[...])

    def chunk_rows(rows, ci, lo, hi):
        parts = [rows(j * seq_rows + ci * cl, j * seq_rows + (ci + 1) * cl, lo, hi)
                 for j in range(nb)]
        return parts[0] if nb == 1 else jnp.concatenate(parts, axis=0)

    units = [(ci, h) for ci in range(nc) for h in range(heads)]
    seqs = range(nb)

    def stack(parts):
        return parts[0] if nb == 1 else jnp.concatenate(parts, axis=0)

    conv = []
    for j in seqs:
        xall = xe_scr[j]
        acc = xall * cw_ref[0:1, :]
        for tap in range(1, CONV_W):
            acc = pltpu.roll(acc, 1, axis=0) + xall * cw_ref[tap:tap + 1, :]
        conv.append(acc[SUBLANES:])

    mixed, beta, gcum, gtot, gcum_t = [], [], [], [], []
    for ci in range(nc):
        mixed.append(_silu(stack([conv[j][ci * cl:(ci + 1) * cl] for j in seqs])))
        beta.append(_sigmoid(chunk_rows(ba_rows, ci, 0, LANES)))
        g = neg_a * _softplus(chunk_rows(ba_rows, ci, LANES, 2 * LANES) + dtb_ref[...])
        g3 = _split3(g)
        gg = _dot(cum_masks, g3[0]) + (_dot(cum_masks, g3[1]) + _dot(cum_masks, g3[2]))
        gcum.append(gg[:c])
        gtot.append(gg[c:])
        gc3 = _split3(gg[:c])
        gcum_t.append(_dot_nt(sel, gc3[0]) + (_dot_nt(sel, gc3[1]) + _dot_nt(sel, gc3[2])))

    q_l, eg_l, lows, rhs_l, qk_l, kd_l = [], [], [], [], [], []
    for ci, h in units:
        q = mixed[ci][:, h * dk:(h + 1) * dk]
        k = mixed[ci][:, qk_w + h * dk:qk_w + (h + 1) * dk]
        v = mixed[ci][:, 2 * qk_w + h * dv:2 * qk_w + (h + 1) * dv]
        q = q * lax.rsqrt(jnp.sum(q * q, axis=-1, keepdims=True) + EPS) * (dk ** -0.5)
        k = k * lax.rsqrt(jnp.sum(k * k, axis=-1, keepdims=True) + EPS)
        gc = gcum[ci][:, h:h + 1]
        gr = gcum_t[ci][h:h + 1, :]
        bh = beta[ci][:, h:h + 1]
        dec = jnp.where(incl, jnp.exp(jnp.where(incl, gc - gr, 0.0)), 0.0)
        eg = jnp.exp(gc)
        qb = q.astype(BF16)
        kb = k.astype(BF16)
        lows.append(jnp.where(strict, bh * _dot_nt(kb, kb) * dec, 0.0))
        qk_l.append((_dot_nt(qb, kb) * dec).astype(BF16))
        rhs_l.append(jnp.concatenate([bh * v, (bh * eg) * k], axis=-1).astype(BF16))
        kd_l.append(k * jnp.exp(gtot[ci][:, h:h + 1] - gc))
        q_l.append(q)
        eg_l.append(eg)
    invs = _unit_lower_inverses(lows, eye, cl)
    sols = [_dot(inv.astype(BF16), rhs) for inv, rhs in zip(invs, rhs_l)]

    for ci in range(nc):
        idx = [ci * heads + h for h in range(heads)]
        s_old = [[s_scr[j, h] for j in seqs] for h in range(heads)]
        res = [[_dot(jnp.concatenate([sols[i][j * cl:(j + 1) * cl, dv:],
                                      q_l[i][j * cl:(j + 1) * cl]], axis=0).astype(BF16),
                     s_old[h][j].astype(BF16)) for j in seqs]
               for h, i in enumerate(idx)]
        u = [stack([sols[i][j * cl:(j + 1) * cl, :dv] - res[h][j][:cl] for j in seqs])
             for h, i in enumerate(idx)]
        o = [eg_l[i] * stack([res[h][j][cl:] for j in seqs]) + _dot(qk_l[i], u[h].astype(BF16))
             for h, i in enumerate(idx)]
        for h, i in enumerate(idx):
            for j in seqs:
                r0, r1 = j * cl, (j + 1) * cl
                s_scr[j, h] = (jnp.exp(gtot[ci][r0:r0 + 1, h:h + 1]) * s_old[h][j]
                               + _dot_tn(kd_l[i][r0:r1].astype(BF16), u[h][r0:r1].astype(BF16)))
        for h in range(heads):
            on = (o[h] * lax.rsqrt(jnp.mean(o[h] * o[h], axis=-1, keepdims=True) + EPS)
                  * onw_ref[...])
            out = on * _silu(chunk_rows(p_rows, ci, conv_ch + h * dv, conv_ch + (h + 1) * dv))
            for j in seqs:
                put_out(j * seq_rows + ci * cl, j * seq_rows + (ci + 1) * cl, h,
                        out[j * cl:(j + 1) * cl])

    new_hist = [xe_scr[j, seq_rows + hist0:seq_rows + SUBLANES, :] for j in range(nb)]
    for j in range(nb):
        xe_scr[j, hist0:SUBLANES, :] = new_hist[j]

    @pl.when(t == pl.num_programs(1) - 1)
    def _():
        s_ref[...] = s_scr[...]
        for j in range(nb):
            c_ref[j] = new_hist[j]


def _gdn_core_kernel(p_ref, ba_ref, s0_ref, c0_ref, cw_ref, alog_ref, dtb_ref, onw_ref,
                     o_ref, s_ref, c_ref, s_scr, xe_scr, *, heads, dk, dv, nb, cl, nc):
    def p_rows(r0, r1, lo, hi):
        return p_ref[0, r0:r1, lo:hi]

    def ba_rows(r0, r1, lo, hi):
        return ba_ref[0, r0:r1, lo:hi]

    def put_out(r0, r1, h, value):
        o_ref[0, r0:r1, h * dv:(h + 1) * dv] = value

    _gdn_step(p_rows, ba_rows, put_out, s0_ref, c0_ref, cw_ref, alog_ref, dtb_ref, onw_ref,
              s_ref, c_ref, s_scr, xe_scr, heads=heads, dk=dk, dv=dv, nb=nb, cl=cl, nc=nc)


def _gdn_layer_kernel(x_ref, nw_ref, win_ref, wba_ref, s0_ref, c0_ref, cw_ref, alog_ref, dtb_ref,
                      onw_ref, wout_ref, y_ref, s_ref, c_ref, s_scr, xe_scr, o_scr,
                      *, heads, dk, dv, cl, nc):
    x = x_ref[0]
    ms = jnp.mean(x * x, axis=-1, keepdims=True)
    xn = (x * lax.rsqrt(ms + EPS) * nw_ref[...]).astype(BF16)
    p = _dot(xn, win_ref[...])
    ba = _dot(xn, wba_ref[...])

    def p_rows(r0, r1, lo, hi):
        return p[r0:r1, lo:hi]

    def ba_rows(r0, r1, lo, hi):
        return ba[r0:r1, lo:hi]

    def put_out(r0, r1, h, value):
        o_scr[r0:r1, h * dv:(h + 1) * dv] = value.astype(BF16)

    _gdn_step(p_rows, ba_rows, put_out, s0_ref, c0_ref, cw_ref, alog_ref, dtb_ref, onw_ref,
              s_ref, c_ref, s_scr, xe_scr, heads=heads, dk=dk, dv=dv, nb=1, cl=cl, nc=nc)
    y_ref[0] = x + _dot(o_scr[...], wout_ref[...])


def _gdn_layer(x, norm_w, w_main, w_ba, s0, c0, conv_w, a_log, dt_bias, onorm_w, w_out):
    b, l, d = x.shape
    heads, dk, dv = s0.shape[1:]
    conv_ch = c0.shape[2]
    pw = w_main.shape[1]
    cl, nc = CHUNK, GDN_CHUNKS_PER_STEP
    rows = cl * nc
    assert l % rows == 0 and heads <= SUBLANES and dk == LANES and dv == LANES
    assert pw == conv_ch + heads * dv and w_ba.shape[1] == 2 * LANES
    resident = dict(pipeline_mode=pl.Buffered(1))
    kern = functools.partial(_gdn_layer_kernel, heads=heads, dk=dk, dv=dv, cl=cl, nc=nc)
    return pl.pallas_call(
        kern, grid=(b, l // rows),
        in_specs=[pl.BlockSpec((1, rows, d), lambda i, t: (i, t, 0)),
                  pl.BlockSpec((1, d), lambda i, t: (0, 0)),
                  pl.BlockSpec((d, pw), lambda i, t: (0, 0), **resident),
                  pl.BlockSpec((d, 2 * LANES), lambda i, t: (0, 0), **resident),
                  pl.BlockSpec((1, heads, dk, dv), lambda i, t: (i, 0, 0, 0)),
                  pl.BlockSpec((1, CONV_W - 1, conv_ch), lambda i, t: (i, 0, 0)),
                  pl.BlockSpec((CONV_W, conv_ch), lambda i, t: (0, 0)),
                  pl.BlockSpec((1, LANES), lambda i, t: (0, 0)),
                  pl.BlockSpec((1, LANES), lambda i, t: (0, 0)),
                  pl.BlockSpec((1, dv), lambda i, t: (0, 0)),
                  pl.BlockSpec((heads * dv, d), lambda i, t: (0, 0), **resident)],
        out_specs=[pl.BlockSpec((1, rows, d), lambda i, t: (i, t, 0)),
                   pl.BlockSpec((1, heads, dk, dv), lambda i, t: (i, 0, 0, 0)),
                   pl.BlockSpec((1, CONV_W - 1, conv_ch), lambda i, t: (i, 0, 0))],
        out_shape=[jax.ShapeDtypeStruct((b, l, d), F32),
                   jax.ShapeDtypeStruct(s0.shape, F32),
                   jax.ShapeDtypeStruct(c0.shape, F32)],
        scratch_shapes=[pltpu.VMEM((1, heads, dk, dv), F32),
                        pltpu.VMEM((1, SUBLANES + rows, conv_ch), F32),
                        pltpu.VMEM((rows, heads * dv), BF16)],
        compiler_params=_params("parallel", "arbitrary"), name="gdn_layer",
    )(x, norm_w.reshape(1, d), w_main, w_ba, s0, c0, conv_w, _lane_pad(a_log),
      _lane_pad(dt_bias), onorm_w.reshape(1, dv), w_out)


def _lane_pad(vec):
    return jnp.pad(vec.astype(F32), (0, LANES - vec.shape[0])).reshape(1, LANES)


def _gdn_core(p, ba, s0, c0, conv_w, a_log, dt_bias, onorm_w):
    b, l, pw = p.shape
    heads, dk, dv = s0.shape[1:]
    conv_ch = c0.shape[2]
    assert heads <= SUBLANES and dk == LANES and dv == LANES and pw == conv_ch + heads * dv
    if l <= CHUNK:
        assert l % SUBLANES == 0 and l & (l - 1) == 0
        cl, nc = l, 1
        nb = max(n for n in range(1, CHUNK // l + 1) if b % n == 0)
    else:
        assert l % CHUNK == 0
        cl, nb = CHUNK, 1
        nc = GDN_CHUNKS_PER_STEP if l % (CHUNK * GDN_CHUNKS_PER_STEP) == 0 else 1
    groups, rows = b // nb, nb * cl * nc
    steps = l // (cl * nc)
    kern = functools.partial(_gdn_core_kernel, heads=heads, dk=dk, dv=dv, nb=nb, cl=cl, nc=nc)
    o, s, cs = pl.pallas_call(
        kern, grid=(groups, steps),
        in_specs=[pl.BlockSpec((1, rows, pw), lambda i, t: (i, t, 0)),
                  pl.BlockSpec((1, rows, 2 * LANES), lambda i, t: (i, t, 0)),
                  pl.BlockSpec((nb, heads, dk, dv), lambda i, t: (i, 0, 0, 0)),
                  pl.BlockSpec((nb, CONV_W - 1, conv_ch), lambda i, t: (i, 0, 0)),
                  pl.BlockSpec((CONV_W, conv_ch), lambda i, t: (0, 0)),
                  pl.BlockSpec((1, LANES), lambda i, t: (0, 0)),
                  pl.BlockSpec((1, LANES), lambda i, t: (0, 0)),
                  pl.BlockSpec((1, dv), lambda i, t: (0, 0))],
        out_specs=[pl.BlockSpec((1, rows, heads * dv), lambda i, t: (i, t, 0)),
                   pl.BlockSpec((nb, heads, dk, dv), lambda i, t: (i, 0, 0, 0)),
                   pl.BlockSpec((nb, CONV_W - 1, conv_ch), lambda i, t: (i, 0, 0))],
        out_shape=[jax.ShapeDtypeStruct((groups, nb * l, heads * dv), F32),
                   jax.ShapeDtypeStruct(s0.shape, F32),
                   jax.ShapeDtypeStruct(c0.shape, F32)],
        scratch_shapes=[pltpu.VMEM((nb, heads, dk, dv), F32),
                        pltpu.VMEM((nb, SUBLANES + cl * nc, conv_ch), F32)],
        compiler_params=_params("parallel", "arbitrary"), name="gdn_core",
    )(p.reshape(groups, nb * l, pw), ba.reshape(groups, nb * l, 2 * LANES), s0, c0, conv_w,
      _lane_pad(a_log), _lane_pad(dt_bias), onorm_w.reshape(1, dv))
    return o.reshape(b, l, heads * dv), s, cs


def _ret_chunk(cols, put_out, cos_ref, sin_ref, s0_ref, onw_ref, s_ref,
               *, heads, dk, dv, nb, c, single_step):
    qk_w = heads * dk
    g_off = 2 * qk_w + heads * dv
    half = dk // 2

    if not single_step:
        @pl.when(pl.program_id(1) == 0)
        def _():
            s_ref[...] = s0_ref[...]
    state_ref = s0_ref if single_step else s_ref

    cos = cos_ref[...]
    sin = sin_ref[...]
    row = lax.broadcasted_iota(jnp.int32, (c, c), 0)
    col = lax.broadcasted_iota(jnp.int32, (c, c), 1)
    diff = (row - col).astype(F32)
    idx = lax.broadcasted_iota(jnp.int32, (c, 1), 0).astype(F32)

    def rotary(x):
        x1, x2 = x[:, :half], x[:, half:]
        return jnp.concatenate([x1 * cos - x2 * sin, x1 * sin + x2 * cos], axis=-1)

    decay, q_dec, k_dec, s_dec = [], [], [], []
    for h in range(heads):
        lg = jnp.log(jnp.full((1, 1), 1.0 - 2.0 ** (-5.0 - h), F32))
        decay.append(jnp.where(diff >= 0, jnp.exp(lg * jnp.maximum(diff, 0.0)), 0.0))
        q_dec.append(jnp.exp(lg * (idx + 1.0)))
        k_dec.append(jnp.exp(lg * (c - 1.0 - idx)))
        s_dec.append(jnp.exp(lg * c))

    units = [(j, h) for j in range(nb) for h in range(heads)]
    qb, kb, kdb, vb = [], [], [], []
    for j, h in units:
        q = rotary(cols(j, h * dk, (h + 1) * dk))
        k = rotary(cols(j, qk_w + h * dk, qk_w + (h + 1) * dk)) * (dk ** -0.5)
        qb.append(q.astype(BF16))
        kb.append(k.astype(BF16))
        kdb.append((k * k_dec[h]).astype(BF16))
        vb.append(cols(j, 2 * qk_w + h * dv, 2 * qk_w + (h + 1) * dv).astype(BF16))
    qk = [(_dot_nt(qb[i], kb[i]) * decay[h]).astype(BF16) for i, (j, h) in enumerate(units)]
    s_old = [state_ref[j, h] for j, h in units]
    qs = [_dot(qb[i], s_old[i].astype(BF16)) for i in range(len(units))]
    o = [_dot(qk[i], vb[i]) + q_dec[h] * qs[i] for i, (j, h) in enumerate(units)]
    for i, (j, h) in enumerate(units):
        s_ref[j, h] = s_dec[h] * s_old[i] + _dot_tn(kdb[i], vb[i])
    for i, (j, h) in enumerate(units):
        on = (o[i] * lax.rsqrt(jnp.mean(o[i] * o[i], axis=-1, keepdims=True) + EPS)
              * onw_ref[h:h + 1, :])
        put_out(j, h, on * _silu(cols(j, g_off + h * dv, g_off + (h + 1) * dv)))


def _ret_core_kernel(p_ref, cos_ref, sin_ref, s0_ref, onw_ref, o_ref, s_ref,
                     *, heads, dk, dv, nb, c, single_step):
    def cols(j, lo, hi):
        return p_ref[0, j * c:(j + 1) * c, lo:hi]

    def put_out(j, h, value):
        o_ref[0, j * c:(j + 1) * c, h * dv:(h + 1) * dv] = value

    _ret_chunk(cols, put_out, cos_ref, sin_ref, s0_ref, onw_ref, s_ref,
               heads=heads, dk=dk, dv=dv, nb=nb, c=c, single_step=single_step)


def _ret_layer_kernel(x_ref, nw_ref, win_ref, cos_ref, sin_ref, s0_ref, onw_ref, wout_ref,
                      fw_ref, y_ref, s_ref, o_scr, *, heads, dk, dv, c, single_step):
    x = x_ref[0]
    ms = jnp.mean(x * x, axis=-1, keepdims=True)
    xn = (x * lax.rsqrt(ms + EPS) * nw_ref[...]).astype(BF16)
    p = _dot(xn, win_ref[...])

    def cols(j, lo, hi):
        return p[:, lo:hi]

    def put_out(j, h, value):
        o_scr[:, h * dv:(h + 1) * dv] = value.astype(BF16)

    _ret_chunk(cols, put_out, cos_ref, sin_ref, s0_ref, onw_ref, s_ref,
               heads=heads, dk=dk, dv=dv, nb=1, c=c, single_step=single_step)
    y = x + _dot(o_scr[...], wout_ref[...])
    ms = jnp.mean(y * y, axis=-1, keepdims=True)
    y_ref[0] = y * lax.rsqrt(ms + EPS) * fw_ref[...]


def _ret_layer(x, norm_w, w_in, cos, sin, s0, onorm_w, w_out, final_w):
    b, l, d = x.shape
    heads, dk, dv = s0.shape[1:]
    pw = w_in.shape[1]
    c = RET_CHUNK
    assert l % c == 0 and dk // 2 == LANES and pw == 2 * heads * dk + 2 * heads * dv
    resident = dict(pipeline_mode=pl.Buffered(1))
    kern = functools.partial(_ret_layer_kernel, heads=heads, dk=dk, dv=dv, c=c,
                             single_step=l == c)
    return pl.pallas_call(
        kern, grid=(b, l // c),
        in_specs=[pl.BlockSpec((1, c, d), lambda i, t: (i, t, 0)),
                  pl.BlockSpec((1, d), lambda i, t: (0, 0)),
                  pl.BlockSpec((d, pw), lambda i, t: (0, 0), **resident),
                  pl.BlockSpec((c, dk // 2), lambda i, t: (t, 0)),
                  pl.BlockSpec((c, dk // 2), lambda i, t: (t, 0)),
                  pl.BlockSpec((1, heads, dk, dv), lambda i, t: (i, 0, 0, 0)),
                  pl.BlockSpec((heads, dv), lambda i, t: (0, 0)),
                  pl.BlockSpec((heads * dv, d), lambda i, t: (0, 0), **resident),
                  pl.BlockSpec((1, d), lambda i, t: (0, 0))],
        out_specs=[pl.BlockSpec((1, c, d), lambda i, t: (i, t, 0)),
                   pl.BlockSpec((1, heads, dk, dv), lambda i, t: (i, 0, 0, 0))],
        out_shape=[jax.ShapeDtypeStruct((b, l, d), F32),
                   jax.ShapeDtypeStruct(s0.shape, F32)],
        scratch_shapes=[pltpu.VMEM((c, heads * dv), BF16)],
        compiler_params=_params("parallel", "arbitrary"), name="ret_layer",
    )(x, norm_w.reshape(1, d), w_in, cos, sin, s0, onorm_w, w_out, final_w.reshape(1, d))


def _ret_core(p, cos, sin, s0, onorm_w):
    b, l, pw = p.shape
    heads, dk, dv = s0.shape[1:]
    half = dk // 2
    assert half == LANES and pw == 2 * heads * dk + 2 * heads * dv
    if l <= CHUNK:
        assert l % SUBLANES == 0
        c = l
        nb = RET_SEQS_PER_STEP if b % RET_SEQS_PER_STEP == 0 else 1
    else:
        c = RET_CHUNK if l % RET_CHUNK == 0 else CHUNK
        assert l % c == 0
        nb = 1
    groups, steps = b // nb, l // c
    kern = functools.partial(_ret_core_kernel, heads=heads, dk=dk, dv=dv, nb=nb, c=c,
                             single_step=steps == 1)
    o, s = pl.pallas_call(
        kern, grid=(groups, steps),
        in_specs=[pl.BlockSpec((1, nb * c, pw), lambda i, t: (i, t, 0)),
                  pl.BlockSpec((c, half), lambda i, t: (t, 0)),
                  pl.BlockSpec((c, half), lambda i, t: (t, 0)),
                  pl.BlockSpec((nb, heads, dk, dv), lambda i, t: (i, 0, 0, 0)),
                  pl.BlockSpec((heads, dv), lambda i, t: (0, 0))],
        out_specs=[pl.BlockSpec((1, nb * c, heads * dv), lambda i, t: (i, t, 0)),
                   pl.BlockSpec((nb, heads, dk, dv), lambda i, t: (i, 0, 0, 0))],
        out_shape=[jax.ShapeDtypeStruct((groups, nb * l, heads * dv), F32),
                   jax.ShapeDtypeStruct(s0.shape, F32)],
        compiler_params=_params("parallel", "arbitrary"), name="ret_core",
    )(p.reshape(groups, nb * l, pw), cos, sin, s0, onorm_w)
    return o.reshape(b, l, heads * dv), s


def _row_tile(t):
    for tm in (256, 128, 64, 32, 16, 8):
        if t % tm == 0:
            return tm
    raise ValueError(f"token count {t} is not a multiple of {SUBLANES}")


def _rope_tables(l, half, pos0):
    inv = 1.0 / (ROPE_BASE ** jnp.linspace(0.0, 1.0, half, dtype=F32))
    pos = pos0 + jnp.arange(l, dtype=F32)
    ang = pos[:, None] * inv[None, :]
    return jnp.cos(ang), jnp.sin(ang)


def _trunk(x, pos0, gdn_s, gdn_conv, ret_s, norm_w, w_in_a, conv_w_a, a_log_a, dt_bias_a,
           onorm_a, w_out_a, w_in_b, onorm_b, w_out_b, final_norm_w):
    b, l, d = x.shape
    t = b * l
    tm = _row_tile(t)
    heads_a, dk_a, dv_a = gdn_s.shape[2:]
    main_a = 2 * heads_a * dk_a + 2 * heads_a * dv_a
    x2 = x.reshape(t, d)

    w_main = w_in_a[0][:, :main_a].astype(BF16)
    w_tail = jnp.zeros((d, 2 * LANES), F32)
    w_tail = w_tail.at[:, 0:heads_a].set(w_in_a[0][:, main_a:main_a + heads_a])
    w_tail = w_tail.at[:, LANES:LANES + heads_a].set(w_in_a[0][:, main_a + heads_a:])
    if l % (CHUNK * GDN_CHUNKS_PER_STEP) == 0:
        x3, s_a, c_a = _gdn_layer(x, norm_w[0], w_main, w_tail.astype(BF16), gdn_s[0],
                                  gdn_conv[0], conv_w_a[0], a_log_a[0], dt_bias_a[0],
                                  onorm_a[0], w_out_a[0].astype(BF16))
        x2 = x3.reshape(t, d)
    else:
        p, ba = _in_proj(x2, norm_w[0], [w_main, w_tail.astype(BF16)], tm)
        o, s_a, c_a = _gdn_core(p.reshape(b, l, main_a), ba.reshape(b, l, 2 * LANES),
                                gdn_s[0], gdn_conv[0], conv_w_a[0], a_log_a[0], dt_bias_a[0],
                                onorm_a[0])
        x2 = _out_proj(o.reshape(t, -1), x2, w_out_a[0].astype(BF16), final_norm_w, tm, False)

    heads_b, dk_b, dv_b = ret_s.shape[2:]
    cos, sin = _rope_tables(l, dk_b // 2, pos0)
    if l % RET_CHUNK == 0:
        y, s_b = _ret_layer(x2.reshape(b, l, d), norm_w[1], w_in_b[0].astype(BF16), cos, sin,
                            ret_s[0], onorm_b[0], w_out_b[0].astype(BF16), final_norm_w)
    else:
        (p,) = _in_proj(x2, norm_w[1], [w_in_b[0].astype(BF16)], tm)
        o, s_b = _ret_core(p.reshape(b, l, -1), cos, sin, ret_s[0], onorm_b[0])
        y = _out_proj(o.reshape(t, -1), x2, w_out_b[0].astype(BF16), final_norm_w, tm, True)
    return y.reshape(b, l, d), s_a[None], c_a[None], s_b[None]


def kernel(x_prompt, x_sample, state_gdn_ssm, state_gdn_conv, state_ret, norm_w, w_in_a,
           conv_w_a, a_log_a, dt_bias_a, onorm_a, w_out_a, w_in_b, onorm_b, w_out_b,
           final_norm_w):
    assert state_gdn_ssm.shape[0] == 1 and state_ret.shape[0] == 1 and norm_w.shape[0] == 2
    bp = x_prompt.shape[0]
    weights = (norm_w, w_in_a, conv_w_a, a_log_a, dt_bias_a, onorm_a, w_out_a, w_in_b, onorm_b,
               w_out_b, final_norm_w)
    z_sa = jnp.zeros((1, bp) + state_gdn_ssm.shape[2:], F32)
    z_ca = jnp.zeros((1, bp) + state_gdn_conv.shape[2:], F32)
    z_sb = jnp.zeros((1, bp) + state_ret.shape[2:], F32)
    y_p, sa_p, ca_p, sb_p = _trunk(x_prompt, 0.0, z_sa, z_ca, z_sb, *weights)
    y_s, sa_s, ca_s, sb_s = _trunk(x_sample, PAST_LEN, state_gdn_ssm, state_gdn_conv, state_ret,
                                   *weights)
    return (y_p, y_s, sa_p, ca_p, sb_p, sa_s, ca_s, sb_s)
```

```python
import functools

import jax
import jax.numpy as jnp
from jax import lax
from jax.experimental import pallas as pl
from jax.experimental.pallas import tpu as pltpu

F32 = jnp.float32
BF16 = jnp.bfloat16
EPS = 1e-6
CHUNK = 64
CONV_W = 4
ROPE_BASE = 10000.0
PAST_LEN = 16384.0
LANES = 128
SUBLANES = 8
VMEM_LIMIT_BYTES = 56 * 1024 * 1024
GDN_CHUNKS_PER_STEP = 4
RET_CHUNK = 256
CONV_SLAB = 512
SLABS_PER_FILL = 2
UNITS_PER_FILL = 4
IN_PROJ_TILE = 512
RET_SEQS_PER_STEP = 2


def _dot(a, b):
    return jnp.dot(a, b, preferred_element_type=F32)


def _dot_nt(a, b):
    return lax.dot_general(a, b, (((1,), (1,)), ((), ())), preferred_element_type=F32)


def _dot_tn(a, b):
    return lax.dot_general(a, b, (((0,), (0,)), ((), ())), preferred_element_type=F32)


def _split3(x):
    hi = x.astype(BF16)
    r = x - hi.astype(F32)
    mid = r.astype(BF16)
    lo = (r - mid.astype(F32)).astype(BF16)
    return hi, mid, lo


def _split2(x):
    hi = x.astype(BF16)
    return hi, (x - hi.astype(F32)).astype(BF16)


def _dot_x3(a, b):
    a_hi, a_lo = _split2(a)
    b_hi, b_lo = _split2(b)
    return _dot(a_hi, b_hi) + (_dot(a_hi, b_lo) + _dot(a_lo, b_hi))


def _sigmoid(x):
    return 1.0 / (1.0 + jnp.exp(-x))


def _silu(x):
    return x * _sigmoid(x)


def _softplus(x):
    return jnp.maximum(x, 0.0) + jnp.log1p(jnp.exp(-jnp.abs(x)))


def _params(*sem):
    return pltpu.CompilerParams(dimension_semantics=sem, vmem_limit_bytes=VMEM_LIMIT_BYTES)


def _in_proj_kernel(x_ref, nw_ref, *refs):
    n = len(refs) // 2
    w_refs, o_refs = refs[:n], refs[n:]
    x = x_ref[...]
    ms = jnp.mean(x * x, axis=-1, keepdims=True)
    xn = (x * lax.rsqrt(ms + EPS) * nw_ref[...]).astype(BF16)
    for w_ref, o_ref in zip(w_refs, o_refs):
        o_ref[...] = _dot(xn, w_ref[...])


def _in_proj(x, norm_w, weights, tm):
    t, d = x.shape
    assert t % tm == 0
    in_specs = [pl.BlockSpec((tm, d), lambda i: (i, 0)),
                pl.BlockSpec((1, d), lambda i: (0, 0))]
    in_specs += [pl.BlockSpec(w.shape, lambda i: (0, 0)) for w in weights]
    out_specs = [pl.BlockSpec((tm, w.shape[1]), lambda i: (i, 0)) for w in weights]
    out_shape = [jax.ShapeDtypeStruct((t, w.shape[1]), F32) for w in weights]
    return pl.pallas_call(
        _in_proj_kernel, grid=(t // tm,), in_specs=in_specs, out_specs=out_specs,
        out_shape=out_shape, compiler_params=_params("parallel"), name="in_proj",
    )(x, norm_w.reshape(1, d), *weights)


def _out_proj_kernel(o_ref, x_ref, w_ref, fw_ref, y_ref, *, final_norm):
    y = x_ref[...] + _dot(o_ref[...].astype(BF16), w_ref[...])
    if final_norm:
        ms = jnp.mean(y * y, axis=-1, keepdims=True)
        y = y * lax.rsqrt(ms + EPS) * fw_ref[...]
    y_ref[...] = y


def _out_proj(o, x, w, final_w, tm, final_norm):
    t, k = o.shape
    d = x.shape[1]
    assert t % tm == 0
    return pl.pallas_call(
        functools.partial(_out_proj_kernel, final_norm=final_norm),
        grid=(t // tm,),
        in_specs=[pl.BlockSpec((tm, k), lambda i: (i, 0)),
                  pl.BlockSpec((tm, d), lambda i: (i, 0)),
                  pl.BlockSpec((k, d), lambda i: (0, 0)),
                  pl.BlockSpec((1, d), lambda i: (0, 0))],
        out_specs=pl.BlockSpec((tm, d), lambda i: (i, 0)),
        out_shape=jax.ShapeDtypeStruct((t, d), F32),
        compiler_params=_params("parallel"), name="out_proj",
    )(o, x, w, final_w.reshape(1, d))


def _unit_lower_inverses(lows, eye, cl):
    powers = [(-low).astype(BF16) for low in lows]
    invs = [eye - low for low in lows]
    p = 1
    while 2 * p < cl:
        sq = [_dot(pw, pw) for pw in powers]
        p *= 2
        powers = [s.astype(BF16) for s in sq]
        invs = [inv + _dot(inv.astype(BF16), pw) for inv, pw in zip(invs, powers)]
    return invs


def _gdn_step(xall, z_rows, ba_rows, put_out, cw_ref, alog_ref, dtb_ref, onw_ref, s_scr,
              *, heads, dk, dv, nb, cl, nc, fill=lambda: None):
    c = nb * cl
    seq_rows = cl * nc
    qk_w = heads * dk
    conv_ch = 2 * qk_w + heads * dv

    row = lax.broadcasted_iota(jnp.int32, (c, c), 0)
    col = lax.broadcasted_iota(jnp.int32, (c, c), 1)
    if nb > 1:
        shift = cl.bit_length() - 1
        same = lax.shift_right_logical(row, shift) == lax.shift_right_logical(col, shift)
        incl = same & (row >= col)
        strict = same & (row > col)
    else:
        same = row >= 0
        incl = row >= col
        strict = row > col
    eye = (row == col).astype(F32)
    cum_masks = jnp.concatenate([incl.astype(BF16), same.astype(BF16)], axis=0)
    sel = (lax.broadcasted_iota(jnp.int32, (SUBLANES, LANES), 0)
           == lax.broadcasted_iota(jnp.int32, (SUBLANES, LANES), 1)).astype(BF16)
    neg_a = -jnp.exp(alog_ref[...])

    def chunk_rows(rows, ci, lo, hi):
        parts = [rows(j * seq_rows + ci * cl, j * seq_rows + (ci + 1) * cl, lo, hi)
                 for j in range(nb)]
        return parts[0] if nb == 1 else jnp.concatenate(parts, axis=0)

    units = [(ci, h) for ci in range(nc) for h in range(heads)]
    seqs = range(nb)

    def stack(parts):
        return parts[0] if nb == 1 else jnp.concatenate(parts, axis=0)

    conv = []
    for j in seqs:
        slabs = []
        for lo in range(0, conv_ch, CONV_SLAB):
            xs = xall[j][:, lo:lo + CONV_SLAB]
            acc = xs * cw_ref[0:1, lo:lo + CONV_SLAB]
            for tap in range(1, CONV_W):
                acc = pltpu.roll(acc, 1, axis=0) + xs * cw_ref[tap:tap + 1, lo:lo + CONV_SLAB]
            slabs.append(_silu(acc[SUBLANES:]))
            if len(slabs) % SLABS_PER_FILL == 0:
                fill()
        conv.append(slabs)

    def conv_cols(ci, lo, hi):
        s, off = divmod(lo, CONV_SLAB)
        assert off + hi - lo <= CONV_SLAB
        return stack([conv[j][s][ci * cl:(ci + 1) * cl, off:off + hi - lo] for j in seqs])

    beta, gcum, gtot, gcum_t = [], [], [], []
    for ci in range(nc):
        beta.append(_sigmoid(chunk_rows(ba_rows, ci, 0, LANES)))
        g = neg_a * _softplus(chunk_rows(ba_rows, ci, LANES, 2 * LANES) + dtb_ref[...])
        g3 = _split3(g)
        gg = _dot(cum_masks, g3[0]) + (_dot(cum_masks, g3[1]) + _dot(cum_masks, g3[2]))
        gcum.append(gg[:c])
        gtot.append(gg[c:])
        gc3 = _split3(gg[:c])
        gcum_t.append(_dot_nt(sel, gc3[0]) + (_dot_nt(sel, gc3[1]) + _dot_nt(sel, gc3[2])))

    q_l, eg_l, lows, rhs_l, qk_l, kd_l = [], [], [], [], [], []
    for n, (ci, h) in enumerate(units):
        if n % UNITS_PER_FILL == 0:
            fill()
        q = conv_cols(ci, h * dk, (h + 1) * dk)
        k = conv_cols(ci, qk_w + h * dk, qk_w + (h + 1) * dk)
        v = conv_cols(ci, 2 * qk_w + h * dv, 2 * qk_w + (h + 1) * dv)
        q = q * lax.rsqrt(jnp.sum(q * q, axis=-1, keepdims=True) + EPS) * (dk ** -0.5)
        k = k * lax.rsqrt(jnp.sum(k * k, axis=-1, keepdims=True) + EPS)
        gc = gcum[ci][:, h:h + 1]
        gr = gcum_t[ci][h:h + 1, :]
        bh = beta[ci][:, h:h + 1]
        dec = jnp.where(incl, jnp.exp(jnp.where(incl, gc - gr, 0.0)), 0.0)
        eg = jnp.exp(gc)
        qb = q.astype(BF16)
        kb = k.astype(BF16)
        lows.append(jnp.where(strict, bh * _dot_nt(kb, kb) * dec, 0.0))
        qk_l.append((_dot_nt(qb, kb) * dec).astype(BF16))
        rhs_l.append(jnp.concatenate([bh * v, (bh * eg) * k], axis=-1).astype(BF16))
        kd_l.append(k * jnp.exp(gtot[ci][:, h:h + 1] - gc))
        q_l.append(q)
        eg_l.append(eg)
    invs = _unit_lower_inverses(lows, eye, cl)
    sols = [_dot(inv.astype(BF16), rhs) for inv, rhs in zip(invs, rhs_l)]

    for ci in range(nc):
        idx = [ci * heads + h for h in range(heads)]
        s_old = [[s_scr[j, h] for j in seqs] for h in range(heads)]
        res = [[_dot(jnp.concatenate([sols[i][j * cl:(j + 1) * cl, dv:],
                                      q_l[i][j * cl:(j + 1) * cl]], axis=0).astype(BF16),
                     s_old[h][j].astype(BF16)) for j in seqs]
               for h, i in enumerate(idx)]
        u = [stack([sols[i][j * cl:(j + 1) * cl, :dv] - res[h][j][:cl] for j in seqs])
             for h, i in enumerate(idx)]
        o = [eg_l[i] * stack([res[h][j][cl:] for j in seqs]) + _dot(qk_l[i], u[h].astype(BF16))
             for h, i in enumerate(idx)]
        for h, i in enumerate(idx):
            for j in seqs:
                r0, r1 = j * cl, (j + 1) * cl
                s_scr[j, h] = (jnp.exp(gtot[ci][r0:r0 + 1, h:h + 1]) * s_old[h][j]
                               + _dot_tn(kd_l[i][r0:r1].astype(BF16), u[h][r0:r1].astype(BF16)))
        for h in range(heads):
            on = (o[h] * lax.rsqrt(jnp.mean(o[h] * o[h], axis=-1, keepdims=True) + EPS)
                  * onw_ref[...])
            out = on * _silu(chunk_rows(z_rows, ci, h * dv, (h + 1) * dv))
            for j in seqs:
                put_out(j * seq_rows + ci * cl, j * seq_rows + (ci + 1) * cl, h,
                        out[j * cl:(j + 1) * cl])


def _conv_history(xall):
    return xall[xall.shape[0] - (CONV_W - 1):]


HIST0 = SUBLANES - (CONV_W - 1)


def _gdn_core_kernel(p_ref, ba_ref, s0_ref, c0_ref, cw_ref, alog_ref, dtb_ref, onw_ref,
                     o_ref, s_ref, c_ref, s_scr, xe_scr, *, heads, dk, dv, nb, cl, nc):
    seq_rows = cl * nc
    conv_ch = xe_scr.shape[2]
    t = pl.program_id(1)

    @pl.when(t == 0)
    def _():
        s_scr[...] = s0_ref[...]
        for j in range(nb):
            xe_scr[j, 0:SUBLANES, :] = jnp.zeros((SUBLANES, conv_ch), F32)
            xe_scr[j, HIST0:SUBLANES, :] = c0_ref[j]

    for j in range(nb):
        xe_scr[j, SUBLANES:SUBLANES + seq_rows, :] = p_ref[0, j * seq_rows:(j + 1) * seq_rows,
                                                            0:conv_ch]
    xall = [xe_scr[j] for j in range(nb)]

    def z_rows(r0, r1, lo, hi):
        return p_ref[0, r0:r1, conv_ch + lo:conv_ch + hi]

    def ba_rows(r0, r1, lo, hi):
        return ba_ref[0, r0:r1, lo:hi]

    def put_out(r0, r1, h, value):
        o_ref[0, r0:r1, h * dv:(h + 1) * dv] = value

    _gdn_step(xall, z_rows, ba_rows, put_out, cw_ref, alog_ref, dtb_ref, onw_ref, s_scr,
              heads=heads, dk=dk, dv=dv, nb=nb, cl=cl, nc=nc)
    new_hist = [_conv_history(xall[j]) for j in range(nb)]
    for j in range(nb):
        xe_scr[j, HIST0:SUBLANES, :] = new_hist[j]

    @pl.when(t == pl.num_programs(1) - 1)
    def _():
        s_ref[...] = s_scr[...]
        for j in range(nb):
            c_ref[j] = new_hist[j]


def _gdn_layer_kernel(x_ref, xnext_ref, nw_ref, win_ref, wba_ref, s0_ref, c0_ref, cw_ref,
                      alog_ref, dtb_ref, onw_ref, wout_ref, y_ref, s_ref, c_ref,
                      s_scr, xe_scr, z_scr, ba_scr, o_scr, *, heads, dk, dv, cl, nc):
    rows = cl * nc
    conv_ch = xe_scr.shape[2]
    z_w = z_scr.shape[2]
    t = pl.program_id(1)
    step = pl.program_id(0) * pl.num_programs(1) + t
    slot = lax.rem(step, 2)

    def in_proj_tiles(src_ref, dst):
        x = src_ref[0]
        ms = jnp.mean(x * x, axis=-1, keepdims=True)
        xn = (x * lax.rsqrt(ms + EPS) * nw_ref[...]).astype(BF16)
        for lo in range(0, conv_ch, IN_PROJ_TILE):
            xe_scr[dst, SUBLANES:SUBLANES + rows, lo:lo + IN_PROJ_TILE] = _dot(
                xn, win_ref[:, lo:lo + IN_PROJ_TILE])
            yield
        for lo in range(0, z_w, IN_PROJ_TILE):
            z_scr[dst, :, lo:lo + IN_PROJ_TILE] = _dot(
                xn, win_ref[:, conv_ch + lo:conv_ch + lo + IN_PROJ_TILE])
            yield
        ba_scr[dst] = _dot(xn, wba_ref[...])
        yield

    @pl.when(step == 0)
    def _():
        for _ in in_proj_tiles(x_ref, 0):
            pass

    @pl.when(t == 0)
    def _():
        s_scr[...] = s0_ref[...]
        xe_scr[slot, 0:SUBLANES, :] = jnp.zeros((SUBLANES, conv_ch), F32)
        xe_scr[slot, HIST0:SUBLANES, :] = c0_ref[0]

    xall = xe_scr[slot]
    z = z_scr[slot]
    ba = ba_scr[slot]
    tiles = in_proj_tiles(xnext_ref, 1 - slot)

    def z_rows(r0, r1, lo, hi):
        return z[r0:r1, lo:hi]

    def ba_rows(r0, r1, lo, hi):
        return ba[r0:r1, lo:hi]

    def put_out(r0, r1, h, value):
        o_scr[r0:r1, h * dv:(h + 1) * dv] = value.astype(BF16)

    _gdn_step([xall], z_rows, ba_rows, put_out, cw_ref, alog_ref, dtb_ref, onw_ref, s_scr,
              heads=heads, dk=dk, dv=dv, nb=1, cl=cl, nc=nc, fill=lambda: next(tiles, None))
    for _ in tiles:
        pass
    new_hist = _conv_history(xall)
    xe_scr[1 - slot, 0:SUBLANES, :] = jnp.zeros((SUBLANES, conv_ch), F32)
    xe_scr[1 - slot, HIST0:SUBLANES, :] = new_hist
    y_ref[0] = x_ref[0] + _dot(o_scr[...], wout_ref[...])

    @pl.when(t == pl.num_programs(1) - 1)
    def _():
        s_ref[...] = s_scr[...]
        c_ref[0] = new_hist


def _gdn_layer(x, norm_w, w_main, w_ba, s0, c0, conv_w, a_log, dt_bias, onorm_w, w_out):
    b, l, d = x.shape
    heads, dk, dv = s0.shape[1:]
    conv_ch = c0.shape[2]
    pw = w_main.shape[1]
    cl, nc = CHUNK, GDN_CHUNKS_PER_STEP
    rows = cl * nc
    steps = l // rows
    z_w = heads * dv
    assert l % rows == 0 and heads <= SUBLANES and dk == LANES and dv == LANES
    assert pw == conv_ch + z_w and w_ba.shape[1] == 2 * LANES
    assert conv_ch % IN_PROJ_TILE == 0 and z_w % IN_PROJ_TILE == 0
    resident = dict(pipeline_mode=pl.Buffered(1))

    def next_block(i, t):
        nxt = jnp.minimum(i * steps + t + 1, b * steps - 1)
        return (nxt // steps, nxt % steps, 0)

    kern = functools.partial(_gdn_layer_kernel, heads=heads, dk=dk, dv=dv, cl=cl, nc=nc)
    return pl.pallas_call(
        kern, grid=(b, steps),
        in_specs=[pl.BlockSpec((1, rows, d), lambda i, t: (i, t, 0)),
                  pl.BlockSpec((1, rows, d), next_block),
                  pl.BlockSpec((1, d), lambda i, t: (0, 0)),
                  pl.BlockSpec((d, pw), lambda i, t: (0, 0), **resident),
                  pl.BlockSpec((d, 2 * LANES), lambda i, t: (0, 0), **resident),
                  pl.BlockSpec((1, heads, dk, dv), lambda i, t: (i, 0, 0, 0)),
                  pl.BlockSpec((1, CONV_W - 1, conv_ch), lambda i, t: (i, 0, 0)),
                  pl.BlockSpec((CONV_W, conv_ch), lambda i, t: (0, 0)),
                  pl.BlockSpec((1, LANES), lambda i, t: (0, 0)),
                  pl.BlockSpec((1, LANES), lambda i, t: (0, 0)),
                  pl.BlockSpec((1, dv), lambda i, t: (0, 0)),
                  pl.BlockSpec((heads * dv, d), lambda i, t: (0, 0), **resident)],
        out_specs=[pl.BlockSpec((1, rows, d), lambda i, t: (i, t, 0)),
                   pl.BlockSpec((1, heads, dk, dv), lambda i, t: (i, 0, 0, 0)),
                   pl.BlockSpec((1, CONV_W - 1, conv_ch), lambda i, t: (i, 0, 0))],
        out_shape=[jax.ShapeDtypeStruct((b, l, d), F32),
                   jax.ShapeDtypeStruct(s0.shape, F32),
                   jax.ShapeDtypeStruct(c0.shape, F32)],
        scratch_shapes=[pltpu.VMEM((1, heads, dk, dv), F32),
                        pltpu.VMEM((2, SUBLANES + rows, conv_ch), F32),
                        pltpu.VMEM((2, rows, z_w), F32),
                        pltpu.VMEM((2, rows, 2 * LANES), F32),
                        pltpu.VMEM((rows, z_w), BF16)],
        compiler_params=_params("arbitrary", "arbitrary"), name="gdn_layer",
    )(x, x, norm_w.reshape(1, d), w_main, w_ba, s0, c0, conv_w, _lane_pad(a_log),
      _lane_pad(dt_bias), onorm_w.reshape(1, dv), w_out)


def _lane_pad(vec):
    return jnp.pad(vec.astype(F32), (0, LANES - vec.shape[0])).reshape(1, LANES)


def _gdn_core(p, ba, s0, c0, conv_w, a_log, dt_bias, onorm_w):
    b, l, pw = p.shape
    heads, dk, dv = s0.shape[1:]
    conv_ch = c0.shape[2]
    assert heads <= SUBLANES and dk == LANES and dv == LANES and pw == conv_ch + heads * dv
    if l <= CHUNK:
        assert l % SUBLANES == 0 and l & (l - 1) == 0
        cl, nc = l, 1
        nb = max(n for n in range(1, CHUNK // l + 1) if b % n == 0)
    else:
        assert l % CHUNK == 0
        cl, nb = CHUNK, 1
        nc = GDN_CHUNKS_PER_STEP if l % (CHUNK * GDN_CHUNKS_PER_STEP) == 0 else 1
    groups, rows = b // nb, nb * cl * nc
    steps = l // (cl * nc)
    kern = functools.partial(_gdn_core_kernel, heads=heads, dk=dk, dv=dv, nb=nb, cl=cl, nc=nc)
    o, s, cs = pl.pallas_call(
        kern, grid=(groups, steps),
        in_specs=[pl.BlockSpec((1, rows, pw), lambda i, t: (i, t, 0)),
                  pl.BlockSpec((1, rows, 2 * LANES), lambda i, t: (i, t, 0)),
                  pl.BlockSpec((nb, heads, dk, dv), lambda i, t: (i, 0, 0, 0)),
                  pl.BlockSpec((nb, CONV_W - 1, conv_ch), lambda i, t: (i, 0, 0)),
                  pl.BlockSpec((CONV_W, conv_ch), lambda i, t: (0, 0)),
                  pl.BlockSpec((1, LANES), lambda i, t: (0, 0)),
                  pl.BlockSpec((1, LANES), lambda i, t: (0, 0)),
                  pl.BlockSpec((1, dv), lambda i, t: (0, 0))],
        out_specs=[pl.BlockSpec((1, rows, heads * dv), lambda i, t: (i, t, 0)),
                   pl.BlockSpec((nb, heads, dk, dv), lambda i, t: (i, 0, 0, 0)),
                   pl.BlockSpec((nb, CONV_W - 1, conv_ch), lambda i, t: (i, 0, 0))],
        out_shape=[jax.ShapeDtypeStruct((groups, nb * l, heads * dv), F32),
                   jax.ShapeDtypeStruct(s0.shape, F32),
                   jax.ShapeDtypeStruct(c0.shape, F32)],
        scratch_shapes=[pltpu.VMEM((nb, heads, dk, dv), F32),
                        pltpu.VMEM((nb, SUBLANES + cl * nc, conv_ch), F32)],
        compiler_params=_params("parallel", "arbitrary"), name="gdn_core",
    )(p.reshape(groups, nb * l, pw), ba.reshape(groups, nb * l, 2 * LANES), s0, c0, conv_w,
      _lane_pad(a_log), _lane_pad(dt_bias), onorm_w.reshape(1, dv))
    return o.reshape(b, l, heads * dv), s, cs


def _ret_chunk(cols, put_out, cos_ref, sin_ref, s0_ref, onw_ref, s_ref,
               *, heads, dk, dv, nb, c, first_step, fill=lambda: None):
    qk_w = heads * dk
    g_off = 2 * qk_w + heads * dv
    half = dk // 2

    if first_step is not None:
        @pl.when(first_step)
        def _():
            s_ref[...] = s0_ref[...]
    state_ref = s0_ref if first_step is None else s_ref

    cos = cos_ref[...]
    sin = sin_ref[...]
    row = lax.broadcasted_iota(jnp.int32, (c, c), 0)
    col = lax.broadcasted_iota(jnp.int32, (c, c), 1)
    diff = (row - col).astype(F32)
    idx = lax.broadcasted_iota(jnp.int32, (c, 1), 0).astype(F32)

    def rotary(x):
        x1, x2 = x[:, :half], x[:, half:]
        return jnp.concatenate([x1 * cos - x2 * sin, x1 * sin + x2 * cos], axis=-1)

    decay, q_dec, k_dec, s_dec = [], [], [], []
    for h in range(heads):
        lg = jnp.log(jnp.full((1, 1), 1.0 - 2.0 ** (-5.0 - h), F32))
        decay.append(jnp.where(diff >= 0, jnp.exp(lg * jnp.maximum(diff, 0.0)), 0.0))
        q_dec.append(jnp.exp(lg * (idx + 1.0)))
        k_dec.append(jnp.exp(lg * (c - 1.0 - idx)))
        s_dec.append(jnp.exp(lg * c))
    fill()

    units = [(j, h) for j in range(nb) for h in range(heads)]
    qb, kb, kdb, vb = [], [], [], []
    for j, h in units:
        q = rotary(cols(j, h * dk, (h + 1) * dk))
        k = rotary(cols(j, qk_w + h * dk, qk_w + (h + 1) * dk)) * (dk ** -0.5)
        qb.append(q.astype(BF16))
        kb.append(k.astype(BF16))
        kdb.append((k * k_dec[h]).astype(BF16))
        vb.append(cols(j, 2 * qk_w + h * dv, 2 * qk_w + (h + 1) * dv).astype(BF16))
        fill()
    qk = [(_dot_nt(qb[i], kb[i]) * decay[h]).astype(BF16) for i, (j, h) in enumerate(units)]
    fill()
    s_old = [state_ref[j, h] for j, h in units]
    qs = [_dot(qb[i], s_old[i].astype(BF16)) for i in range(len(units))]
    fill()
    o = [_dot(qk[i], vb[i]) + q_dec[h] * qs[i] for i, (j, h) in enumerate(units)]
    fill()
    for i, (j, h) in enumerate(units):
        s_ref[j, h] = s_dec[h] * s_old[i] + _dot_tn(kdb[i], vb[i])
        fill()
    for i, (j, h) in enumerate(units):
        on = (o[i] * lax.rsqrt(jnp.mean(o[i] * o[i], axis=-1, keepdims=True) + EPS)
              * onw_ref[h:h + 1, :])
        put_out(j, h, on * _silu(cols(j, g_off + h * dv, g_off + (h + 1) * dv)))
        fill()


def _ret_core_kernel(p_ref, cos_ref, sin_ref, s0_ref, onw_ref, o_ref, s_ref,
                     *, heads, dk, dv, nb, c, single_step):
    def cols(j, lo, hi):
        return p_ref[0, j * c:(j + 1) * c, lo:hi]

    def put_out(j, h, value):
        o_ref[0, j * c:(j + 1) * c, h * dv:(h + 1) * dv] = value

    _ret_chunk(cols, put_out, cos_ref, sin_ref, s0_ref, onw_ref, s_ref,
               heads=heads, dk=dk, dv=dv, nb=nb, c=c,
               first_step=None if single_step else pl.program_id(1) == 0)


def _ret_layer_kernel(x_ref, xnext_ref, nw_ref, win_ref, cos_ref, sin_ref, s0_ref, onw_ref,
                      wout_ref, fw_ref, y_ref, s_ref, p_scr, o_scr, *, heads, dk, dv, c):
    pw = win_ref.shape[1]
    step = pl.program_id(0) * pl.num_programs(1) + pl.program_id(1)
    slot = lax.rem(step, 2)

    def in_proj_tiles(src_ref, dst):
        x = src_ref[0]
        ms = jnp.mean(x * x, axis=-1, keepdims=True)
        xn = (x * lax.rsqrt(ms + EPS) * nw_ref[...]).astype(BF16)
        for lo in range(0, pw, IN_PROJ_TILE):
            p_scr[dst, :, lo:lo + IN_PROJ_TILE] = _dot(xn, win_ref[:, lo:lo + IN_PROJ_TILE])
            yield

    @pl.when(step == 0)
    def _():
        for _ in in_proj_tiles(x_ref, 0):
            pass

    p = p_scr[slot]
    tiles = in_proj_tiles(xnext_ref, 1 - slot)

    def cols(j, lo, hi):
        return p[:, lo:hi]

    def put_out(j, h, value):
        o_scr[:, h * dv:(h + 1) * dv] = value.astype(BF16)

    _ret_chunk(cols, put_out, cos_ref, sin_ref, s0_ref, onw_ref, s_ref,
               heads=heads, dk=dk, dv=dv, nb=1, c=c, first_step=pl.program_id(1) == 0,
               fill=lambda: next(tiles, None))
    for _ in tiles:
        pass
    x = x_ref[0]
    y = x + _dot(o_scr[...], wout_ref[...])
    ms = jnp.mean(y * y, axis=-1, keepdims=True)
    y_ref[0] = y * lax.rsqrt(ms + EPS) * fw_ref[...]


def _ret_layer(x, norm_w, w_in, cos, sin, s0, onorm_w, w_out, final_w):
    b, l, d = x.shape
    heads, dk, dv = s0.shape[1:]
    pw = w_in.shape[1]
    c = RET_CHUNK
    steps = l // c
    assert l % c == 0 and dk // 2 == LANES and pw == 2 * heads * dk + 2 * heads * dv
    assert pw % IN_PROJ_TILE == 0
    resident = dict(pipeline_mode=pl.Buffered(1))

    def next_block(i, t):
        nxt = jnp.minimum(i * steps + t + 1, b * steps - 1)
        return (nxt // steps, nxt % steps, 0)

    kern = functools.partial(_ret_layer_kernel, heads=heads, dk=dk, dv=dv, c=c)
    return pl.pallas_call(
        kern, grid=(b, steps),
        in_specs=[pl.BlockSpec((1, c, d), lambda i, t: (i, t, 0)),
                  pl.BlockSpec((1, c, d), next_block),
                  pl.BlockSpec((1, d), lambda i, t: (0, 0)),
                  pl.BlockSpec((d, pw), lambda i, t: (0, 0), **resident),
                  pl.BlockSpec((c, dk // 2), lambda i, t: (t, 0)),
                  pl.BlockSpec((c, dk // 2), lambda i, t: (t, 0)),
                  pl.BlockSpec((1, heads, dk, dv), lambda i, t: (i, 0, 0, 0)),
                  pl.BlockSpec((heads, dv), lambda i, t: (0, 0)),
                  pl.BlockSpec((heads * dv, d), lambda i, t: (0, 0), **resident),
                  pl.BlockSpec((1, d), lambda i, t: (0, 0))],
        out_specs=[pl.BlockSpec((1, c, d), lambda i, t: (i, t, 0)),
                   pl.BlockSpec((1, heads, dk, dv), lambda i, t: (i, 0, 0, 0))],
        out_shape=[jax.ShapeDtypeStruct((b, l, d), F32),
                   jax.ShapeDtypeStruct(s0.shape, F32)],
        scratch_shapes=[pltpu.VMEM((2, c, pw), F32),
                        pltpu.VMEM((c, heads * dv), BF16)],
        compiler_params=_params("arbitrary", "arbitrary"), name="ret_layer",
    )(x, x, norm_w.reshape(1, d), w_in, cos, sin, s0, onorm_w, w_out, final_w.reshape(1, d))


def _ret_core(p, cos, sin, s0, onorm_w):
    b, l, pw = p.shape
    heads, dk, dv = s0.shape[1:]
    half = dk // 2
    assert half == LANES and pw == 2 * heads * dk + 2 * heads * dv
    if l <= CHUNK:
        assert l % SUBLANES == 0
        c = l
        nb = RET_SEQS_PER_STEP if b % RET_SEQS_PER_STEP == 0 else 1
    else:
        c = RET_CHUNK if l % RET_CHUNK == 0 else CHUNK
        assert l % c == 0
        nb = 1
    groups, steps = b // nb, l // c
    kern = functools.partial(_ret_core_kernel, heads=heads, dk=dk, dv=dv, nb=nb, c=c,
                             single_step=steps == 1)
    o, s = pl.pallas_call(
        kern, grid=(groups, steps),
        in_specs=[pl.BlockSpec((1, nb * c, pw), lambda i, t: (i, t, 0)),
                  pl.BlockSpec((c, half), lambda i, t: (t, 0)),
                  pl.BlockSpec((c, half), lambda i, t: (t, 0)),
                  pl.BlockSpec((nb, heads, dk, dv), lambda i, t: (i, 0, 0, 0)),
                  pl.BlockSpec((heads, dv), lambda i, t: (0, 0))],
        out_specs=[pl.BlockSpec((1, nb * c, heads * dv), lambda i, t: (i, t, 0)),
                   pl.BlockSpec((nb, heads, dk, dv), lambda i, t: (i, 0, 0, 0))],
        out_shape=[jax.ShapeDtypeStruct((groups, nb * l, heads * dv), F32),
                   jax.ShapeDtypeStruct(s0.shape, F32)],
        compiler_params=_params("parallel", "arbitrary"), name="ret_core",
    )(p.reshape(groups, nb * l, pw), cos, sin, s0, onorm_w)
    return o.reshape(b, l, heads * dv), s


def _row_tile(t):
    for tm in (256, 128, 64, 32, 16, 8):
        if t % tm == 0:
            return tm
    raise ValueError(f"token count {t} is not a multiple of {SUBLANES}")


def _rope_tables(l, half, pos0):
    inv = 1.0 / (ROPE_BASE ** jnp.linspace(0.0, 1.0, half, dtype=F32))
    pos = pos0 + jnp.arange(l, dtype=F32)
    ang = pos[:, None] * inv[None, :]
    return jnp.cos(ang), jnp.sin(ang)


def _trunk(x, pos0, gdn_s, gdn_conv, ret_s, norm_w, w_in_a, conv_w_a, a_log_a, dt_bias_a,
           onorm_a, w_out_a, w_in_b, onorm_b, w_out_b, final_norm_w):
    b, l, d = x.shape
    t = b * l
    tm = _row_tile(t)
    heads_a, dk_a, dv_a = gdn_s.shape[2:]
    main_a = 2 * heads_a * dk_a + 2 * heads_a * dv_a
    x2 = x.reshape(t, d)

    w_main = w_in_a[0][:, :main_a].astype(BF16)
    w_tail = jnp.zeros((d, 2 * LANES), F32)
    w_tail = w_tail.at[:, 0:heads_a].set(w_in_a[0][:, main_a:main_a + heads_a])
    w_tail = w_tail.at[:, LANES:LANES + heads_a].set(w_in_a[0][:, main_a + heads_a:])
    if l % (CHUNK * GDN_CHUNKS_PER_STEP) == 0:
        x3, s_a, c_a = _gdn_layer(x, norm_w[0], w_main, w_tail.astype(BF16), gdn_s[0],
                                  gdn_conv[0], conv_w_a[0], a_log_a[0], dt_bias_a[0],
                                  onorm_a[0], w_out_a[0].astype(BF16))
        x2 = x3.reshape(t, d)
    else:
        p, ba = _in_proj(x2, norm_w[0], [w_main, w_tail.astype(BF16)], tm)
        o, s_a, c_a = _gdn_core(p.reshape(b, l, main_a), ba.reshape(b, l, 2 * LANES),
                                gdn_s[0], gdn_conv[0], conv_w_a[0], a_log_a[0], dt_bias_a[0],
                                onorm_a[0])
        x2 = _out_proj(o.reshape(t, -1), x2, w_out_a[0].astype(BF16), final_norm_w, tm, False)

    heads_b, dk_b, dv_b = ret_s.shape[2:]
    cos, sin = _rope_tables(l, dk_b // 2, pos0)
    if l % RET_CHUNK == 0:
        y, s_b = _ret_layer(x2.reshape(b, l, d), norm_w[1], w_in_b[0].astype(BF16), cos, sin,
                            ret_s[0], onorm_b[0], w_out_b[0].astype(BF16), final_norm_w)
    else:
        (p,) = _in_proj(x2, norm_w[1], [w_in_b[0].astype(BF16)], tm)
        o, s_b = _ret_core(p.reshape(b, l, -1), cos, sin, ret_s[0], onorm_b[0])
        y = _out_proj(o.reshape(t, -1), x2, w_out_b[0].astype(BF16), final_norm_w, tm, True)
    return y.reshape(b, l, d), s_a[None], c_a[None], s_b[None]


def kernel(x_prompt, x_sample, state_gdn_ssm, state_gdn_conv, state_ret, norm_w, w_in_a,
           conv_w_a, a_log_a, dt_bias_a, onorm_a, w_out_a, w_in_b, onorm_b, w_out_b,
           final_norm_w):
    assert state_gdn_ssm.shape[0] == 1 and state_ret.shape[0] == 1 and norm_w.shape[0] == 2
    bp = x_prompt.shape[0]
    weights = (norm_w, w_in_a, conv_w_a, a_log_a, dt_bias_a, onorm_a, w_out_a, w_in_b, onorm_b,
               w_out_b, final_norm_w)
    z_sa = jnp.zeros((1, bp) + state_gdn_ssm.shape[2:], F32)
    z_ca = jnp.zeros((1, bp) + state_gdn_conv.shape[2:], F32)
    z_sb = jnp.zeros((1, bp) + state_ret.shape[2:], F32)
    y_p, sa_p, ca_p, sb_p = _trunk(x_prompt, 0.0, z_sa, z_ca, z_sb, *weights)
    y_s, sa_s, ca_s, sb_s = _trunk(x_sample, PAST_LEN, state_gdn_ssm, state_gdn_conv, state_ret,
                                   *weights)
    return (y_p, y_s, sa_p, ca_p, sb_p, sa_s, ca_s, sb_s)
```

```python
import functools

import jax
import jax.numpy as jnp
from jax import lax
from jax.experimental import pallas as pl
from jax.experimental.pallas import tpu as pltpu

F32 = jnp.float32
BF16 = jnp.bfloat16
EPS = 1e-6
CHUNK = 64
CONV_W = 4
ROPE_BASE = 10000.0
PAST_LEN = 16384.0
LANES = 128
SUBLANES = 8
HIST0 = SUBLANES - (CONV_W - 1)
VMEM_LIMIT_BYTES = 60 * 1024 * 1024
GDN_CHUNKS_PER_STEP = 4
RET_CHUNK = 256
RET_SEQS_PER_STEP = 2


def _dot(a, b):
    return jnp.dot(a, b, preferred_element_type=F32)


def _dot_nt(a, b):
    return lax.dot_general(a, b, (((1,), (1,)), ((), ())), preferred_element_type=F32)


def _dot_tn(a, b):
    return lax.dot_general(a, b, (((0,), (0,)), ((), ())), preferred_element_type=F32)


def _split3(x):
    hi = x.astype(BF16)
    r = x - hi.astype(F32)
    mid = r.astype(BF16)
    lo = (r - mid.astype(F32)).astype(BF16)
    return hi, mid, lo


def _sigmoid(x):
    return 1.0 / (1.0 + jnp.exp(-x))


def _silu(x):
    return x * _sigmoid(x)


def _softplus(x):
    return jnp.maximum(x, 0.0) + jnp.log1p(jnp.exp(-jnp.abs(x)))


def _rms_scale(x, w):
    return x * lax.rsqrt(jnp.mean(x * x, axis=-1, keepdims=True) + EPS) * w


def _params(*sem):
    return pltpu.CompilerParams(dimension_semantics=sem, vmem_limit_bytes=VMEM_LIMIT_BYTES)


def _lane_pad(vec):
    return jnp.pad(vec.astype(F32), (0, LANES - vec.shape[0])).reshape(1, LANES)


def _in_proj_kernel(x_ref, nw_ref, w_ref, o_ref):
    o_ref[...] = _dot(_rms_scale(x_ref[...], nw_ref[...]).astype(BF16), w_ref[...])


def _in_proj(x, norm_w, w, tm):
    t, d = x.shape
    n = w.shape[1]
    assert t % tm == 0
    return pl.pallas_call(
        _in_proj_kernel, grid=(t // tm,),
        in_specs=[pl.BlockSpec((tm, d), lambda i: (i, 0)),
                  pl.BlockSpec((1, d), lambda i: (0, 0)),
                  pl.BlockSpec((d, n), lambda i: (0, 0))],
        out_specs=pl.BlockSpec((tm, n), lambda i: (i, 0)),
        out_shape=jax.ShapeDtypeStruct((t, n), F32),
        compiler_params=_params("parallel"), name="in_proj",
    )(x, norm_w.reshape(1, d), w)


def _out_proj_kernel(o_ref, x_ref, w_ref, fw_ref, y_ref, *, final_norm):
    y = x_ref[...] + _dot(o_ref[...].astype(BF16), w_ref[...])
    y_ref[...] = _rms_scale(y, fw_ref[...]) if final_norm else y


def _out_proj(o, x, w, final_w, tm, final_norm):
    t, k = o.shape
    d = x.shape[1]
    assert t % tm == 0
    return pl.pallas_call(
        functools.partial(_out_proj_kernel, final_norm=final_norm),
        grid=(t // tm,),
        in_specs=[pl.BlockSpec((tm, k), lambda i: (i, 0)),
                  pl.BlockSpec((tm, d), lambda i: (i, 0)),
                  pl.BlockSpec((k, d), lambda i: (0, 0)),
                  pl.BlockSpec((1, d), lambda i: (0, 0))],
        out_specs=pl.BlockSpec((tm, d), lambda i: (i, 0)),
        out_shape=jax.ShapeDtypeStruct((t, d), F32),
        compiler_params=_params("parallel"), name="out_proj",
    )(o, x, w, final_w.reshape(1, d))


def _unit_lower_inverses(lows, eye, cl):
    powers = [(-low).astype(BF16) for low in lows]
    invs = [eye - low for low in lows]
    p = 1
    while 2 * p < cl:
        sq = [_dot(pw, pw) for pw in powers]
        p *= 2
        powers = [s.astype(BF16) for s in sq]
        invs = [inv + _dot(inv.astype(BF16), pw) for inv, pw in zip(invs, powers)]
    return invs


def _gdn_step(p_rows, put_out, s0_ref, c0_ref, cw_ref, alog_ref, dtb_ref, onw_ref,
              s_ref, c_ref, s_scr, xe_scr, *, heads, dk, dv, nb, cl, nc):
    c = nb * cl
    seq_rows = cl * nc
    qk_w = heads * dk
    conv_ch = 2 * qk_w + heads * dv
    z_off = conv_ch
    b_off = conv_ch + heads * dv
    a_off = b_off + LANES
    t = pl.program_id(1)

    @pl.when(t == 0)
    def _():
        s_scr[...] = jnp.zeros(s_scr.shape, F32) if s0_ref is None else s0_ref[...]
        for j in range(nb):
            xe_scr[j, 0:SUBLANES, :] = jnp.zeros((SUBLANES, conv_ch), F32)
            if c0_ref is not None:
                xe_scr[j, HIST0:SUBLANES, :] = c0_ref[j]

    for j in range(nb):
        xe_scr[j, SUBLANES:SUBLANES + seq_rows, :] = p_rows(j * seq_rows, (j + 1) * seq_rows,
                                                            0, conv_ch)

    row = lax.broadcasted_iota(jnp.int32, (c, c), 0)
    col = lax.broadcasted_iota(jnp.int32, (c, c), 1)
    if nb > 1:
        shift = cl.bit_length() - 1
        same = lax.shift_right_logical(row, shift) == lax.shift_right_logical(col, shift)
        incl = same & (row >= col)
        strict = same & (row > col)
    else:
        same = row >= 0
        incl = row >= col
        strict = row > col
    eye = (row == col).astype(F32)
    cum_masks = jnp.concatenate([incl.astype(BF16), same.astype(BF16)], axis=0)
    sel = (lax.broadcasted_iota(jnp.int32, (SUBLANES, LANES), 0)
           == lax.broadcasted_iota(jnp.int32, (SUBLANES, LANES), 1)).astype(BF16)
    neg_a = -jnp.exp(alog_ref[...])

    def chunk_rows(ci, lo, hi):
        parts = [p_rows(j * seq_rows + ci * cl, j * seq_rows + (ci + 1) * cl, lo, hi)
                 for j in range(nb)]
        return parts[0] if nb == 1 else jnp.concatenate(parts, axis=0)

    units = [(ci, h) for ci in range(nc) for h in range(heads)]
    seqs = range(nb)

    def stack(parts):
        return parts[0] if nb == 1 else jnp.concatenate(parts, axis=0)

    conv = []
    for j in seqs:
        xall = xe_scr[j]
        acc = xall * cw_ref[0:1, :]
        for tap in range(1, CONV_W):
            acc = pltpu.roll(acc, 1, axis=0) + xall * cw_ref[tap:tap + 1, :]
        conv.append(acc[SUBLANES:])

    mixed, beta, gcum, gtot, gcum_t = [], [], [], [], []
    for ci in range(nc):
        mixed.append(_silu(stack([conv[j][ci * cl:(ci + 1) * cl] for j in seqs])))
        beta.append(_sigmoid(chunk_rows(ci, b_off, b_off + LANES)))
        g = neg_a * _softplus(chunk_rows(ci, a_off, a_off + LANES) + dtb_ref[...])
        g3 = _split3(g)
        gg = _dot(cum_masks, g3[0]) + (_dot(cum_masks, g3[1]) + _dot(cum_masks, g3[2]))
        gcum.append(gg[:c])
        gtot.append(gg[c:])
        gc3 = _split3(gg[:c])
        gcum_t.append(_dot_nt(sel, gc3[0]) + (_dot_nt(sel, gc3[1]) + _dot_nt(sel, gc3[2])))

    q_l, eg_l, lows, rhs_l, qk_l, kd_l = [], [], [], [], [], []
    for ci, h in units:
        q = mixed[ci][:, h * dk:(h + 1) * dk]
        k = mixed[ci][:, qk_w + h * dk:qk_w + (h + 1) * dk]
        v = mixed[ci][:, 2 * qk_w + h * dv:2 * qk_w + (h + 1) * dv]
        q = q * lax.rsqrt(jnp.sum(q * q, axis=-1, keepdims=True) + EPS) * (dk ** -0.5)
        k = k * lax.rsqrt(jnp.sum(k * k, axis=-1, keepdims=True) + EPS)
        gc = gcum[ci][:, h:h + 1]
        gr = gcum_t[ci][h:h + 1, :]
        bh = beta[ci][:, h:h + 1]
        dec = jnp.where(incl, jnp.exp(jnp.where(incl, gc - gr, 0.0)), 0.0)
        eg = jnp.exp(gc)
        qb = q.astype(BF16)
        kb = k.astype(BF16)
        lows.append(jnp.where(strict, bh * _dot_nt(kb, kb) * dec, 0.0))
        qk_l.append((_dot_nt(qb, kb) * dec).astype(BF16))
        rhs_l.append(jnp.concatenate([bh * v, (bh * eg) * k], axis=-1).astype(BF16))
        kd_l.append(k * jnp.exp(gtot[ci][:, h:h + 1] - gc))
        q_l.append(q)
        eg_l.append(eg)
    invs = _unit_lower_inverses(lows, eye, cl)
    sols = [_dot(inv.astype(BF16), rhs) for inv, rhs in zip(invs, rhs_l)]

    for ci in range(nc):
        idx = [ci * heads + h for h in range(heads)]
        s_old = [[s_scr[j, h] for j in seqs] for h in range(heads)]
        res = [[_dot(jnp.concatenate([sols[i][j * cl:(j + 1) * cl, dv:],
                                      q_l[i][j * cl:(j + 1) * cl]], axis=0).astype(BF16),
                     s_old[h][j].astype(BF16)) for j in seqs]
               for h, i in enumerate(idx)]
        u = [stack([sols[i][j * cl:(j + 1) * cl, :dv] - res[h][j][:cl] for j in seqs])
             for h, i in enumerate(idx)]
        o = [eg_l[i] * stack([res[h][j][cl:] for j in seqs]) + _dot(qk_l[i], u[h].astype(BF16))
             for h, i in enumerate(idx)]
        for h, i in enumerate(idx):
            for j in seqs:
                r0, r1 = j * cl, (j + 1) * cl
                s_scr[j, h] = (jnp.exp(gtot[ci][r0:r0 + 1, h:h + 1]) * s_old[h][j]
                               + _dot_tn(kd_l[i][r0:r1].astype(BF16), u[h][r0:r1].astype(BF16)))
        for h in range(heads):
            out = _rms_scale(o[h], onw_ref[...]) * _silu(
                chunk_rows(ci, z_off + h * dv, z_off + (h + 1) * dv))
            for j in seqs:
                put_out(j * seq_rows + ci * cl, j * seq_rows + (ci + 1) * cl, h,
                        out[j * cl:(j + 1) * cl])

    new_hist = [xe_scr[j, seq_rows + HIST0:seq_rows + SUBLANES, :] for j in range(nb)]
    for j in range(nb):
        xe_scr[j, HIST0:SUBLANES, :] = new_hist[j]

    @pl.when(t == pl.num_programs(1) - 1)
    def _():
        s_ref[...] = s_scr[...]
        for j in range(nb):
            c_ref[j] = new_hist[j]


def _gdn_core_kernel(p_ref, s0_ref, c0_ref, cw_ref, alog_ref, dtb_ref, onw_ref,
                     o_ref, s_ref, c_ref, s_scr, xe_scr, *, heads, dk, dv, nb, cl, nc):
    def p_rows(r0, r1, lo, hi):
        return p_ref[0, r0:r1, lo:hi]

    def put_out(r0, r1, h, value):
        o_ref[0, r0:r1, h * dv:(h + 1) * dv] = value

    _gdn_step(p_rows, put_out, s0_ref, c0_ref, cw_ref, alog_ref, dtb_ref, onw_ref,
              s_ref, c_ref, s_scr, xe_scr, heads=heads, dk=dk, dv=dv, nb=nb, cl=cl, nc=nc)


def _gdn_core(p, s0, c0, conv_w, a_log, dt_bias, onorm_w):
    b, l, pw = p.shape
    heads, dk, dv = s0.shape[1:]
    conv_ch = c0.shape[2]
    assert heads <= SUBLANES and dk == LANES and dv == LANES
    assert pw == conv_ch + heads * dv + 2 * LANES
    assert l <= CHUNK and l % SUBLANES == 0 and l & (l - 1) == 0
    nb = max(n for n in range(1, CHUNK // l + 1) if b % n == 0)
    groups, rows = b // nb, nb * l
    kern = functools.partial(_gdn_core_kernel, heads=heads, dk=dk, dv=dv, nb=nb, cl=l, nc=1)
    o, s, cs = pl.pallas_call(
        kern, grid=(groups, 1),
        in_specs=[pl.BlockSpec((1, rows, pw), lambda i, t: (i, 0, 0)),
                  pl.BlockSpec((nb, heads, dk, dv), lambda i, t: (i, 0, 0, 0)),
                  pl.BlockSpec((nb, CONV_W - 1, conv_ch), lambda i, t: (i, 0, 0)),
                  pl.BlockSpec((CONV_W, conv_ch), lambda i, t: (0, 0)),
                  pl.BlockSpec((1, LANES), lambda i, t: (0, 0)),
                  pl.BlockSpec((1, LANES), lambda i, t: (0, 0)),
                  pl.BlockSpec((1, dv), lambda i, t: (0, 0))],
        out_specs=[pl.BlockSpec((1, rows, heads * dv), lambda i, t: (i, 0, 0)),
                   pl.BlockSpec((nb, heads, dk, dv), lambda i, t: (i, 0, 0, 0)),
                   pl.BlockSpec((nb, CONV_W - 1, conv_ch), lambda i, t: (i, 0, 0))],
        out_shape=[jax.ShapeDtypeStruct((groups, rows, heads * dv), F32),
                   jax.ShapeDtypeStruct(s0.shape, F32),
                   jax.ShapeDtypeStruct(c0.shape, F32)],
        scratch_shapes=[pltpu.VMEM((nb, heads, dk, dv), F32),
                        pltpu.VMEM((nb, SUBLANES + l, conv_ch), F32)],
        compiler_params=_params("parallel", "arbitrary"), name="gdn_core",
    )(p.reshape(groups, rows, pw), s0, c0, conv_w, _lane_pad(a_log), _lane_pad(dt_bias),
      onorm_w.reshape(1, dv))
    return o.reshape(b, l, heads * dv), s, cs


def _gdn_layer_kernel(x_ref, nw_ref, win_ref, cw_ref, alog_ref, dtb_ref, onw_ref, wout_ref,
                      y_ref, s_ref, c_ref, s_scr, xe_scr, o_scr, *, heads, dk, dv, cl, nc):
    x = x_ref[0]
    p = _dot(_rms_scale(x, nw_ref[...]).astype(BF16), win_ref[...])

    def p_rows(r0, r1, lo, hi):
        return p[r0:r1, lo:hi]

    def put_out(r0, r1, h, value):
        o_scr[r0:r1, h * dv:(h + 1) * dv] = value.astype(BF16)

    _gdn_step(p_rows, put_out, None, None, cw_ref, alog_ref, dtb_ref, onw_ref,
              s_ref, c_ref, s_scr, xe_scr, heads=heads, dk=dk, dv=dv, nb=1, cl=cl, nc=nc)
    y_ref[0] = x + _dot(o_scr[...], wout_ref[...])


def _gdn_layer(x, norm_w, w_in, conv_w, a_log, dt_bias, onorm_w, w_out, heads):
    b, l, d = x.shape
    dv = w_out.shape[0] // heads
    dk = dv
    conv_ch = conv_w.shape[1]
    pw = w_in.shape[1]
    cl, nc = CHUNK, GDN_CHUNKS_PER_STEP
    rows = cl * nc
    assert l % rows == 0 and heads <= SUBLANES and dk == LANES and dv == LANES
    assert conv_ch == 3 * heads * dk and pw == conv_ch + heads * dv + 2 * LANES
    resident = dict(pipeline_mode=pl.Buffered(1))
    kern = functools.partial(_gdn_layer_kernel, heads=heads, dk=dk, dv=dv, cl=cl, nc=nc)
    return pl.pallas_call(
        kern, grid=(b, l // rows),
        in_specs=[pl.BlockSpec((1, rows, d), lambda i, t: (i, t, 0)),
                  pl.BlockSpec((1, d), lambda i, t: (0, 0)),
                  pl.BlockSpec((d, pw), lambda i, t: (0, 0), **resident),
                  pl.BlockSpec((CONV_W, conv_ch), lambda i, t: (0, 0)),
                  pl.BlockSpec((1, LANES), lambda i, t: (0, 0)),
                  pl.BlockSpec((1, LANES), lambda i, t: (0, 0)),
                  pl.BlockSpec((1, dv), lambda i, t: (0, 0)),
                  pl.BlockSpec((heads * dv, d), lambda i, t: (0, 0), **resident)],
        out_specs=[pl.BlockSpec((1, rows, d), lambda i, t: (i, t, 0)),
                   pl.BlockSpec((1, heads, dk, dv), lambda i, t: (i, 0, 0, 0)),
                   pl.BlockSpec((1, CONV_W - 1, conv_ch), lambda i, t: (i, 0, 0))],
        out_shape=[jax.ShapeDtypeStruct((b, l, d), F32),
                   jax.ShapeDtypeStruct((b, heads, dk, dv), F32),
                   jax.ShapeDtypeStruct((b, CONV_W - 1, conv_ch), F32)],
        scratch_shapes=[pltpu.VMEM((1, heads, dk, dv), F32),
                        pltpu.VMEM((1, SUBLANES + rows, conv_ch), F32),
                        pltpu.VMEM((rows, heads * dv), BF16)],
        compiler_params=_params("parallel", "arbitrary"), name="gdn_layer",
    )(x, norm_w.reshape(1, d), w_in, conv_w, _lane_pad(a_log), _lane_pad(dt_bias),
      onorm_w.reshape(1, dv), w_out)


def _ret_chunk(cols, put_out, cos_ref, sin_ref, s0_ref, onw_ref, s_ref,
               *, heads, dk, dv, nb, c, first_step):
    qk_w = heads * dk
    g_off = 2 * qk_w + heads * dv
    half = dk // 2

    if first_step is not None:
        @pl.when(first_step)
        def _():
            s_ref[...] = jnp.zeros(s_ref.shape, F32) if s0_ref is None else s0_ref[...]
    state_ref = s0_ref if first_step is None else s_ref

    cos = cos_ref[...]
    sin = sin_ref[...]
    row = lax.broadcasted_iota(jnp.int32, (c, c), 0)
    col = lax.broadcasted_iota(jnp.int32, (c, c), 1)
    diff = (row - col).astype(F32)
    idx = lax.broadcasted_iota(jnp.int32, (c, 1), 0).astype(F32)

    def rotary(x):
        x1, x2 = x[:, :half], x[:, half:]
        return jnp.concatenate([x1 * cos - x2 * sin, x1 * sin + x2 * cos], axis=-1)

    decay, q_dec, k_dec, s_dec = [], [], [], []
    for h in range(heads):
        lg = jnp.log(jnp.full((1, 1), 1.0 - 2.0 ** (-5.0 - h), F32))
        decay.append(jnp.where(diff >= 0, jnp.exp(lg * jnp.maximum(diff, 0.0)), 0.0))
        q_dec.append(jnp.exp(lg * (idx + 1.0)))
        k_dec.append(jnp.exp(lg * (c - 1.0 - idx)))
        s_dec.append(jnp.exp(lg * c))

    units = [(j, h) for j in range(nb) for h in range(heads)]
    qb, kb, kdb, vb = [], [], [], []
    for j, h in units:
        q = rotary(cols(j, h * dk, (h + 1) * dk))
        k = rotary(cols(j, qk_w + h * dk, qk_w + (h + 1) * dk)) * (dk ** -0.5)
        qb.append(q.astype(BF16))
        kb.append(k.astype(BF16))
        kdb.append((k * k_dec[h]).astype(BF16))
        vb.append(cols(j, 2 * qk_w + h * dv, 2 * qk_w + (h + 1) * dv).astype(BF16))
    qk = [(_dot_nt(qb[i], kb[i]) * decay[h]).astype(BF16) for i, (j, h) in enumerate(units)]
    s_old = [state_ref[j, h] for j, h in units]
    qs = [_dot(qb[i], s_old[i].astype(BF16)) for i in range(len(units))]
    o = [_dot(qk[i], vb[i]) + q_dec[h] * qs[i] for i, (j, h) in enumerate(units)]
    for i, (j, h) in enumerate(units):
        s_ref[j, h] = s_dec[h] * s_old[i] + _dot_tn(kdb[i], vb[i])
    for i, (j, h) in enumerate(units):
        put_out(j, h, _rms_scale(o[i], onw_ref[h:h + 1, :])
                * _silu(cols(j, g_off + h * dv, g_off + (h + 1) * dv)))


def _ret_side_chunk(ps_ref, cos_ref, sin_ref, s0_ref, onw_ref, o_ref, s_ref, *, heads, dk, dv,
                    nb):
    c = ps_ref.shape[1] // nb

    def cols(j, lo, hi):
        return ps_ref[0, j * c:(j + 1) * c, lo:hi]

    def put_out(j, h, value):
        o_ref[0, j * c:(j + 1) * c, h * dv:(h + 1) * dv] = value

    _ret_chunk(cols, put_out, cos_ref, sin_ref, s0_ref, onw_ref, s_ref,
               heads=heads, dk=dk, dv=dv, nb=nb, c=c, first_step=None)


def _ret_core_kernel(ps_ref, cos_ref, sin_ref, s0_ref, onw_ref, o_ref, s_ref,
                     *, heads, dk, dv, nb):
    _ret_side_chunk(ps_ref, cos_ref, sin_ref, s0_ref, onw_ref, o_ref, s_ref,
                    heads=heads, dk=dk, dv=dv, nb=nb)


def _ret_side_specs(p_s, s0_s, nbs, group):
    bs, ls, pw = p_s.shape
    heads, dk, dv = s0_s.shape[1:]
    assert bs % nbs == 0 and ls <= CHUNK and ls % SUBLANES == 0 and dk // 2 == LANES
    assert pw == 2 * heads * dk + 2 * heads * dv
    in_specs = [pl.BlockSpec((1, nbs * ls, pw), lambda *g: (group(*g), 0, 0)),
                pl.BlockSpec((ls, dk // 2), lambda *g: (0, 0)),
                pl.BlockSpec((ls, dk // 2), lambda *g: (0, 0)),
                pl.BlockSpec((nbs, heads, dk, dv), lambda *g: (group(*g), 0, 0, 0))]
    out_specs = [pl.BlockSpec((1, nbs * ls, heads * dv), lambda *g: (group(*g), 0, 0)),
                 pl.BlockSpec((nbs, heads, dk, dv), lambda *g: (group(*g), 0, 0, 0))]
    out_shape = [jax.ShapeDtypeStruct((bs // nbs, nbs * ls, heads * dv), F32),
                 jax.ShapeDtypeStruct(s0_s.shape, F32)]
    return in_specs, out_specs, out_shape


def _ret_core(p_s, cos_s, sin_s, s0_s, onorm_w):
    bs, ls, pw = p_s.shape
    heads, dk, dv = s0_s.shape[1:]
    nbs = RET_SEQS_PER_STEP if bs % RET_SEQS_PER_STEP == 0 else 1
    in_specs, out_specs, out_shape = _ret_side_specs(p_s, s0_s, nbs, lambda i: i)
    in_specs.append(pl.BlockSpec((heads, dv), lambda i: (0, 0)))
    o, s = pl.pallas_call(
        functools.partial(_ret_core_kernel, heads=heads, dk=dk, dv=dv, nb=nbs),
        grid=(bs // nbs,), in_specs=in_specs, out_specs=out_specs, out_shape=out_shape,
        compiler_params=_params("parallel"), name="ret_core",
    )(p_s.reshape(bs // nbs, nbs * ls, pw), cos_s, sin_s, s0_s, onorm_w)
    return o.reshape(bs, ls, heads * dv), s


def _ret_layer_kernel(*refs, heads, dk, dv, c, side_seqs):
    x_ref, nw_ref, win_ref, cos_ref, sin_ref, onw_ref, wout_ref, fw_ref = refs[:8]
    if side_seqs:
        ps_ref, cos_s_ref, sin_s_ref, s0s_ref, y_ref, s_ref, os_ref, ss_ref, o_scr = refs[8:]
    else:
        y_ref, s_ref, o_scr = refs[8:]

    x = x_ref[0]
    p = _dot(_rms_scale(x, nw_ref[...]).astype(BF16), win_ref[...])

    def cols(j, lo, hi):
        return p[:, lo:hi]

    def put_out(j, h, value):
        o_scr[:, h * dv:(h + 1) * dv] = value.astype(BF16)

    _ret_chunk(cols, put_out, cos_ref, sin_ref, None, onw_ref, s_ref,
               heads=heads, dk=dk, dv=dv, nb=1, c=c, first_step=pl.program_id(1) == 0)
    if side_seqs:
        _ret_side_chunk(ps_ref, cos_s_ref, sin_s_ref, s0s_ref, onw_ref, os_ref, ss_ref,
                        heads=heads, dk=dk, dv=dv, nb=side_seqs)
    y_ref[0] = _rms_scale(x + _dot(o_scr[...], wout_ref[...]), fw_ref[...])


def _ret_layer(x, norm_w, w_in, cos, sin, onorm_w, w_out, final_w, heads, side=None):
    b, l, d = x.shape
    pw = w_in.shape[1]
    dv = w_out.shape[0] // heads
    dk = (pw - 2 * heads * dv) // (2 * heads)
    c = RET_CHUNK
    steps = l // c
    assert l % c == 0 and dk // 2 == LANES
    resident = dict(pipeline_mode=pl.Buffered(1))
    in_specs = [pl.BlockSpec((1, c, d), lambda i, t: (i, t, 0)),
                pl.BlockSpec((1, d), lambda i, t: (0, 0)),
                pl.BlockSpec((d, pw), lambda i, t: (0, 0), **resident),
                pl.BlockSpec((c, dk // 2), lambda i, t: (t, 0)),
                pl.BlockSpec((c, dk // 2), lambda i, t: (t, 0)),
                pl.BlockSpec((heads, dv), lambda i, t: (0, 0)),
                pl.BlockSpec((heads * dv, d), lambda i, t: (0, 0), **resident),
                pl.BlockSpec((1, d), lambda i, t: (0, 0))]
    out_specs = [pl.BlockSpec((1, c, d), lambda i, t: (i, t, 0)),
                 pl.BlockSpec((1, heads, dk, dv), lambda i, t: (i, 0, 0, 0))]
    out_shape = [jax.ShapeDtypeStruct((b, l, d), F32),
                 jax.ShapeDtypeStruct((b, heads, dk, dv), F32)]
    args = [x, norm_w.reshape(1, d), w_in, cos, sin, onorm_w, w_out, final_w.reshape(1, d)]
    nbs = 0
    if side is not None:
        p_s, cos_s, sin_s, s0_s = side
        bs, ls, _ = p_s.shape
        nbs = RET_SEQS_PER_STEP
        assert bs == nbs * b * steps and p_s.shape[2] == pw
        assert s0_s.shape == (bs, heads, dk, dv)
        side_in, side_out, side_shape = _ret_side_specs(p_s, s0_s, nbs,
                                                        lambda i, t: i * steps + t)
        in_specs += side_in
        out_specs += side_out
        out_shape += side_shape
        args += [p_s.reshape(bs // nbs, nbs * ls, pw), cos_s, sin_s, s0_s]
    kern = functools.partial(_ret_layer_kernel, heads=heads, dk=dk, dv=dv, c=c, side_seqs=nbs)
    outs = pl.pallas_call(
        kern, grid=(b, steps), in_specs=in_specs, out_specs=out_specs, out_shape=out_shape,
        scratch_shapes=[pltpu.VMEM((c, heads * dv), BF16)],
        compiler_params=_params("parallel", "arbitrary"), name="ret_layer",
    )(*args)
    if side is None:
        return outs
    y, s, o_s, s_s = outs
    return y, s, o_s.reshape(bs, ls, heads * dv), s_s


def _row_tile(t):
    for tm in (256, 128, 64, 32, 16, 8):
        if t % tm == 0:
            return tm
    raise ValueError(f"token count {t} is not a multiple of {SUBLANES}")


def _rope_tables(l, half, pos0):
    inv = 1.0 / (ROPE_BASE ** jnp.linspace(0.0, 1.0, half, dtype=F32))
    pos = pos0 + jnp.arange(l, dtype=F32)
    ang = pos[:, None] * inv[None, :]
    return jnp.cos(ang), jnp.sin(ang)


def _gdn_in_proj_weight(w, heads, main):
    zeros = jnp.zeros((w.shape[0], LANES - heads), w.dtype)
    return jnp.concatenate([w[:, :main], w[:, main:main + heads], zeros, w[:, main + heads:],
                            zeros], axis=1).astype(BF16)


def kernel(x_prompt, x_sample, state_gdn_ssm, state_gdn_conv, state_ret, norm_w, w_in_a,
           conv_w_a, a_log_a, dt_bias_a, onorm_a, w_out_a, w_in_b, onorm_b, w_out_b,
           final_norm_w):
    assert state_gdn_ssm.shape[0] == 1 and state_ret.shape[0] == 1 and norm_w.shape[0] == 2
    bp, lp, d = x_prompt.shape
    bs, ls, _ = x_sample.shape
    heads_a, dk_a, dv_a = state_gdn_ssm.shape[2:]
    heads_b, dk_b, dv_b = state_ret.shape[2:]
    main_a = 2 * heads_a * dk_a + 2 * heads_a * dv_a
    win_a = _gdn_in_proj_weight(w_in_a[0], heads_a, main_a)
    wout_a = w_out_a[0].astype(BF16)
    win_b = w_in_b[0].astype(BF16)
    wout_b = w_out_b[0].astype(BF16)

    ts = bs * ls
    tm = _row_tile(ts)
    xs = x_sample.reshape(ts, d)
    p = _in_proj(xs, norm_w[0], win_a, tm)
    o, sa_s, ca_s = _gdn_core(p.reshape(bs, ls, -1), state_gdn_ssm[0], state_gdn_conv[0],
                              conv_w_a[0], a_log_a[0], dt_bias_a[0], onorm_a[0])
    xs = _out_proj(o.reshape(ts, -1), xs, wout_a, final_norm_w, tm, False)
    p_s = _in_proj(xs, norm_w[1], win_b, tm).reshape(bs, ls, -1)
    cos_s, sin_s = _rope_tables(ls, dk_b // 2, PAST_LEN)

    x1, sa_p, ca_p = _gdn_layer(x_prompt, norm_w[0], win_a, conv_w_a[0], a_log_a[0],
                                dt_bias_a[0], onorm_a[0], wout_a, heads_a)
    cos_p, sin_p = _rope_tables(lp, dk_b // 2, 0.0)
    if bs == RET_SEQS_PER_STEP * bp * (lp // RET_CHUNK):
        y_p, sb_p, o, sb_s = _ret_layer(x1, norm_w[1], win_b, cos_p, sin_p, onorm_b[0], wout_b,
                                        final_norm_w, heads_b,
                                        side=(p_s, cos_s, sin_s, state_ret[0]))
    else:
        y_p, sb_p = _ret_layer(x1, norm_w[1], win_b, cos_p, sin_p, onorm_b[0], wout_b,
                               final_norm_w, heads_b)
        o, sb_s = _ret_core(p_s, cos_s, sin_s, state_ret[0], onorm_b[0])
    y_s = _out_proj(o.reshape(ts, -1), xs, wout_b, final_norm_w, tm, True).reshape(bs, ls, d)
    return (y_p, y_s, sa_p[None], ca_p[None], sb_p[None], sa_s[None], ca_s[None], sb_s[None])
```

```python
import functools

import jax
import jax.numpy as jnp
from jax import lax
from jax.experimental import pallas as pl
from jax.experimental.pallas import tpu as pltpu

F32 = jnp.float32
BF16 = jnp.bfloat16
EPS = 1e-6
CHUNK = 64
CONV_W = 4
ROPE_BASE = 10000.0
PAST_LEN = 16384.0
LANES = 128
SUBLANES = 8
HIST0 = SUBLANES - (CONV_W - 1)
VMEM_LIMIT_BYTES = 60 * 1024 * 1024
GDN_CHUNKS_PER_STEP = 4
RET_CHUNK = 256
RET_SEQS_PER_STEP = 2


def _dot(a, b):
    return jnp.dot(a, b, preferred_element_type=F32)


def _dot_nt(a, b):
    return lax.dot_general(a, b, (((1,), (1,)), ((), ())), preferred_element_type=F32)


def _dot_tn(a, b):
    return lax.dot_general(a, b, (((0,), (0,)), ((), ())), preferred_element_type=F32)


def _split3(x):
    hi = x.astype(BF16)
    r = x - hi.astype(F32)
    mid = r.astype(BF16)
    lo = (r - mid.astype(F32)).astype(BF16)
    return hi, mid, lo


def _sigmoid(x):
    return 1.0 / (1.0 + jnp.exp(-x))


def _silu(x):
    return x * _sigmoid(x)


def _softplus(x):
    return jnp.maximum(x, 0.0) + jnp.log1p(jnp.exp(-jnp.abs(x)))


def _rms_scale(x, w):
    return x * lax.rsqrt(jnp.mean(x * x, axis=-1, keepdims=True) + EPS) * w


def _params(*sem):
    return pltpu.CompilerParams(dimension_semantics=sem, vmem_limit_bytes=VMEM_LIMIT_BYTES)


def _lane_pad(vec):
    return jnp.pad(vec.astype(F32), (0, LANES - vec.shape[0])).reshape(1, LANES)


def _cast_kernel(w_ref, o_ref):
    o_ref[...] = w_ref[0].astype(BF16)


def _cast_bf16(w, n_out=None):
    _, rows, n = w.shape
    n_out = n if n_out is None else n_out
    tr = _row_tile(rows)
    assert n_out % LANES == 0 and n_out <= n
    return pl.pallas_call(
        _cast_kernel, grid=(rows // tr,),
        in_specs=[pl.BlockSpec((1, tr, n_out), lambda i: (0, i, 0))],
        out_specs=pl.BlockSpec((tr, n_out), lambda i: (i, 0)),
        out_shape=jax.ShapeDtypeStruct((rows, n_out), BF16),
        compiler_params=_params("parallel"), name="cast_bf16",
    )(w)


def _in_proj_kernel(x_ref, nw_ref, *refs):
    n = len(refs) // 2
    xn = _rms_scale(x_ref[...], nw_ref[...]).astype(BF16)
    for w_ref, o_ref in zip(refs[:n], refs[n:]):
        o_ref[...] = _dot(xn, w_ref[...])


def _in_proj(x, norm_w, weights, tm):
    t, d = x.shape
    assert t % tm == 0
    in_specs = [pl.BlockSpec((tm, d), lambda i: (i, 0)),
                pl.BlockSpec((1, d), lambda i: (0, 0))]
    in_specs += [pl.BlockSpec(w.shape, lambda i: (0, 0)) for w in weights]
    return pl.pallas_call(
        _in_proj_kernel, grid=(t // tm,), in_specs=in_specs,
        out_specs=[pl.BlockSpec((tm, w.shape[1]), lambda i: (i, 0)) for w in weights],
        out_shape=[jax.ShapeDtypeStruct((t, w.shape[1]), F32) for w in weights],
        compiler_params=_params("parallel"), name="in_proj",
    )(x, norm_w.reshape(1, d), *weights)


def _out_proj_kernel(o_ref, x_ref, w_ref, fw_ref, y_ref, *, final_norm):
    y = x_ref[...] + _dot(o_ref[...].astype(BF16), w_ref[...])
    y_ref[...] = _rms_scale(y, fw_ref[...]) if final_norm else y


def _out_proj(o, x, w, final_w, tm, final_norm):
    t, k = o.shape
    d = x.shape[1]
    assert t % tm == 0
    return pl.pallas_call(
        functools.partial(_out_proj_kernel, final_norm=final_norm),
        grid=(t // tm,),
        in_specs=[pl.BlockSpec((tm, k), lambda i: (i, 0)),
                  pl.BlockSpec((tm, d), lambda i: (i, 0)),
                  pl.BlockSpec((k, d), lambda i: (0, 0)),
                  pl.BlockSpec((1, d), lambda i: (0, 0))],
        out_specs=pl.BlockSpec((tm, d), lambda i: (i, 0)),
        out_shape=jax.ShapeDtypeStruct((t, d), F32),
        compiler_params=_params("parallel"), name="out_proj",
    )(o, x, w, final_w.reshape(1, d))


def _unit_lower_inverses(lows, eye, cl):
    powers = [(-low).astype(BF16) for low in lows]
    invs = [eye - low for low in lows]
    p = 1
    while 2 * p < cl:
        sq = [_dot(pw, pw) for pw in powers]
        p *= 2
        powers = [s.astype(BF16) for s in sq]
        invs = [inv + _dot(inv.astype(BF16), pw) for inv, pw in zip(invs, powers)]
    return invs


def _gdn_step(p_rows, put_out, s0_ref, c0_ref, cw_ref, alog_ref, dtb_ref, onw_ref,
              s_ref, c_ref, s_scr, xe_scr, *, heads, dk, dv, nb, cl, nc):
    c = nb * cl
    seq_rows = cl * nc
    qk_w = heads * dk
    conv_ch = 2 * qk_w + heads * dv
    z_off = conv_ch
    b_off = conv_ch + heads * dv
    a_off = b_off + LANES
    t = pl.program_id(1)

    @pl.when(t == 0)
    def _():
        s_scr[...] = jnp.zeros(s_scr.shape, F32) if s0_ref is None else s0_ref[...]
        for j in range(nb):
            xe_scr[j, 0:SUBLANES, :] = jnp.zeros((SUBLANES, conv_ch), F32)
            if c0_ref is not None:
                xe_scr[j, HIST0:SUBLANES, :] = c0_ref[0, j]

    for j in range(nb):
        xe_scr[j, SUBLANES:SUBLANES + seq_rows, :] = p_rows(j * seq_rows, (j + 1) * seq_rows,
                                                            0, conv_ch)

    row = lax.broadcasted_iota(jnp.int32, (c, c), 0)
    col = lax.broadcasted_iota(jnp.int32, (c, c), 1)
    if nb > 1:
        shift = cl.bit_length() - 1
        same = lax.shift_right_logical(row, shift) == lax.shift_right_logical(col, shift)
        incl = same & (row >= col)
        strict = same & (row > col)
    else:
        same = row >= 0
        incl = row >= col
        strict = row > col
    eye = (row == col).astype(F32)
    cum_masks = jnp.concatenate([incl.astype(BF16), same.astype(BF16)], axis=0)
    sel = (lax.broadcasted_iota(jnp.int32, (SUBLANES, LANES), 0)
           == lax.broadcasted_iota(jnp.int32, (SUBLANES, LANES), 1)).astype(BF16)
    neg_a = -jnp.exp(alog_ref[...])

    def chunk_rows(ci, lo, hi):
        parts = [p_rows(j * seq_rows + ci * cl, j * seq_rows + (ci + 1) * cl, lo, hi)
                 for j in range(nb)]
        return parts[0] if nb == 1 else jnp.concatenate(parts, axis=0)

    units = [(ci, h) for ci in range(nc) for h in range(heads)]
    seqs = range(nb)

    def stack(parts):
        return parts[0] if nb == 1 else jnp.concatenate(parts, axis=0)

    conv = []
    for j in seqs:
        xall = xe_scr[j]
        acc = xall * cw_ref[0:1, :]
        for tap in range(1, CONV_W):
            acc = pltpu.roll(acc, 1, axis=0) + xall * cw_ref[tap:tap + 1, :]
        conv.append(acc[SUBLANES:])

    mixed, beta, gcum, gtot, gcum_t = [], [], [], [], []
    for ci in range(nc):
        mixed.append(_silu(stack([conv[j][ci * cl:(ci + 1) * cl] for j in seqs])))
        beta.append(_sigmoid(chunk_rows(ci, b_off, b_off + LANES)))
        g = neg_a * _softplus(chunk_rows(ci, a_off, a_off + LANES) + dtb_ref[...])
        g3 = _split3(g)
        gg = _dot(cum_masks, g3[0]) + (_dot(cum_masks, g3[1]) + _dot(cum_masks, g3[2]))
        gcum.append(gg[:c])
        gtot.append(gg[c:])
        gc3 = _split3(gg[:c])
        gcum_t.append(_dot_nt(sel, gc3[0]) + (_dot_nt(sel, gc3[1]) + _dot_nt(sel, gc3[2])))

    q_l, eg_l, lows, rhs_l, qk_l, kd_l = [], [], [], [], [], []
    for ci, h in units:
        q = mixed[ci][:, h * dk:(h + 1) * dk]
        k = mixed[ci][:, qk_w + h * dk:qk_w + (h + 1) * dk]
        v = mixed[ci][:, 2 * qk_w + h * dv:2 * qk_w + (h + 1) * dv]
        q = q * lax.rsqrt(jnp.sum(q * q, axis=-1, keepdims=True) + EPS) * (dk ** -0.5)
        k = k * lax.rsqrt(jnp.sum(k * k, axis=-1, keepdims=True) + EPS)
        gc = gcum[ci][:, h:h + 1]
        gr = gcum_t[ci][h:h + 1, :]
        bh = beta[ci][:, h:h + 1]
        dec = jnp.where(incl, jnp.exp(jnp.where(incl, gc - gr, 0.0)), 0.0)
        eg = jnp.exp(gc)
        qb = q.astype(BF16)
        kb = k.astype(BF16)
        lows.append(jnp.where(strict, bh * _dot_nt(kb, kb) * dec, 0.0))
        qk_l.append((_dot_nt(qb, kb) * dec).astype(BF16))
        rhs_l.append(jnp.concatenate([bh * v, (bh * eg) * k], axis=-1).astype(BF16))
        kd_l.append(k * jnp.exp(gtot[ci][:, h:h + 1] - gc))
        q_l.append(q)
        eg_l.append(eg)
    invs = _unit_lower_inverses(lows, eye, cl)
    sols = [_dot(inv.astype(BF16), rhs) for inv, rhs in zip(invs, rhs_l)]

    for ci in range(nc):
        idx = [ci * heads + h for h in range(heads)]
        s_old = [[s_scr[j, h] for j in seqs] for h in range(heads)]
        res = [[_dot(jnp.concatenate([sols[i][j * cl:(j + 1) * cl, dv:],
                                      q_l[i][j * cl:(j + 1) * cl]], axis=0).astype(BF16),
                     s_old[h][j].astype(BF16)) for j in seqs]
               for h, i in enumerate(idx)]
        u = [stack([sols[i][j * cl:(j + 1) * cl, :dv] - res[h][j][:cl] for j in seqs])
             for h, i in enumerate(idx)]
        o = [eg_l[i] * stack([res[h][j][cl:] for j in seqs]) + _dot(qk_l[i], u[h].astype(BF16))
             for h, i in enumerate(idx)]
        for h, i in enumerate(idx):
            for j in seqs:
                r0, r1 = j * cl, (j + 1) * cl
                s_scr[j, h] = (jnp.exp(gtot[ci][r0:r0 + 1, h:h + 1]) * s_old[h][j]
                               + _dot_tn(kd_l[i][r0:r1].astype(BF16), u[h][r0:r1].astype(BF16)))
        for h in range(heads):
            out = _rms_scale(o[h], onw_ref[...]) * _silu(
                chunk_rows(ci, z_off + h * dv, z_off + (h + 1) * dv))
            for j in seqs:
                put_out(j * seq_rows + ci * cl, j * seq_rows + (ci + 1) * cl, h,
                        out[j * cl:(j + 1) * cl])

    new_hist = [xe_scr[j, seq_rows + HIST0:seq_rows + SUBLANES, :] for j in range(nb)]
    for j in range(nb):
        xe_scr[j, HIST0:SUBLANES, :] = new_hist[j]

    @pl.when(t == pl.num_programs(1) - 1)
    def _():
        s_ref[...] = s_scr[...]
        for j in range(nb):
            c_ref[0, j] = new_hist[j]


def _gdn_core_kernel(p_ref, ba_ref, s0_ref, c0_ref, cw_ref, alog_ref, dtb_ref, onw_ref,
                     o_ref, s_ref, c_ref, s_scr, xe_scr, *, heads, dk, dv, nb, cl, nc):
    main = p_ref.shape[2]

    def p_rows(r0, r1, lo, hi):
        if lo >= main:
            return ba_ref[0, r0:r1, lo - main:hi - main]
        return p_ref[0, r0:r1, lo:hi]

    def put_out(r0, r1, h, value):
        o_ref[0, r0:r1, h * dv:(h + 1) * dv] = value

    _gdn_step(p_rows, put_out, s0_ref, c0_ref, cw_ref, alog_ref, dtb_ref, onw_ref,
              s_ref, c_ref, s_scr, xe_scr, heads=heads, dk=dk, dv=dv, nb=nb, cl=cl, nc=nc)


def _gdn_core(p, ba, s0, c0, conv_w, a_log, dt_bias, onorm_w):
    b, l, pw = p.shape
    heads, dk, dv = s0.shape[1:]
    conv_ch = c0.shape[3]
    assert heads <= SUBLANES and dk == LANES and dv == LANES
    assert pw == conv_ch + heads * dv and ba.shape[2] == 2 * LANES
    assert l <= CHUNK and l % SUBLANES == 0 and l & (l - 1) == 0
    nb = max(n for n in range(1, CHUNK // l + 1) if b % n == 0)
    groups, rows = b // nb, nb * l
    kern = functools.partial(_gdn_core_kernel, heads=heads, dk=dk, dv=dv, nb=nb, cl=l, nc=1)
    o, s, cs = pl.pallas_call(
        kern, grid=(groups, 1),
        in_specs=[pl.BlockSpec((1, rows, pw), lambda i, t: (i, 0, 0)),
                  pl.BlockSpec((1, rows, 2 * LANES), lambda i, t: (i, 0, 0)),
                  pl.BlockSpec((nb, heads, dk, dv), lambda i, t: (i, 0, 0, 0)),
                  pl.BlockSpec((1, nb, CONV_W - 1, conv_ch), lambda i, t: (0, i, 0, 0)),
                  pl.BlockSpec((CONV_W, conv_ch), lambda i, t: (0, 0)),
                  pl.BlockSpec((1, LANES), lambda i, t: (0, 0)),
                  pl.BlockSpec((1, LANES), lambda i, t: (0, 0)),
                  pl.BlockSpec((1, dv), lambda i, t: (0, 0))],
        out_specs=[pl.BlockSpec((1, rows, heads * dv), lambda i, t: (i, 0, 0)),
                   pl.BlockSpec((nb, heads, dk, dv), lambda i, t: (i, 0, 0, 0)),
                   pl.BlockSpec((1, nb, CONV_W - 1, conv_ch), lambda i, t: (0, i, 0, 0))],
        out_shape=[jax.ShapeDtypeStruct((groups, rows, heads * dv), F32),
                   jax.ShapeDtypeStruct(s0.shape, F32),
                   jax.ShapeDtypeStruct(c0.shape, F32)],
        scratch_shapes=[pltpu.VMEM((nb, heads, dk, dv), F32),
                        pltpu.VMEM((nb, SUBLANES + l, conv_ch), F32)],
        compiler_params=_params("parallel", "arbitrary"), name="gdn_core",
    )(p.reshape(groups, rows, pw), ba.reshape(groups, rows, 2 * LANES), s0, c0, conv_w,
      _lane_pad(a_log), _lane_pad(dt_bias), onorm_w.reshape(1, dv))
    return o.reshape(b, l, heads * dv), s, cs


def _gdn_layer_kernel(x_ref, nw_ref, win_ref, wba_ref, cw_ref, alog_ref, dtb_ref, onw_ref,
                      wout_ref, y_ref, s_ref, c_ref, s_scr, xe_scr, o_scr,
                      *, heads, dk, dv, cl, nc):
    x = x_ref[0]
    xn = _rms_scale(x, nw_ref[...]).astype(BF16)
    p = _dot(xn, win_ref[...])
    ba = _dot(xn, wba_ref[...])
    main = p.shape[1]

    def p_rows(r0, r1, lo, hi):
        if lo >= main:
            return ba[r0:r1, lo - main:hi - main]
        return p[r0:r1, lo:hi]

    def put_out(r0, r1, h, value):
        o_scr[r0:r1, h * dv:(h + 1) * dv] = value.astype(BF16)

    _gdn_step(p_rows, put_out, None, None, cw_ref, alog_ref, dtb_ref, onw_ref,
              s_ref, c_ref, s_scr, xe_scr, heads=heads, dk=dk, dv=dv, nb=1, cl=cl, nc=nc)
    y_ref[0] = x + _dot(o_scr[...], wout_ref[...])


def _gdn_layer(x, norm_w, w_in, w_ba, conv_w, a_log, dt_bias, onorm_w, w_out, heads):
    b, l, d = x.shape
    dv = w_out.shape[0] // heads
    dk = dv
    conv_ch = conv_w.shape[1]
    pw = w_in.shape[1]
    cl, nc = CHUNK, GDN_CHUNKS_PER_STEP
    rows = cl * nc
    assert l % rows == 0 and heads <= SUBLANES and dk == LANES and dv == LANES
    assert conv_ch == 3 * heads * dk and pw == conv_ch + heads * dv
    assert w_ba.shape[1] == 2 * LANES
    resident = dict(pipeline_mode=pl.Buffered(1))
    kern = functools.partial(_gdn_layer_kernel, heads=heads, dk=dk, dv=dv, cl=cl, nc=nc)
    return pl.pallas_call(
        kern, grid=(b, l // rows),
        in_specs=[pl.BlockSpec((1, rows, d), lambda i, t: (i, t, 0)),
                  pl.BlockSpec((1, d), lambda i, t: (0, 0)),
                  pl.BlockSpec((d, pw), lambda i, t: (0, 0), **resident),
                  pl.BlockSpec((d, 2 * LANES), lambda i, t: (0, 0), **resident),
                  pl.BlockSpec((CONV_W, conv_ch), lambda i, t: (0, 0)),
                  pl.BlockSpec((1, LANES), lambda i, t: (0, 0)),
                  pl.BlockSpec((1, LANES), lambda i, t: (0, 0)),
                  pl.BlockSpec((1, dv), lambda i, t: (0, 0)),
                  pl.BlockSpec((heads * dv, d), lambda i, t: (0, 0), **resident)],
        out_specs=[pl.BlockSpec((1, rows, d), lambda i, t: (i, t, 0)),
                   pl.BlockSpec((1, heads, dk, dv), lambda i, t: (i, 0, 0, 0)),
                   pl.BlockSpec((1, 1, CONV_W - 1, conv_ch), lambda i, t: (0, i, 0, 0))],
        out_shape=[jax.ShapeDtypeStruct((b, l, d), F32),
                   jax.ShapeDtypeStruct((b, heads, dk, dv), F32),
                   jax.ShapeDtypeStruct((1, b, CONV_W - 1, conv_ch), F32)],
        scratch_shapes=[pltpu.VMEM((1, heads, dk, dv), F32),
                        pltpu.VMEM((1, SUBLANES + rows, conv_ch), F32),
                        pltpu.VMEM((rows, heads * dv), BF16)],
        compiler_params=_params("parallel", "arbitrary"), name="gdn_layer",
    )(x, norm_w.reshape(1, d), w_in, w_ba, conv_w, _lane_pad(a_log), _lane_pad(dt_bias),
      onorm_w.reshape(1, dv), w_out)


def _ret_chunk(cols, put_out, cos_ref, sin_ref, s0_ref, onw_ref, s_ref,
               *, heads, dk, dv, nb, c, first_step):
    qk_w = heads * dk
    g_off = 2 * qk_w + heads * dv
    half = dk // 2

    if first_step is not None:
        @pl.when(first_step)
        def _():
            s_ref[...] = jnp.zeros(s_ref.shape, F32) if s0_ref is None else s0_ref[...]
    state_ref = s0_ref if first_step is None else s_ref

    cos = cos_ref[...]
    sin = sin_ref[...]
    row = lax.broadcasted_iota(jnp.int32, (c, c), 0)
    col = lax.broadcasted_iota(jnp.int32, (c, c), 1)
    diff = (row - col).astype(F32)
    idx = lax.broadcasted_iota(jnp.int32, (c, 1), 0).astype(F32)

    def rotary(x):
        x1, x2 = x[:, :half], x[:, half:]
        return jnp.concatenate([x1 * cos - x2 * sin, x1 * sin + x2 * cos], axis=-1)

    decay, q_dec, k_dec, s_dec = [], [], [], []
    for h in range(heads):
        lg = jnp.log(jnp.full((1, 1), 1.0 - 2.0 ** (-5.0 - h), F32))
        decay.append(jnp.where(diff >= 0, jnp.exp(lg * jnp.maximum(diff, 0.0)), 0.0))
        q_dec.append(jnp.exp(lg * (idx + 1.0)))
        k_dec.append(jnp.exp(lg * (c - 1.0 - idx)))
        s_dec.append(jnp.exp(lg * c))

    units = [(j, h) for j in range(nb) for h in range(heads)]
    qb, kb, kdb, vb = [], [], [], []
    for j, h in units:
        q = rotary(cols(j, h * dk, (h + 1) * dk))
        k = rotary(cols(j, qk_w + h * dk, qk_w + (h + 1) * dk)) * (dk ** -0.5)
        qb.append(q.astype(BF16))
        kb.append(k.astype(BF16))
        kdb.append((k * k_dec[h]).astype(BF16))
        vb.append(cols(j, 2 * qk_w + h * dv, 2 * qk_w + (h + 1) * dv).astype(BF16))
    qk = [(_dot_nt(qb[i], kb[i]) * decay[h]).astype(BF16) for i, (j, h) in enumerate(units)]
    s_old = [state_ref[j, h] for j, h in units]
    qs = [_dot(qb[i], s_old[i].astype(BF16)) for i in range(len(units))]
    o = [_dot(qk[i], vb[i]) + q_dec[h] * qs[i] for i, (j, h) in enumerate(units)]
    for i, (j, h) in enumerate(units):
        s_ref[j, h] = s_dec[h] * s_old[i] + _dot_tn(kdb[i], vb[i])
    for i, (j, h) in enumerate(units):
        put_out(j, h, _rms_scale(o[i], onw_ref[h:h + 1, :])
                * _silu(cols(j, g_off + h * dv, g_off + (h + 1) * dv)))


def _ret_side_chunk(ps_ref, cos_ref, sin_ref, s0_ref, onw_ref, o_ref, s_ref, *, heads, dk, dv,
                    nb):
    c = ps_ref.shape[1] // nb

    def cols(j, lo, hi):
        return ps_ref[0, j * c:(j + 1) * c, lo:hi]

    def put_out(j, h, value):
        o_ref[0, j * c:(j + 1) * c, h * dv:(h + 1) * dv] = value

    _ret_chunk(cols, put_out, cos_ref, sin_ref, s0_ref, onw_ref, s_ref,
               heads=heads, dk=dk, dv=dv, nb=nb, c=c, first_step=None)


def _ret_core_kernel(ps_ref, cos_ref, sin_ref, s0_ref, onw_ref, o_ref, s_ref,
                     *, heads, dk, dv, nb):
    _ret_side_chunk(ps_ref, cos_ref, sin_ref, s0_ref, onw_ref, o_ref, s_ref,
                    heads=heads, dk=dk, dv=dv, nb=nb)


def _ret_side_specs(p_s, s0_s, nbs, group):
    bs, ls, pw = p_s.shape
    heads, dk, dv = s0_s.shape[1:]
    assert bs % nbs == 0 and ls <= CHUNK and ls % SUBLANES == 0 and dk // 2 == LANES
    assert pw == 2 * heads * dk + 2 * heads * dv
    in_specs = [pl.BlockSpec((1, nbs * ls, pw), lambda *g: (group(*g), 0, 0)),
                pl.BlockSpec((ls, dk // 2), lambda *g: (0, 0)),
                pl.BlockSpec((ls, dk // 2), lambda *g: (0, 0)),
                pl.BlockSpec((nbs, heads, dk, dv), lambda *g: (group(*g), 0, 0, 0))]
    out_specs = [pl.BlockSpec((1, nbs * ls, heads * dv), lambda *g: (group(*g), 0, 0)),
                 pl.BlockSpec((nbs, heads, dk, dv), lambda *g: (group(*g), 0, 0, 0))]
    out_shape = [jax.ShapeDtypeStruct((bs // nbs, nbs * ls, heads * dv), F32),
                 jax.ShapeDtypeStruct(s0_s.shape, F32)]
    return in_specs, out_specs, out_shape


def _ret_core(p_s, cos_s, sin_s, s0_s, onorm_w):
    bs, ls, pw = p_s.shape
    heads, dk, dv = s0_s.shape[1:]
    nbs = RET_SEQS_PER_STEP if bs % RET_SEQS_PER_STEP == 0 else 1
    in_specs, out_specs, out_shape = _ret_side_specs(p_s, s0_s, nbs, lambda i: i)
    in_specs.append(pl.BlockSpec((heads, dv), lambda i: (0, 0)))
    o, s = pl.pallas_call(
        functools.partial(_ret_core_kernel, heads=heads, dk=dk, dv=dv, nb=nbs),
        grid=(bs // nbs,), in_specs=in_specs, out_specs=out_specs, out_shape=out_shape,
        compiler_params=_params("parallel"), name="ret_core",
    )(p_s.reshape(bs // nbs, nbs * ls, pw), cos_s, sin_s, s0_s, onorm_w)
    return o.reshape(bs, ls, heads * dv), s


def _ret_layer_kernel(*refs, heads, dk, dv, c, side_seqs):
    x_ref, nw_ref, win_ref, cos_ref, sin_ref, onw_ref, wout_ref, fw_ref = refs[:8]
    if side_seqs:
        ps_ref, cos_s_ref, sin_s_ref, s0s_ref, y_ref, s_ref, os_ref, ss_ref, o_scr = refs[8:]
    else:
        y_ref, s_ref, o_scr = refs[8:]

    x = x_ref[0]
    p = _dot(_rms_scale(x, nw_ref[...]).astype(BF16), win_ref[...])

    def cols(j, lo, hi):
        return p[:, lo:hi]

    def put_out(j, h, value):
        o_scr[:, h * dv:(h + 1) * dv] = value.astype(BF16)

    _ret_chunk(cols, put_out, cos_ref, sin_ref, None, onw_ref, s_ref,
               heads=heads, dk=dk, dv=dv, nb=1, c=c, first_step=pl.program_id(1) == 0)
    if side_seqs:
        _ret_side_chunk(ps_ref, cos_s_ref, sin_s_ref, s0s_ref, onw_ref, os_ref, ss_ref,
                        heads=heads, dk=dk, dv=dv, nb=side_seqs)
    y_ref[0] = _rms_scale(x + _dot(o_scr[...], wout_ref[...]), fw_ref[...])


def _ret_layer(x, norm_w, w_in, cos, sin, onorm_w, w_out, final_w, heads, side=None):
    b, l, d = x.shape
    pw = w_in.shape[1]
    dv = w_out.shape[0] // heads
    dk = (pw - 2 * heads * dv) // (2 * heads)
    c = RET_CHUNK
    steps = l // c
    assert l % c == 0 and dk // 2 == LANES
    resident = dict(pipeline_mode=pl.Buffered(1))
    in_specs = [pl.BlockSpec((1, c, d), lambda i, t: (i, t, 0)),
                pl.BlockSpec((1, d), lambda i, t: (0, 0)),
                pl.BlockSpec((d, pw), lambda i, t: (0, 0), **resident),
                pl.BlockSpec((c, dk // 2), lambda i, t: (t, 0)),
                pl.BlockSpec((c, dk // 2), lambda i, t: (t, 0)),
                pl.BlockSpec((heads, dv), lambda i, t: (0, 0)),
                pl.BlockSpec((heads * dv, d), lambda i, t: (0, 0), **resident),
                pl.BlockSpec((1, d), lambda i, t: (0, 0))]
    out_specs = [pl.BlockSpec((1, c, d), lambda i, t: (i, t, 0)),
                 pl.BlockSpec((1, heads, dk, dv), lambda i, t: (i, 0, 0, 0))]
    out_shape = [jax.ShapeDtypeStruct((b, l, d), F32),
                 jax.ShapeDtypeStruct((b, heads, dk, dv), F32)]
    args = [x, norm_w.reshape(1, d), w_in, cos, sin, onorm_w, w_out, final_w.reshape(1, d)]
    nbs = 0
    if side is not None:
        p_s, cos_s, sin_s, s0_s = side
        bs, ls, _ = p_s.shape
        nbs = RET_SEQS_PER_STEP
        assert bs == nbs * b * steps and p_s.shape[2] == pw
        assert s0_s.shape == (bs, heads, dk, dv)
        side_in, side_out, side_shape = _ret_side_specs(p_s, s0_s, nbs,
                                                        lambda i, t: i * steps + t)
        in_specs += side_in
        out_specs += side_out
        out_shape += side_shape
        args += [p_s.reshape(bs // nbs, nbs * ls, pw), cos_s, sin_s, s0_s]
    kern = functools.partial(_ret_layer_kernel, heads=heads, dk=dk, dv=dv, c=c, side_seqs=nbs)
    outs = pl.pallas_call(
        kern, grid=(b, steps), in_specs=in_specs, out_specs=out_specs, out_shape=out_shape,
        scratch_shapes=[pltpu.VMEM((c, heads * dv), BF16)],
        compiler_params=_params("parallel", "arbitrary"), name="ret_layer",
    )(*args)
    if side is None:
        return outs
    y, s, o_s, s_s = outs
    return y, s, o_s.reshape(bs, ls, heads * dv), s_s


def _row_tile(t):
    for tm in (256, 128, 64, 32, 16, 8):
        if t % tm == 0:
            return tm
    raise ValueError(f"token count {t} is not a multiple of {SUBLANES}")


def _rope_tables(l, half, pos0):
    inv = 1.0 / (ROPE_BASE ** jnp.linspace(0.0, 1.0, half, dtype=F32))
    pos = pos0 + jnp.arange(l, dtype=F32)
    ang = pos[:, None] * inv[None, :]
    return jnp.cos(ang), jnp.sin(ang)


def _beta_decay_weight(w, heads, main):
    zeros = jnp.zeros((w.shape[0], LANES - heads), w.dtype)
    return jnp.concatenate([w[:, main:main + heads], zeros, w[:, main + heads:], zeros],
                           axis=1).astype(BF16)


def kernel(x_prompt, x_sample, state_gdn_ssm, state_gdn_conv, state_ret, norm_w, w_in_a,
           conv_w_a, a_log_a, dt_bias_a, onorm_a, w_out_a, w_in_b, onorm_b, w_out_b,
           final_norm_w):
    assert state_gdn_ssm.shape[0] == 1 and state_ret.shape[0] == 1 and norm_w.shape[0] == 2
    bp, lp, d = x_prompt.shape
    bs, ls, _ = x_sample.shape
    heads_a, dk_a, dv_a = state_gdn_ssm.shape[2:]
    heads_b, dk_b, dv_b = state_ret.shape[2:]
    main_a = 2 * heads_a * dk_a + 2 * heads_a * dv_a
    win_a = _cast_bf16(w_in_a, main_a)
    wba_a = _beta_decay_weight(w_in_a[0], heads_a, main_a)
    wout_a = _cast_bf16(w_out_a)
    win_b = _cast_bf16(w_in_b)
    wout_b = _cast_bf16(w_out_b)

    ts = bs * ls
    tm = _row_tile(ts)
    xs = x_sample.reshape(ts, d)
    p, ba = _in_proj(xs, norm_w[0], [win_a, wba_a], tm)
    o, sa_s, ca_s = _gdn_core(p.reshape(bs, ls, -1), ba.reshape(bs, ls, -1), state_gdn_ssm[0],
                              state_gdn_conv, conv_w_a[0], a_log_a[0], dt_bias_a[0], onorm_a[0])
    xs = _out_proj(o.reshape(ts, -1), xs, wout_a, final_norm_w, tm, False)
    (p_s,) = _in_proj(xs, norm_w[1], [win_b], tm)
    p_s = p_s.reshape(bs, ls, -1)
    cos_s, sin_s = _rope_tables(ls, dk_b // 2, PAST_LEN)

    x1, sa_p, ca_p = _gdn_layer(x_prompt, norm_w[0], win_a, wba_a, conv_w_a[0], a_log_a[0],
                                dt_bias_a[0], onorm_a[0], wout_a, heads_a)
    cos_p, sin_p = _rope_tables(lp, dk_b // 2, 0.0)
    if bs == RET_SEQS_PER_STEP * bp * (lp // RET_CHUNK):
        y_p, sb_p, o, sb_s = _ret_layer(x1, norm_w[1], win_b, cos_p, sin_p, onorm_b[0], wout_b,
                                        final_norm_w, heads_b,
                                        side=(p_s, cos_s, sin_s, state_ret[0]))
    else:
        y_p, sb_p = _ret_layer(x1, norm_w[1], win_b, cos_p, sin_p, onorm_b[0], wout_b,
                               final_norm_w, heads_b)
        o, sb_s = _ret_core(p_s, cos_s, sin_s, state_ret[0], onorm_b[0])
    y_s = _out_proj(o.reshape(ts, -1), xs, wout_b, final_norm_w, tm, True).reshape(bs, ls, d)
    return (y_p, y_s, sa_p[None], ca_p, sb_p[None], sa_s[None], ca_s, sb_s[None])
```

```python
import functools

import jax
import jax.numpy as jnp
from jax import lax
from jax.experimental import pallas as pl
from jax.experimental.pallas import tpu as pltpu

F32 = jnp.float32
BF16 = jnp.bfloat16
EPS = 1e-6
CHUNK = 64
CONV_W = 4
ROPE_BASE = 10000.0
PAST_LEN = 16384.0
LANES = 128
SUBLANES = 8
HIST0 = SUBLANES - (CONV_W - 1)
VMEM_LIMIT_BYTES = 60 * 1024 * 1024
GDN_CHUNKS_PER_STEP = 4
ROW_STRIDE = 4
RET_CHUNK = 256
RET_SEQS_PER_STEP = 2


def _dot(a, b):
    return jnp.dot(a, b, preferred_element_type=F32)


def _dot_nt(a, b):
    return lax.dot_general(a, b, (((1,), (1,)), ((), ())), preferred_element_type=F32)


def _dot_tn(a, b):
    return lax.dot_general(a, b, (((0,), (0,)), ((), ())), preferred_element_type=F32)


def _split3(x):
    hi = x.astype(BF16)
    r = x - hi.astype(F32)
    mid = r.astype(BF16)
    lo = (r - mid.astype(F32)).astype(BF16)
    return hi, mid, lo


def _sigmoid(x):
    return 1.0 / (1.0 + jnp.exp(-x))


def _silu(x):
    return x * _sigmoid(x)


def _softplus(x):
    return jnp.maximum(x, 0.0) + jnp.log1p(jnp.exp(-jnp.abs(x)))


def _rms_scale(x, w):
    return x * lax.rsqrt(jnp.mean(x * x, axis=-1, keepdims=True) + EPS) * w


def _params(*sem):
    return pltpu.CompilerParams(dimension_semantics=sem, vmem_limit_bytes=VMEM_LIMIT_BYTES)


def _lane_pad(vec):
    return jnp.pad(vec.astype(F32), (0, LANES - vec.shape[0])).reshape(1, LANES)


def _in_proj_kernel(x_ref, nw_ref, *refs):
    n = len(refs) // 2
    xn = _rms_scale(x_ref[...], nw_ref[...]).astype(BF16)
    for w_ref, o_ref in zip(refs[:n], refs[n:]):
        o_ref[...] = _dot(xn, w_ref[...])


def _in_proj(x, norm_w, weights, tm):
    t, d = x.shape
    assert t % tm == 0
    in_specs = [pl.BlockSpec((tm, d), lambda i: (i, 0)),
                pl.BlockSpec((1, d), lambda i: (0, 0))]
    in_specs += [pl.BlockSpec(w.shape, lambda i: (0, 0)) for w in weights]
    return pl.pallas_call(
        _in_proj_kernel, grid=(t // tm,), in_specs=in_specs,
        out_specs=[pl.BlockSpec((tm, w.shape[1]), lambda i: (i, 0)) for w in weights],
        out_shape=[jax.ShapeDtypeStruct((t, w.shape[1]), F32) for w in weights],
        compiler_params=_params("parallel"), name="in_proj",
    )(x, norm_w.reshape(1, d), *weights)


def _out_proj_kernel(o_ref, x_ref, w_ref, fw_ref, y_ref, *, final_norm):
    y = x_ref[...] + _dot(o_ref[...].astype(BF16), w_ref[...])
    y_ref[...] = _rms_scale(y, fw_ref[...]) if final_norm else y


def _out_proj(o, x, w, final_w, tm, final_norm):
    t, k = o.shape
    d = x.shape[1]
    assert t % tm == 0
    return pl.pallas_call(
        functools.partial(_out_proj_kernel, final_norm=final_norm),
        grid=(t // tm,),
        in_specs=[pl.BlockSpec((tm, k), lambda i: (i, 0)),
                  pl.BlockSpec((tm, d), lambda i: (i, 0)),
                  pl.BlockSpec((k, d), lambda i: (0, 0)),
                  pl.BlockSpec((1, d), lambda i: (0, 0))],
        out_specs=pl.BlockSpec((tm, d), lambda i: (i, 0)),
        out_shape=jax.ShapeDtypeStruct((t, d), F32),
        compiler_params=_params("parallel"), name="out_proj",
    )(o, x, w, final_w.reshape(1, d))


def _unit_lower_inverses(lows, eye, cl):
    powers = [(-low).astype(BF16) for low in lows]
    invs = [eye - low for low in lows]
    p = 1
    while 2 * p < cl:
        sq = [_dot(pw, pw) for pw in powers]
        p *= 2
        powers = [s.astype(BF16) for s in sq]
        invs = [inv + _dot(inv.astype(BF16), pw) for inv, pw in zip(invs, powers)]
    return invs


def _gdn_step(front, tok, alog_ref, dtb_ref, onw_ref, s_scr, *, heads, dk, dv, nb, cl, nc):
    c = nb * cl
    qk_w = heads * dk
    row = lax.broadcasted_iota(jnp.int32, (c, c), 0)
    col = lax.broadcasted_iota(jnp.int32, (c, c), 1)
    shift = cl.bit_length() - 1
    same = lax.shift_right_logical(row, shift) == lax.shift_right_logical(col, shift)
    incl = same & (tok(row) >= tok(col))
    strict = same & (tok(row) > tok(col))
    eye = (row == col).astype(F32)
    cum_masks = jnp.concatenate([incl.astype(BF16), same.astype(BF16)], axis=0)
    sel = (lax.broadcasted_iota(jnp.int32, (SUBLANES, LANES), 0)
           == lax.broadcasted_iota(jnp.int32, (SUBLANES, LANES), 1)).astype(BF16)
    neg_a = -jnp.exp(alog_ref[...])
    units = [(ci, h) for ci in range(nc) for h in range(heads)]
    seqs = range(nb)

    def stack(parts):
        return parts[0] if nb == 1 else jnp.concatenate(parts, axis=0)

    beta, gcum, gtot, gcum_t = [], [], [], []
    for ci in range(nc):
        beta.append(_sigmoid(front.gate(ci, 0)))
        g = neg_a * _softplus(front.gate(ci, 1) + dtb_ref[...])
        g3 = _split3(g)
        gg = _dot(cum_masks, g3[0]) + (_dot(cum_masks, g3[1]) + _dot(cum_masks, g3[2]))
        gcum.append(gg[:c])
        gtot.append(gg[c:])
        gc3 = _split3(gg[:c])
        gcum_t.append(_dot_nt(sel, gc3[0]) + (_dot_nt(sel, gc3[1]) + _dot_nt(sel, gc3[2])))

    q_l, eg_l, lows, rhs_l, qk_l, kd_l = [], [], [], [], [], []
    for ci, h in units:
        q = front.mixed(ci, h * dk, (h + 1) * dk)
        k = front.mixed(ci, qk_w + h * dk, qk_w + (h + 1) * dk)
        v = front.mixed(ci, 2 * qk_w + h * dv, 2 * qk_w + (h + 1) * dv)
        q = q * lax.rsqrt(jnp.sum(q * q, axis=-1, keepdims=True) + EPS) * (dk ** -0.5)
        k = k * lax.rsqrt(jnp.sum(k * k, axis=-1, keepdims=True) + EPS)
        gc = gcum[ci][:, h:h + 1]
        gr = gcum_t[ci][h:h + 1, :]
        bh = beta[ci][:, h:h + 1]
        dec = jnp.where(incl, jnp.exp(jnp.where(incl, gc - gr, 0.0)), 0.0)
        eg = jnp.exp(gc)
        qb = q.astype(BF16)
        kb = k.astype(BF16)
        lows.append(jnp.where(strict, bh * _dot_nt(kb, kb) * dec, 0.0))
        qk_l.append((_dot_nt(qb, kb) * dec).astype(BF16))
        rhs_l.append(jnp.concatenate([bh * v, (bh * eg) * k], axis=-1).astype(BF16))
        kd_l.append(k * jnp.exp(gtot[ci][:, h:h + 1] - gc))
        q_l.append(q)
        eg_l.append(eg)
    invs = _unit_lower_inverses(lows, eye, cl)
    sols = [_dot(inv.astype(BF16), rhs) for inv, rhs in zip(invs, rhs_l)]

    for ci in range(nc):
        idx = [ci * heads + h for h in range(heads)]
        s_old = [[s_scr[j, h] for j in seqs] for h in range(heads)]
        res = [[_dot(jnp.concatenate([sols[i][j * cl:(j + 1) * cl, dv:],
                                      q_l[i][j * cl:(j + 1) * cl]], axis=0).astype(BF16),
                     s_old[h][j].astype(BF16)) for j in seqs]
               for h, i in enumerate(idx)]
        u = [stack([sols[i][j * cl:(j + 1) * cl, :dv] - res[h][j][:cl] for j in seqs])
             for h, i in enumerate(idx)]
        o = [eg_l[i] * stack([res[h][j][cl:] for j in seqs]) + _dot(qk_l[i], u[h].astype(BF16))
             for h, i in enumerate(idx)]
        for h, i in enumerate(idx):
            for j in seqs:
                r0, r1 = j * cl, (j + 1) * cl
                s_scr[j, h] = (jnp.exp(gtot[ci][r0:r0 + 1, h:h + 1]) * s_old[h][j]
                               + _dot_tn(kd_l[i][r0:r1].astype(BF16), u[h][r0:r1].astype(BF16)))
        for h in range(heads):
            front.put(ci, h, _rms_scale(o[h], onw_ref[...]) * _silu(front.z(ci, h)))


class _RollFront:
    def __init__(self, p_ref, ba_ref, o_ref, xe_scr, cw_ref, *, dv, nb):
        self.p_ref, self.ba_ref, self.o_ref, self.dv = p_ref, ba_ref, o_ref, dv
        self.conv_ch = xe_scr.shape[2]
        conv = []
        for j in range(nb):
            xall = xe_scr[j]
            acc = xall * cw_ref[0:1, :]
            for tap in range(1, CONV_W):
                acc = pltpu.roll(acc, 1, axis=0) + xall * cw_ref[tap:tap + 1, :]
            conv.append(acc[SUBLANES:])
        self.act = _silu(conv[0] if nb == 1 else jnp.concatenate(conv, axis=0))

    def mixed(self, ci, lo, hi):
        return self.act[:, lo:hi]

    def gate(self, ci, k):
        return self.ba_ref[0, :, k * LANES:(k + 1) * LANES]

    def z(self, ci, h):
        return self.p_ref[0, :, self.conv_ch + h * self.dv:self.conv_ch + (h + 1) * self.dv]

    def put(self, ci, h, value):
        self.o_ref[0, :, h * self.dv:(h + 1) * self.dv] = value


def _gdn_core_kernel(p_ref, ba_ref, s0_ref, c0_ref, cw_ref, alog_ref, dtb_ref, onw_ref,
                     o_ref, s_ref, c_ref, s_scr, xe_scr, *, heads, dk, dv, nb, cl):
    conv_ch = xe_scr.shape[2]
    s_scr[...] = s0_ref[...]
    for j in range(nb):
        xe_scr[j, 0:SUBLANES, :] = jnp.zeros((SUBLANES, conv_ch), F32)
        xe_scr[j, HIST0:SUBLANES, :] = c0_ref[0, j]
        xe_scr[j, SUBLANES:SUBLANES + cl, :] = p_ref[0, j * cl:(j + 1) * cl, 0:conv_ch]
    front = _RollFront(p_ref, ba_ref, o_ref, xe_scr, cw_ref, dv=dv, nb=nb)
    _gdn_step(front, lambda r: r & (cl - 1), alog_ref, dtb_ref, onw_ref, s_scr,
              heads=heads, dk=dk, dv=dv, nb=nb, cl=cl, nc=1)
    s_ref[...] = s_scr[...]
    for j in range(nb):
        c_ref[0, j] = xe_scr[j, cl + HIST0:cl + SUBLANES, :]


def _gdn_core(p, ba, s0, c0, conv_w, a_log, dt_bias, onorm_w):
    b, l, pw = p.shape
    heads, dk, dv = s0.shape[1:]
    conv_ch = c0.shape[3]
    assert heads <= SUBLANES and dk == LANES and dv == LANES
    assert pw == conv_ch + heads * dv and ba.shape[2] == 2 * LANES
    assert l <= CHUNK and l % SUBLANES == 0 and l & (l - 1) == 0
    nb = max(n for n in range(1, CHUNK // l + 1) if b % n == 0)
    groups, rows = b // nb, nb * l
    kern = functools.partial(_gdn_core_kernel, heads=heads, dk=dk, dv=dv, nb=nb, cl=l)
    o, s, cs = pl.pallas_call(
        kern, grid=(groups, 1),
        in_specs=[pl.BlockSpec((1, rows, pw), lambda i, t: (i, 0, 0)),
                  pl.BlockSpec((1, rows, 2 * LANES), lambda i, t: (i, 0, 0)),
                  pl.BlockSpec((nb, heads, dk, dv), lambda i, t: (i, 0, 0, 0)),
                  pl.BlockSpec((1, nb, CONV_W - 1, conv_ch), lambda i, t: (0, i, 0, 0)),
                  pl.BlockSpec((CONV_W, conv_ch), lambda i, t: (0, 0)),
                  pl.BlockSpec((1, LANES), lambda i, t: (0, 0)),
                  pl.BlockSpec((1, LANES), lambda i, t: (0, 0)),
                  pl.BlockSpec((1, dv), lambda i, t: (0, 0))],
        out_specs=[pl.BlockSpec((1, rows, heads * dv), lambda i, t: (i, 0, 0)),
                   pl.BlockSpec((nb, heads, dk, dv), lambda i, t: (i, 0, 0, 0)),
                   pl.BlockSpec((1, nb, CONV_W - 1, conv_ch), lambda i, t: (0, i, 0, 0))],
        out_shape=[jax.ShapeDtypeStruct((groups, rows, heads * dv), F32),
                   jax.ShapeDtypeStruct(s0.shape, F32),
                   jax.ShapeDtypeStruct(c0.shape, F32)],
        scratch_shapes=[pltpu.VMEM((nb, heads, dk, dv), F32),
                        pltpu.VMEM((nb, SUBLANES + l, conv_ch), F32)],
        compiler_params=_params("parallel", "arbitrary"), name="gdn_core",
    )(p.reshape(groups, rows, pw), ba.reshape(groups, rows, 2 * LANES), s0, c0, conv_w,
      _lane_pad(a_log), _lane_pad(dt_bias), onorm_w.reshape(1, dv))
    return o.reshape(b, l, heads * dv), s, cs


class _StridedFront:
    def __init__(self, xe_scr, zb_scr, o_scr, cw_ref, *, heads):
        self.xe, self.zb, self.o, self.cw, self.heads = xe_scr, zb_scr, o_scr, cw_ref, heads

    @staticmethod
    def tok(r):
        span = SUBLANES * ROW_STRIDE
        return ((r & (CHUNK - span)) + (lax.shift_right_logical(r, 3) & (ROW_STRIDE - 1))
                + (r & (SUBLANES - 1)) * ROW_STRIDE)

    @staticmethod
    def _starts(ci):
        return [ci * CHUNK + g * SUBLANES * ROW_STRIDE + s
                for g in range(CHUNK // (SUBLANES * ROW_STRIDE)) for s in range(ROW_STRIDE)]

    def _rows(self, ref, slab, base, ci):
        return jnp.concatenate([ref[slab, pl.ds(base + st, SUBLANES, stride=ROW_STRIDE), :]
                                for st in self._starts(ci)], axis=0)

    def mixed(self, ci, lo, hi):
        assert hi - lo == LANES and lo % LANES == 0
        acc = self._rows(self.xe, lo // LANES, HIST0, ci) * self.cw[0:1, lo:hi]
        for tap in range(1, CONV_W):
            acc = acc + self._rows(self.xe, lo // LANES, HIST0 + tap, ci) * self.cw[tap:tap + 1,
                                                                                   lo:hi]
        return _silu(acc)

    def gate(self, ci, k):
        return self._rows(self.zb, self.heads + k, 0, ci)

    def z(self, ci, h):
        return self._rows(self.zb, h, 0, ci)

    def put(self, ci, h, value):
        for n, st in enumerate(self._starts(ci)):
            self.o[h, pl.ds(st, SUBLANES, stride=ROW_STRIDE), :] = value[n * SUBLANES:
                                                                         (n + 1) * SUBLANES]


def _gdn_layer_kernel(x_ref, nw_ref, win_ref, wba_ref, cw_ref, alog_ref, dtb_ref, onw_ref,
                      wout_ref, y_ref, s_ref, c_ref, s_scr, xe_scr, zb_scr, o_scr,
                      *, heads, dk, dv, nc):
    rows = nc * CHUNK
    n_conv = xe_scr.shape[0]
    t = pl.program_id(1)

    @pl.when(t == 0)
    def _():
        s_scr[...] = jnp.zeros(s_scr.shape, F32)
        for s in range(n_conv):
            xe_scr[s, 0:SUBLANES, :] = jnp.zeros((SUBLANES, LANES), F32)

    x = x_ref[0]
    xn = _rms_scale(x, nw_ref[...]).astype(BF16)
    p = _dot(xn, win_ref[...])
    ba = _dot(xn, wba_ref[...])
    for s in range(n_conv):
        xe_scr[s, SUBLANES:SUBLANES + rows, :] = p[:, s * LANES:(s + 1) * LANES]
    for h in range(heads):
        zb_scr[h] = p[:, (n_conv + h) * LANES:(n_conv + h + 1) * LANES]
    zb_scr[heads] = ba[:, 0:LANES]
    zb_scr[heads + 1] = ba[:, LANES:2 * LANES]

    front = _StridedFront(xe_scr, zb_scr, o_scr, cw_ref, heads=heads)
    _gdn_step(front, front.tok, alog_ref, dtb_ref, onw_ref, s_scr,
              heads=heads, dk=dk, dv=dv, nb=1, cl=CHUNK, nc=nc)

    o = jnp.concatenate([o_scr[h] for h in range(heads)], axis=-1).astype(BF16)
    y_ref[0] = x + _dot(o, wout_ref[...])
    hist = [xe_scr[s, rows + HIST0:rows + SUBLANES, :] for s in range(n_conv)]
    for s in range(n_conv):
        xe_scr[s, HIST0:SUBLANES, :] = hist[s]

    @pl.when(t == pl.num_programs(1) - 1)
    def _():
        s_ref[...] = s_scr[...]
        for s in range(n_conv):
            c_ref[0, 0, :, s * LANES:(s + 1) * LANES] = hist[s]


def _gdn_layer(x, norm_w, w_in, w_ba, conv_w, a_log, dt_bias, onorm_w, w_out, heads):
    b, l, d = x.shape
    dv = w_out.shape[0] // heads
    dk = dv
    conv_ch = conv_w.shape[1]
    pw = w_in.shape[1]
    nc = GDN_CHUNKS_PER_STEP
    rows = CHUNK * nc
    assert l % rows == 0 and heads <= SUBLANES and dk == LANES and dv == LANES
    assert conv_ch == 3 * heads * dk and pw == conv_ch + heads * dv
    assert w_ba.shape[1] == 2 * LANES and CHUNK % (SUBLANES * ROW_STRIDE) == 0
    resident = dict(pipeline_mode=pl.Buffered(1))
    kern = functools.partial(_gdn_layer_kernel, heads=heads, dk=dk, dv=dv, nc=nc)
    return pl.pallas_call(
        kern, grid=(b, l // rows),
        in_specs=[pl.BlockSpec((1, rows, d), lambda i, t: (i, t, 0)),
                  pl.BlockSpec((1, d), lambda i, t: (0, 0)),
                  pl.BlockSpec((d, pw), lambda i, t: (0, 0), **resident),
                  pl.BlockSpec((d, 2 * LANES), lambda i, t: (0, 0), **resident),
                  pl.BlockSpec((CONV_W, conv_ch), lambda i, t: (0, 0)),
                  pl.BlockSpec((1, LANES), lambda i, t: (0, 0)),
                  pl.BlockSpec((1, LANES), lambda i, t: (0, 0)),
                  pl.BlockSpec((1, dv), lambda i, t: (0, 0)),
                  pl.BlockSpec((heads * dv, d), lambda i, t: (0, 0), **resident)],
        out_specs=[pl.BlockSpec((1, rows, d), lambda i, t: (i, t, 0)),
                   pl.BlockSpec((1, heads, dk, dv), lambda i, t: (i, 0, 0, 0)),
                   pl.BlockSpec((1, 1, CONV_W - 1, conv_ch), lambda i, t: (0, i, 0, 0))],
        out_shape=[jax.ShapeDtypeStruct((b, l, d), F32),
                   jax.ShapeDtypeStruct((b, heads, dk, dv), F32),
                   jax.ShapeDtypeStruct((1, b, CONV_W - 1, conv_ch), F32)],
        scratch_shapes=[pltpu.VMEM((1, heads, dk, dv), F32),
                        pltpu.VMEM((conv_ch // LANES, SUBLANES + rows, LANES), F32),
                        pltpu.VMEM((heads + 2, rows, LANES), F32),
                        pltpu.VMEM((heads, rows, LANES), F32)],
        compiler_params=_params("parallel", "arbitrary"), name="gdn_layer",
    )(x, norm_w.reshape(1, d), w_in, w_ba, conv_w, _lane_pad(a_log), _lane_pad(dt_bias),
      onorm_w.reshape(1, dv), w_out)


def _ret_chunk(cols, put_out, cos_ref, sin_ref, s0_ref, onw_ref, s_ref,
               *, heads, dk, dv, nb, c, first_step):
    qk_w = heads * dk
    g_off = 2 * qk_w + heads * dv
    half = dk // 2

    if first_step is not None:
        @pl.when(first_step)
        def _():
            s_ref[...] = jnp.zeros(s_ref.shape, F32) if s0_ref is None else s0_ref[...]
    state_ref = s0_ref if first_step is None else s_ref

    cos = cos_ref[...]
    sin = sin_ref[...]
    row = lax.broadcasted_iota(jnp.int32, (c, c), 0)
    col = lax.broadcasted_iota(jnp.int32, (c, c), 1)
    diff = (row - col).astype(F32)
    idx = lax.broadcasted_iota(jnp.int32, (c, 1), 0).astype(F32)

    def rotary(x):
        x1, x2 = x[:, :half], x[:, half:]
        return jnp.concatenate([x1 * cos - x2 * sin, x1 * sin + x2 * cos], axis=-1)

    decay, q_dec, k_dec, s_dec = [], [], [], []
    for h in range(heads):
        lg = jnp.log(jnp.full((1, 1), 1.0 - 2.0 ** (-5.0 - h), F32))
        decay.append(jnp.where(diff >= 0, jnp.exp(lg * jnp.maximum(diff, 0.0)), 0.0))
        q_dec.append(jnp.exp(lg * (idx + 1.0)))
        k_dec.append(jnp.exp(lg * (c - 1.0 - idx)))
        s_dec.append(jnp.exp(lg * c))

    units = [(j, h) for j in range(nb) for h in range(heads)]
    qb, kb, kdb, vb = [], [], [], []
    for j, h in units:
        q = rotary(cols(j, h * dk, (h + 1) * dk))
        k = rotary(cols(j, qk_w + h * dk, qk_w + (h + 1) * dk)) * (dk ** -0.5)
        qb.append(q.astype(BF16))
        kb.append(k.astype(BF16))
        kdb.append((k * k_dec[h]).astype(BF16))
        vb.append(cols(j, 2 * qk_w + h * dv, 2 * qk_w + (h + 1) * dv).astype(BF16))
    qk = [(_dot_nt(qb[i], kb[i]) * decay[h]).astype(BF16) for i, (j, h) in enumerate(units)]
    s_old = [state_ref[j, h] for j, h in units]
    qs = [_dot(qb[i], s_old[i].astype(BF16)) for i in range(len(units))]
    o = [_dot(qk[i], vb[i]) + q_dec[h] * qs[i] for i, (j, h) in enumerate(units)]
    for i, (j, h) in enumerate(units):
        s_ref[j, h] = s_dec[h] * s_old[i] + _dot_tn(kdb[i], vb[i])
    for i, (j, h) in enumerate(units):
        put_out(j, h, _rms_scale(o[i], onw_ref[h:h + 1, :])
                * _silu(cols(j, g_off + h * dv, g_off + (h + 1) * dv)))


def _ret_side_chunk(ps_ref, cos_ref, sin_ref, s0_ref, onw_ref, o_ref, s_ref, *, heads, dk, dv,
                    nb):
    c = ps_ref.shape[1] // nb

    def cols(j, lo, hi):
        return ps_ref[0, j * c:(j + 1) * c, lo:hi]

    def put_out(j, h, value):
        o_ref[0, j * c:(j + 1) * c, h * dv:(h + 1) * dv] = value

    _ret_chunk(cols, put_out, cos_ref, sin_ref, s0_ref, onw_ref, s_ref,
               heads=heads, dk=dk, dv=dv, nb=nb, c=c, first_step=None)


def _ret_core_kernel(ps_ref, cos_ref, sin_ref, s0_ref, onw_ref, o_ref, s_ref,
                     *, heads, dk, dv, nb):
    _ret_side_chunk(ps_ref, cos_ref, sin_ref, s0_ref, onw_ref, o_ref, s_ref,
                    heads=heads, dk=dk, dv=dv, nb=nb)


def _ret_side_specs(p_s, s0_s, nbs, group):
    bs, ls, pw = p_s.shape
    heads, dk, dv = s0_s.shape[1:]
    assert bs % nbs == 0 and ls <= CHUNK and ls % SUBLANES == 0 and dk // 2 == LANES
    assert pw == 2 * heads * dk + 2 * heads * dv
    in_specs = [pl.BlockSpec((1, nbs * ls, pw), lambda *g: (group(*g), 0, 0)),
                pl.BlockSpec((ls, dk // 2), lambda *g: (0, 0)),
                pl.BlockSpec((ls, dk // 2), lambda *g: (0, 0)),
                pl.BlockSpec((nbs, heads, dk, dv), lambda *g: (group(*g), 0, 0, 0))]
    out_specs = [pl.BlockSpec((1, nbs * ls, heads * dv), lambda *g: (group(*g), 0, 0)),
                 pl.BlockSpec((nbs, heads, dk, dv), lambda *g: (group(*g), 0, 0, 0))]
    out_shape = [jax.ShapeDtypeStruct((bs // nbs, nbs * ls, heads * dv), F32),
                 jax.ShapeDtypeStruct(s0_s.shape, F32)]
    return in_specs, out_specs, out_shape


def _ret_core(p_s, cos_s, sin_s, s0_s, onorm_w):
    bs, ls, pw = p_s.shape
    heads, dk, dv = s0_s.shape[1:]
    nbs = RET_SEQS_PER_STEP if bs % RET_SEQS_PER_STEP == 0 else 1
    in_specs, out_specs, out_shape = _ret_side_specs(p_s, s0_s, nbs, lambda i: i)
    in_specs.append(pl.BlockSpec((heads, dv), lambda i: (0, 0)))
    o, s = pl.pallas_call(
        functools.partial(_ret_core_kernel, heads=heads, dk=dk, dv=dv, nb=nbs),
        grid=(bs // nbs,), in_specs=in_specs, out_specs=out_specs, out_shape=out_shape,
        compiler_params=_params("parallel"), name="ret_core",
    )(p_s.reshape(bs // nbs, nbs * ls, pw), cos_s, sin_s, s0_s, onorm_w)
    return o.reshape(bs, ls, heads * dv), s


def _ret_layer_kernel(*refs, heads, dk, dv, c, side_seqs):
    x_ref, nw_ref, win_ref, cos_ref, sin_ref, onw_ref, wout_ref, fw_ref = refs[:8]
    if side_seqs:
        ps_ref, cos_s_ref, sin_s_ref, s0s_ref, y_ref, s_ref, os_ref, ss_ref, o_scr = refs[8:]
    else:
        y_ref, s_ref, o_scr = refs[8:]

    x = x_ref[0]
    p = _dot(_rms_scale(x, nw_ref[...]).astype(BF16), win_ref[...])

    def cols(j, lo, hi):
        return p[:, lo:hi]

    def put_out(j, h, value):
        o_scr[:, h * dv:(h + 1) * dv] = value.astype(BF16)

    _ret_chunk(cols, put_out, cos_ref, sin_ref, None, onw_ref, s_ref,
               heads=heads, dk=dk, dv=dv, nb=1, c=c, first_step=pl.program_id(1) == 0)
    if side_seqs:
        _ret_side_chunk(ps_ref, cos_s_ref, sin_s_ref, s0s_ref, onw_ref, os_ref, ss_ref,
                        heads=heads, dk=dk, dv=dv, nb=side_seqs)
    y_ref[0] = _rms_scale(x + _dot(o_scr[...], wout_ref[...]), fw_ref[...])


def _ret_layer(x, norm_w, w_in, cos, sin, onorm_w, w_out, final_w, heads, side=None):
    b, l, d = x.shape
    pw = w_in.shape[1]
    dv = w_out.shape[0] // heads
    dk = (pw - 2 * heads * dv) // (2 * heads)
    c = RET_CHUNK
    steps = l // c
    assert l % c == 0 and dk // 2 == LANES
    resident = dict(pipeline_mode=pl.Buffered(1))
    in_specs = [pl.BlockSpec((1, c, d), lambda i, t: (i, t, 0)),
                pl.BlockSpec((1, d), lambda i, t: (0, 0)),
                pl.BlockSpec((d, pw), lambda i, t: (0, 0), **resident),
                pl.BlockSpec((c, dk // 2), lambda i, t: (t, 0)),
                pl.BlockSpec((c, dk // 2), lambda i, t: (t, 0)),
                pl.BlockSpec((heads, dv), lambda i, t: (0, 0)),
                pl.BlockSpec((heads * dv, d), lambda i, t: (0, 0), **resident),
                pl.BlockSpec((1, d), lambda i, t: (0, 0))]
    out_specs = [pl.BlockSpec((1, c, d), lambda i, t: (i, t, 0)),
                 pl.BlockSpec((1, heads, dk, dv), lambda i, t: (i, 0, 0, 0))]
    out_shape = [jax.ShapeDtypeStruct((b, l, d), F32),
                 jax.ShapeDtypeStruct((b, heads, dk, dv), F32)]
    args = [x, norm_w.reshape(1, d), w_in, cos, sin, onorm_w, w_out, final_w.reshape(1, d)]
    nbs = 0
    if side is not None:
        p_s, cos_s, sin_s, s0_s = side
        bs, ls, _ = p_s.shape
        nbs = RET_SEQS_PER_STEP
        assert bs == nbs * b * steps and p_s.shape[2] == pw
        assert s0_s.shape == (bs, heads, dk, dv)
        side_in, side_out, side_shape = _ret_side_specs(p_s, s0_s, nbs,
                                                        lambda i, t: i * steps + t)
        in_specs += side_in
        out_specs += side_out
        out_shape += side_shape
        args += [p_s.reshape(bs // nbs, nbs * ls, pw), cos_s, sin_s, s0_s]
    kern = functools.partial(_ret_layer_kernel, heads=heads, dk=dk, dv=dv, c=c, side_seqs=nbs)
    outs = pl.pallas_call(
        kern, grid=(b, steps), in_specs=in_specs, out_specs=out_specs, out_shape=out_shape,
        scratch_shapes=[pltpu.VMEM((c, heads * dv), BF16)],
        compiler_params=_params("parallel", "arbitrary"), name="ret_layer",
    )(*args)
    if side is None:
        return outs
    y, s, o_s, s_s = outs
    return y, s, o_s.reshape(bs, ls, heads * dv), s_s


def _row_tile(t):
    for tm in (256, 128, 64, 32, 16, 8):
        if t % tm == 0:
            return tm
    raise ValueError(f"token count {t} is not a multiple of {SUBLANES}")


def _rope_tables(l, half, pos0):
    inv = 1.0 / (ROPE_BASE ** jnp.linspace(0.0, 1.0, half, dtype=F32))
    pos = pos0 + jnp.arange(l, dtype=F32)
    ang = pos[:, None] * inv[None, :]
    return jnp.cos(ang), jnp.sin(ang)


def _beta_decay_weight(w, heads, main):
    zeros = jnp.zeros((w.shape[0], LANES - heads), w.dtype)
    return jnp.concatenate([w[:, main:main + heads], zeros, w[:, main + heads:], zeros],
                           axis=1).astype(BF16)


def kernel(x_prompt, x_sample, state_gdn_ssm, state_gdn_conv, state_ret, norm_w, w_in_a,
           conv_w_a, a_log_a, dt_bias_a, onorm_a, w_out_a, w_in_b, onorm_b, w_out_b,
           final_norm_w):
    assert state_gdn_ssm.shape[0] == 1 and state_ret.shape[0] == 1 and norm_w.shape[0] == 2
    bp, lp, d = x_prompt.shape
    bs, ls, _ = x_sample.shape
    heads_a, dk_a, dv_a = state_gdn_ssm.shape[2:]
    heads_b, dk_b, dv_b = state_ret.shape[2:]
    main_a = 2 * heads_a * dk_a + 2 * heads_a * dv_a
    win_a = w_in_a[0][:, :main_a].astype(BF16)
    wba_a = _beta_decay_weight(w_in_a[0], heads_a, main_a)
    wout_a = w_out_a[0].astype(BF16)
    win_b = w_in_b[0].astype(BF16)
    wout_b = w_out_b[0].astype(BF16)

    ts = bs * ls
    tm = _row_tile(ts)
    xs = x_sample.reshape(ts, d)
    p, ba = _in_proj(xs, norm_w[0], [win_a, wba_a], tm)
    o, sa_s, ca_s = _gdn_core(p.reshape(bs, ls, -1), ba.reshape(bs, ls, -1), state_gdn_ssm[0],
                              state_gdn_conv, conv_w_a[0], a_log_a[0], dt_bias_a[0], onorm_a[0])
    xs = _out_proj(o.reshape(ts, -1), xs, wout_a, final_norm_w, tm, False)
    (p_s,) = _in_proj(xs, norm_w[1], [win_b], tm)
    p_s = p_s.reshape(bs, ls, -1)
    cos_s, sin_s = _rope_tables(ls, dk_b // 2, PAST_LEN)

    x1, sa_p, ca_p = _gdn_layer(x_prompt, norm_w[0], win_a, wba_a, conv_w_a[0], a_log_a[0],
                                dt_bias_a[0], onorm_a[0], wout_a, heads_a)
    cos_p, sin_p = _rope_tables(lp, dk_b // 2, 0.0)
    if bs == RET_SEQS_PER_STEP * bp * (lp // RET_CHUNK):
        y_p, sb_p, o, sb_s = _ret_layer(x1, norm_w[1], win_b, cos_p, sin_p, onorm_b[0], wout_b,
                                        final_norm_w, heads_b,
                                        side=(p_s, cos_s, sin_s, state_ret[0]))
    else:
        y_p, sb_p = _ret_layer(x1, norm_w[1], win_b, cos_p, sin_p, onorm_b[0], wout_b,
                               final_norm_w, heads_b)
        o, sb_s = _ret_core(p_s, cos_s, sin_s, state_ret[0], onorm_b[0])
    y_s = _out_proj(o.reshape(ts, -1), xs, wout_b, final_norm_w, tm, True).reshape(bs, ls, d)
    return (y_p, y_s, sa_p[None], ca_p, sb_p[None], sa_s[None], ca_s, sb_s[None])
```

```python
import functools

import jax
import jax.numpy as jnp
from jax import lax
from jax.experimental import pallas as pl
from jax.experimental.pallas import tpu as pltpu

F32 = jnp.float32
BF16 = jnp.bfloat16
EPS = 1e-6
CHUNK = 64
CONV_W = 4
ROPE_BASE = 10000.0
PAST_LEN = 16384.0
LANES = 128
SUBLANES = 8
HIST0 = SUBLANES - (CONV_W - 1)
VMEM_LIMIT_BYTES = 60 * 1024 * 1024
GDN_CHUNKS_PER_STEP = 4
GDN_BLOCKS_PER_STEP = 2
IN_PROJ_TILE = 256
ROW_STRIDE = 4
RET_CHUNK = 256
RET_SEQS_PER_STEP = 2


def _dot(a, b):
    return jnp.dot(a, b, preferred_element_type=F32)


def _dot_nt(a, b):
    return lax.dot_general(a, b, (((1,), (1,)), ((), ())), preferred_element_type=F32)


def _dot_tn(a, b):
    return lax.dot_general(a, b, (((0,), (0,)), ((), ())), preferred_element_type=F32)


def _split3(x):
    hi = x.astype(BF16)
    r = x - hi.astype(F32)
    mid = r.astype(BF16)
    lo = (r - mid.astype(F32)).astype(BF16)
    return hi, mid, lo


def _sigmoid(x):
    return 1.0 / (1.0 + jnp.exp(-x))


def _silu(x):
    return x * _sigmoid(x)


def _softplus(x):
    return jnp.maximum(x, 0.0) + jnp.log1p(jnp.exp(-jnp.abs(x)))


def _rms_scale(x, w):
    return x * lax.rsqrt(jnp.mean(x * x, axis=-1, keepdims=True) + EPS) * w


def _params(*sem):
    return pltpu.CompilerParams(dimension_semantics=sem, vmem_limit_bytes=VMEM_LIMIT_BYTES)


def _lane_pad(vec):
    return jnp.pad(vec.astype(F32), (0, LANES - vec.shape[0])).reshape(1, LANES)


def _in_proj_kernel(x_ref, nw_ref, *refs):
    n = len(refs) // 2
    xn = _rms_scale(x_ref[...], nw_ref[...]).astype(BF16)
    for w_ref, o_ref in zip(refs[:n], refs[n:]):
        o_ref[...] = _dot(xn, w_ref[...])


def _in_proj(x, norm_w, weights, tm):
    t, d = x.shape
    assert t % tm == 0
    in_specs = [pl.BlockSpec((tm, d), lambda i: (i, 0)),
                pl.BlockSpec((1, d), lambda i: (0, 0))]
    in_specs += [pl.BlockSpec(w.shape, lambda i: (0, 0)) for w in weights]
    return pl.pallas_call(
        _in_proj_kernel, grid=(t // tm,), in_specs=in_specs,
        out_specs=[pl.BlockSpec((tm, w.shape[1]), lambda i: (i, 0)) for w in weights],
        out_shape=[jax.ShapeDtypeStruct((t, w.shape[1]), F32) for w in weights],
        compiler_params=_params("parallel"), name="in_proj",
    )(x, norm_w.reshape(1, d), *weights)


def _out_proj_kernel(o_ref, x_ref, w_ref, fw_ref, y_ref, *, final_norm):
    y = x_ref[...] + _dot(o_ref[...].astype(BF16), w_ref[...])
    y_ref[...] = _rms_scale(y, fw_ref[...]) if final_norm else y


def _out_proj(o, x, w, final_w, tm, final_norm):
    t, k = o.shape
    d = x.shape[1]
    assert t % tm == 0
    return pl.pallas_call(
        functools.partial(_out_proj_kernel, final_norm=final_norm),
        grid=(t // tm,),
        in_specs=[pl.BlockSpec((tm, k), lambda i: (i, 0)),
                  pl.BlockSpec((tm, d), lambda i: (i, 0)),
                  pl.BlockSpec((k, d), lambda i: (0, 0)),
                  pl.BlockSpec((1, d), lambda i: (0, 0))],
        out_specs=pl.BlockSpec((tm, d), lambda i: (i, 0)),
        out_shape=jax.ShapeDtypeStruct((t, d), F32),
        compiler_params=_params("parallel"), name="out_proj",
    )(o, x, w, final_w.reshape(1, d))


def _unit_lower_solves(lows, rhs, eye, cl):
    powers = [(-low).astype(BF16) for low in lows]
    invs = [eye - low for low in lows]
    p = 1
    while 2 * p < cl:
        sq = [_dot(pw, pw) for pw in powers]
        p *= 2
        powers = [s.astype(BF16) for s in sq]
        invs = [inv + _dot(inv.astype(BF16), pw) for inv, pw in zip(invs, powers)]
    return [_dot(inv.astype(BF16), r) for inv, r in zip(invs, rhs)]


def _gdn_step(front, tok, alog_ref, dtb_ref, onw_ref, s_scr, *, heads, dk, dv, nb, cl, nc,
              fill=lambda: None):
    c = nb * cl
    qk_w = heads * dk
    row = lax.broadcasted_iota(jnp.int32, (c, c), 0)
    col = lax.broadcasted_iota(jnp.int32, (c, c), 1)
    shift = cl.bit_length() - 1
    same = lax.shift_right_logical(row, shift) == lax.shift_right_logical(col, shift)
    incl = same & (tok(row) >= tok(col))
    strict = same & (tok(row) > tok(col))
    eye = (row == col).astype(F32)
    cum_masks = jnp.concatenate([incl.astype(BF16), same.astype(BF16)], axis=0)
    sel = (lax.broadcasted_iota(jnp.int32, (SUBLANES, LANES), 0)
           == lax.broadcasted_iota(jnp.int32, (SUBLANES, LANES), 1)).astype(BF16)
    neg_a = -jnp.exp(alog_ref[...])
    units = [(ci, h) for ci in range(nc) for h in range(heads)]
    seqs = range(nb)

    def stack(parts):
        return parts[0] if nb == 1 else jnp.concatenate(parts, axis=0)

    beta, gcum, gtot, gcum_t = [], [], [], []
    for ci in range(nc):
        beta.append(_sigmoid(front.gate(ci, 0)))
        g = neg_a * _softplus(front.gate(ci, 1) + dtb_ref[...])
        g3 = _split3(g)
        gg = _dot(cum_masks, g3[0]) + (_dot(cum_masks, g3[1]) + _dot(cum_masks, g3[2]))
        gcum.append(gg[:c])
        gtot.append(gg[c:])
        gc3 = _split3(gg[:c])
        gcum_t.append(_dot_nt(sel, gc3[0]) + (_dot_nt(sel, gc3[1]) + _dot_nt(sel, gc3[2])))

    q_l, eg_l, lows, rhs_l, qk_l, kd_l = [], [], [], [], [], []
    for ci, h in units:
        q = front.mixed(ci, h * dk, (h + 1) * dk)
        k = front.mixed(ci, qk_w + h * dk, qk_w + (h + 1) * dk)
        v = front.mixed(ci, 2 * qk_w + h * dv, 2 * qk_w + (h + 1) * dv)
        q = q * lax.rsqrt(jnp.sum(q * q, axis=-1, keepdims=True) + EPS) * (dk ** -0.5)
        k = k * lax.rsqrt(jnp.sum(k * k, axis=-1, keepdims=True) + EPS)
        gc = gcum[ci][:, h:h + 1]
        gr = gcum_t[ci][h:h + 1, :]
        bh = beta[ci][:, h:h + 1]
        dec = jnp.where(incl, jnp.exp(jnp.where(incl, gc - gr, 0.0)), 0.0)
        eg = jnp.exp(gc)
        qb = q.astype(BF16)
        kb = k.astype(BF16)
        lows.append(jnp.where(strict, bh * _dot_nt(kb, kb) * dec, 0.0))
        qk_l.append((_dot_nt(qb, kb) * dec).astype(BF16))
        rhs_l.append(jnp.concatenate([bh * v, (bh * eg) * k], axis=-1).astype(BF16))
        kd_l.append(k * jnp.exp(gtot[ci][:, h:h + 1] - gc))
        q_l.append(q)
        eg_l.append(eg)
        fill()
    sols = _unit_lower_solves(lows, rhs_l, eye, cl)

    for ci in range(nc):
        idx = [ci * heads + h for h in range(heads)]
        s_old = [[s_scr[j, h] for j in seqs] for h in range(heads)]
        res = [[_dot(jnp.concatenate([sols[i][j * cl:(j + 1) * cl, dv:],
                                      q_l[i][j * cl:(j + 1) * cl]], axis=0).astype(BF16),
                     s_old[h][j].astype(BF16)) for j in seqs]
               for h, i in enumerate(idx)]
        u = [stack([sols[i][j * cl:(j + 1) * cl, :dv] - res[h][j][:cl] for j in seqs])
             for h, i in enumerate(idx)]
        o = [eg_l[i] * stack([res[h][j][cl:] for j in seqs]) + _dot(qk_l[i], u[h].astype(BF16))
             for h, i in enumerate(idx)]
        for h, i in enumerate(idx):
            for j in seqs:
                r0, r1 = j * cl, (j + 1) * cl
                s_scr[j, h] = (jnp.exp(gtot[ci][r0:r0 + 1, h:h + 1]) * s_old[h][j]
                               + _dot_tn(kd_l[i][r0:r1].astype(BF16), u[h][r0:r1].astype(BF16)))
        for h in range(heads):
            front.put(ci, h, _rms_scale(o[h], onw_ref[...]) * _silu(front.z(ci, h)))


class _RollFront:
    def __init__(self, p_ref, ba_ref, o_ref, xe_scr, cw_ref, *, dv, nb):
        self.p_ref, self.ba_ref, self.o_ref, self.dv = p_ref, ba_ref, o_ref, dv
        self.conv_ch = xe_scr.shape[2]
        conv = []
        for j in range(nb):
            xall = xe_scr[j]
            acc = xall * cw_ref[0:1, :]
            for tap in range(1, CONV_W):
                acc = pltpu.roll(acc, 1, axis=0) + xall * cw_ref[tap:tap + 1, :]
            conv.append(acc[SUBLANES:])
        self.act = _silu(conv[0] if nb == 1 else jnp.concatenate(conv, axis=0))

    def mixed(self, ci, lo, hi):
        return self.act[:, lo:hi]

    def gate(self, ci, k):
        return self.ba_ref[0, :, k * LANES:(k + 1) * LANES]

    def z(self, ci, h):
        return self.p_ref[0, :, self.conv_ch + h * self.dv:self.conv_ch + (h + 1) * self.dv]

    def put(self, ci, h, value):
        self.o_ref[0, :, h * self.dv:(h + 1) * self.dv] = value


def _gdn_core_kernel(p_ref, ba_ref, s0_ref, c0_ref, cw_ref, alog_ref, dtb_ref, onw_ref,
                     o_ref, s_ref, c_ref, s_scr, xe_scr, *, heads, dk, dv, nb, cl):
    conv_ch = xe_scr.shape[2]
    s_scr[...] = s0_ref[...]
    for j in range(nb):
        xe_scr[j, 0:SUBLANES, :] = jnp.zeros((SUBLANES, conv_ch), F32)
        xe_scr[j, HIST0:SUBLANES, :] = c0_ref[0, j]
        xe_scr[j, SUBLANES:SUBLANES + cl, :] = p_ref[0, j * cl:(j + 1) * cl, 0:conv_ch]
    front = _RollFront(p_ref, ba_ref, o_ref, xe_scr, cw_ref, dv=dv, nb=nb)
    _gdn_step(front, lambda r: r & (cl - 1), alog_ref, dtb_ref, onw_ref, s_scr,
              heads=heads, dk=dk, dv=dv, nb=nb, cl=cl, nc=1)
    s_ref[...] = s_scr[...]
    for j in range(nb):
        c_ref[0, j] = xe_scr[j, cl + HIST0:cl + SUBLANES, :]


def _gdn_core(p, ba, s0, c0, conv_w, a_log, dt_bias, onorm_w):
    b, l, pw = p.shape
    heads, dk, dv = s0.shape[1:]
    conv_ch = c0.shape[3]
    assert heads <= SUBLANES and dk == LANES and dv == LANES
    assert pw == conv_ch + heads * dv and ba.shape[2] == 2 * LANES
    assert l <= CHUNK and l % SUBLANES == 0 and l & (l - 1) == 0
    nb = max(n for n in range(1, CHUNK // l + 1) if b % n == 0)
    groups, rows = b // nb, nb * l
    kern = functools.partial(_gdn_core_kernel, heads=heads, dk=dk, dv=dv, nb=nb, cl=l)
    o, s, cs = pl.pallas_call(
        kern, grid=(groups, 1),
        in_specs=[pl.BlockSpec((1, rows, pw), lambda i, t: (i, 0, 0)),
                  pl.BlockSpec((1, rows, 2 * LANES), lambda i, t: (i, 0, 0)),
                  pl.BlockSpec((nb, heads, dk, dv), lambda i, t: (i, 0, 0, 0)),
                  pl.BlockSpec((1, nb, CONV_W - 1, conv_ch), lambda i, t: (0, i, 0, 0)),
                  pl.BlockSpec((CONV_W, conv_ch), lambda i, t: (0, 0)),
                  pl.BlockSpec((1, LANES), lambda i, t: (0, 0)),
                  pl.BlockSpec((1, LANES), lambda i, t: (0, 0)),
                  pl.BlockSpec((1, dv), lambda i, t: (0, 0))],
        out_specs=[pl.BlockSpec((1, rows, heads * dv), lambda i, t: (i, 0, 0)),
                   pl.BlockSpec((nb, heads, dk, dv), lambda i, t: (i, 0, 0, 0)),
                   pl.BlockSpec((1, nb, CONV_W - 1, conv_ch), lambda i, t: (0, i, 0, 0))],
        out_shape=[jax.ShapeDtypeStruct((groups, rows, heads * dv), F32),
                   jax.ShapeDtypeStruct(s0.shape, F32),
                   jax.ShapeDtypeStruct(c0.shape, F32)],
        scratch_shapes=[pltpu.VMEM((nb, heads, dk, dv), F32),
                        pltpu.VMEM((nb, SUBLANES + l, conv_ch), F32)],
        compiler_params=_params("parallel", "arbitrary"), name="gdn_core",
    )(p.reshape(groups, rows, pw), ba.reshape(groups, rows, 2 * LANES), s0, c0, conv_w,
      _lane_pad(a_log), _lane_pad(dt_bias), onorm_w.reshape(1, dv))
    return o.reshape(b, l, heads * dv), s, cs


class _StridedFront:
    def __init__(self, xe_scr, zb_scr, o_scr, cw_ref, *, heads):
        self.xe, self.zb, self.o, self.cw, self.heads = xe_scr, zb_scr, o_scr, cw_ref, heads

    @staticmethod
    def tok(r):
        span = SUBLANES * ROW_STRIDE
        return ((r & (CHUNK - span)) + (lax.shift_right_logical(r, 3) & (ROW_STRIDE - 1))
                + (r & (SUBLANES - 1)) * ROW_STRIDE)

    @staticmethod
    def _starts(ci):
        return [ci * CHUNK + g * SUBLANES * ROW_STRIDE + s
                for g in range(CHUNK // (SUBLANES * ROW_STRIDE)) for s in range(ROW_STRIDE)]

    def _rows(self, ref, slab, base, ci):
        return jnp.concatenate([ref[slab, pl.ds(base + st, SUBLANES, stride=ROW_STRIDE), :]
                                for st in self._starts(ci)], axis=0)

    def mixed(self, ci, lo, hi):
        assert hi - lo == LANES and lo % LANES == 0
        acc = self._rows(self.xe, lo // LANES, HIST0, ci) * self.cw[0:1, lo:hi]
        for tap in range(1, CONV_W):
            acc = acc + self._rows(self.xe, lo // LANES, HIST0 + tap, ci) * self.cw[tap:tap + 1,
                                                                                   lo:hi]
        return _silu(acc)

    def gate(self, ci, k):
        return self._rows(self.zb, self.heads + k, 0, ci)

    def z(self, ci, h):
        return self._rows(self.zb, h, 0, ci)

    def put(self, ci, h, value):
        for n, st in enumerate(self._starts(ci)):
            self.o[h, pl.ds(st, SUBLANES, stride=ROW_STRIDE), :] = value[n * SUBLANES:
                                                                         (n + 1) * SUBLANES]


def _gdn_layer_kernel(x_ref, xnext_ref, nw_ref, win_ref, wba_ref, cw_ref, alog_ref, dtb_ref,
                      onw_ref, wout_ref, y_ref, s_ref, c_ref, s_scr, xe_scr, zb_scr, o_scr,
                      *, heads, dk, dv, nc):
    rows = nc * CHUNK
    blocks = xe_scr.shape[0]
    n_conv = xe_scr.shape[1]
    t = pl.program_id(1)
    step = pl.program_id(0) * pl.num_programs(1) + t

    def in_proj_tiles(x_rows, blk):
        xn = _rms_scale(x_rows(), nw_ref[...]).astype(BF16)
        for lo in range(0, (n_conv + heads) * LANES, IN_PROJ_TILE):
            r = _dot(xn, win_ref[:, lo:lo + IN_PROJ_TILE])
            for k in range(IN_PROJ_TILE // LANES):
                s = lo // LANES + k
                piece = r[:, k * LANES:(k + 1) * LANES]
                if s < n_conv:
                    xe_scr[blk, s, SUBLANES:SUBLANES + rows, :] = piece
                else:
                    zb_scr[blk, s - n_conv] = piece
            yield
        ba = _dot(xn, wba_ref[...])
        zb_scr[blk, heads] = ba[:, 0:LANES]
        zb_scr[blk, heads + 1] = ba[:, LANES:2 * LANES]
        yield

    def block_rows(blk):
        return lambda: x_ref[0, blk * rows:(blk + 1) * rows, :]

    @pl.when(step == 0)
    def _():
        for _ in in_proj_tiles(block_rows(0), 0):
            pass

    @pl.when(t == 0)
    def _():
        s_scr[...] = jnp.zeros(s_scr.shape, F32)
        for s in range(n_conv):
            xe_scr[0, s, 0:SUBLANES, :] = jnp.zeros((SUBLANES, LANES), F32)

    for blk in range(blocks):
        xe, zb, osc = xe_scr.at[blk], zb_scr.at[blk], o_scr.at[blk]
        if blk + 1 < blocks:
            ahead = in_proj_tiles(block_rows(blk + 1), blk + 1)
        else:
            ahead = in_proj_tiles(lambda: xnext_ref[0], 0)
        front = _StridedFront(xe, zb, osc, cw_ref, heads=heads)
        _gdn_step(front, front.tok, alog_ref, dtb_ref, onw_ref, s_scr, heads=heads, dk=dk,
                  dv=dv, nb=1, cl=CHUNK, nc=nc, fill=lambda: next(ahead, None))
        for _ in ahead:
            pass
        o = jnp.concatenate([osc[h] for h in range(heads)], axis=-1).astype(BF16)
        y_ref[0, blk * rows:(blk + 1) * rows, :] = (x_ref[0, blk * rows:(blk + 1) * rows, :]
                                                    + _dot(o, wout_ref[...]))
        hist = [xe[s, rows + HIST0:rows + SUBLANES, :] for s in range(n_conv)]
        nxt = xe_scr.at[(blk + 1) % blocks]
        for s in range(n_conv):
            nxt[s, HIST0:SUBLANES, :] = hist[s]

    @pl.when(t == pl.num_programs(1) - 1)
    def _():
        s_ref[...] = s_scr[...]
        for s in range(n_conv):
            c_ref[0, 0, :, s * LANES:(s + 1) * LANES] = hist[s]


def _gdn_layer(x, norm_w, w_in, w_ba, conv_w, a_log, dt_bias, onorm_w, w_out, heads):
    b, l, d = x.shape
    dv = w_out.shape[0] // heads
    dk = dv
    conv_ch = conv_w.shape[1]
    pw = w_in.shape[1]
    nc, blocks = GDN_CHUNKS_PER_STEP, GDN_BLOCKS_PER_STEP
    rows = CHUNK * nc
    step_rows = rows * blocks
    steps = l // step_rows
    assert l % step_rows == 0 and heads <= SUBLANES and dk == LANES and dv == LANES
    assert conv_ch == 3 * heads * dk and pw == conv_ch + heads * dv and blocks >= 2
    assert w_ba.shape[1] == 2 * LANES and CHUNK % (SUBLANES * ROW_STRIDE) == 0
    assert pw % IN_PROJ_TILE == 0
    resident = dict(pipeline_mode=pl.Buffered(1))

    def next_first_block(i, t):
        nxt = jnp.minimum(i * steps + t + 1, b * steps - 1)
        return (nxt // steps, (nxt % steps) * blocks, 0)

    kern = functools.partial(_gdn_layer_kernel, heads=heads, dk=dk, dv=dv, nc=nc)
    return pl.pallas_call(
        kern, grid=(b, steps),
        in_specs=[pl.BlockSpec((1, step_rows, d), lambda i, t: (i, t, 0)),
                  pl.BlockSpec((1, rows, d), next_first_block),
                  pl.BlockSpec((1, d), lambda i, t: (0, 0)),
                  pl.BlockSpec((d, pw), lambda i, t: (0, 0), **resident),
                  pl.BlockSpec((d, 2 * LANES), lambda i, t: (0, 0), **resident),
                  pl.BlockSpec((CONV_W, conv_ch), lambda i, t: (0, 0)),
                  pl.BlockSpec((1, LANES), lambda i, t: (0, 0)),
                  pl.BlockSpec((1, LANES), lambda i, t: (0, 0)),
                  pl.BlockSpec((1, dv), lambda i, t: (0, 0)),
                  pl.BlockSpec((heads * dv, d), lambda i, t: (0, 0), **resident)],
        out_specs=[pl.BlockSpec((1, step_rows, d), lambda i, t: (i, t, 0)),
                   pl.BlockSpec((1, heads, dk, dv), lambda i, t: (i, 0, 0, 0)),
                   pl.BlockSpec((1, 1, CONV_W - 1, conv_ch), lambda i, t: (0, i, 0, 0))],
        out_shape=[jax.ShapeDtypeStruct((b, l, d), F32),
                   jax.ShapeDtypeStruct((b, heads, dk, dv), F32),
                   jax.ShapeDtypeStruct((1, b, CONV_W - 1, conv_ch), F32)],
        scratch_shapes=[pltpu.VMEM((1, heads, dk, dv), F32),
                        pltpu.VMEM((blocks, conv_ch // LANES, SUBLANES + rows, LANES), F32),
                        pltpu.VMEM((blocks, heads + 2, rows, LANES), F32),
                        pltpu.VMEM((blocks, heads, rows, LANES), F32)],
        compiler_params=_params("arbitrary", "arbitrary"), name="gdn_layer",
    )(x, x, norm_w.reshape(1, d), w_in, w_ba, conv_w, _lane_pad(a_log), _lane_pad(dt_bias),
      onorm_w.reshape(1, dv), w_out)


def _ret_chunk(cols, put_out, cos_ref, sin_ref, s0_ref, onw_ref, s_ref,
               *, heads, dk, dv, nb, c, first_step):
    qk_w = heads * dk
    g_off = 2 * qk_w + heads * dv
    half = dk // 2

    if first_step is not None:
        @pl.when(first_step)
        def _():
            s_ref[...] = jnp.zeros(s_ref.shape, F32) if s0_ref is None else s0_ref[...]
    state_ref = s0_ref if first_step is None else s_ref

    cos = cos_ref[...]
    sin = sin_ref[...]
    row = lax.broadcasted_iota(jnp.int32, (c, c), 0)
    col = lax.broadcasted_iota(jnp.int32, (c, c), 1)
    diff = (row - col).astype(F32)
    idx = lax.broadcasted_iota(jnp.int32, (c, 1), 0).astype(F32)

    def rotary(x):
        x1, x2 = x[:, :half], x[:, half:]
        return jnp.concatenate([x1 * cos - x2 * sin, x1 * sin + x2 * cos], axis=-1)

    decay, q_dec, k_dec, s_dec = [], [], [], []
    for h in range(heads):
        lg = jnp.log(jnp.full((1, 1), 1.0 - 2.0 ** (-5.0 - h), F32))
        decay.append(jnp.where(diff >= 0, jnp.exp(lg * jnp.maximum(diff, 0.0)), 0.0))
        q_dec.append(jnp.exp(lg * (idx + 1.0)))
        k_dec.append(jnp.exp(lg * (c - 1.0 - idx)))
        s_dec.append(jnp.exp(lg * c))

    units = [(j, h) for j in range(nb) for h in range(heads)]
    qb, kb, kdb, vb = [], [], [], []
    for j, h in units:
        q = rotary(cols(j, h * dk, (h + 1) * dk))
        k = rotary(cols(j, qk_w + h * dk, qk_w + (h + 1) * dk)) * (dk ** -0.5)
        qb.append(q.astype(BF16))
        kb.append(k.astype(BF16))
        kdb.append((k * k_dec[h]).astype(BF16))
        vb.append(cols(j, 2 * qk_w + h * dv, 2 * qk_w + (h + 1) * dv).astype(BF16))
    qk = [(_dot_nt(qb[i], kb[i]) * decay[h]).astype(BF16) for i, (j, h) in enumerate(units)]
    s_old = [state_ref[j, h] for j, h in units]
    qs = [_dot(qb[i], s_old[i].astype(BF16)) for i in range(len(units))]
    o = [_dot(qk[i], vb[i]) + q_dec[h] * qs[i] for i, (j, h) in enumerate(units)]
    for i, (j, h) in enumerate(units):
        s_ref[j, h] = s_dec[h] * s_old[i] + _dot_tn(kdb[i], vb[i])
    for i, (j, h) in enumerate(units):
        put_out(j, h, _rms_scale(o[i], onw_ref[h:h + 1, :])
                * _silu(cols(j, g_off + h * dv, g_off + (h + 1) * dv)))


def _ret_side_chunk(ps_ref, cos_ref, sin_ref, s0_ref, onw_ref, o_ref, s_ref, *, heads, dk, dv,
                    nb):
    c = ps_ref.shape[1] // nb

    def cols(j, lo, hi):
        return ps_ref[0, j * c:(j + 1) * c, lo:hi]

    def put_out(j, h, value):
        o_ref[0, j * c:(j + 1) * c, h * dv:(h + 1) * dv] = value

    _ret_chunk(cols, put_out, cos_ref, sin_ref, s0_ref, onw_ref, s_ref,
               heads=heads, dk=dk, dv=dv, nb=nb, c=c, first_step=None)


def _ret_core_kernel(ps_ref, cos_ref, sin_ref, s0_ref, onw_ref, o_ref, s_ref,
                     *, heads, dk, dv, nb):
    _ret_side_chunk(ps_ref, cos_ref, sin_ref, s0_ref, onw_ref, o_ref, s_ref,
                    heads=heads, dk=dk, dv=dv, nb=nb)


def _ret_side_specs(p_s, s0_s, nbs, group):
    bs, ls, pw = p_s.shape
    heads, dk, dv = s0_s.shape[1:]
    assert bs % nbs == 0 and ls <= CHUNK and ls % SUBLANES == 0 and dk // 2 == LANES
    assert pw == 2 * heads * dk + 2 * heads * dv
    in_specs = [pl.BlockSpec((1, nbs * ls, pw), lambda *g: (group(*g), 0, 0)),
                pl.BlockSpec((ls, dk // 2), lambda *g: (0, 0)),
                pl.BlockSpec((ls, dk // 2), lambda *g: (0, 0)),
                pl.BlockSpec((nbs, heads, dk, dv), lambda *g: (group(*g), 0, 0, 0))]
    out_specs = [pl.BlockSpec((1, nbs * ls, heads * dv), lambda *g: (group(*g), 0, 0)),
                 pl.BlockSpec((nbs, heads, dk, dv), lambda *g: (group(*g), 0, 0, 0))]
    out_shape = [jax.ShapeDtypeStruct((bs // nbs, nbs * ls, heads * dv), F32),
                 jax.ShapeDtypeStruct(s0_s.shape, F32)]
    return in_specs, out_specs, out_shape


def _ret_core(p_s, cos_s, sin_s, s0_s, onorm_w):
    bs, ls, pw = p_s.shape
    heads, dk, dv = s0_s.shape[1:]
    nbs = RET_SEQS_PER_STEP if bs % RET_SEQS_PER_STEP == 0 else 1
    in_specs, out_specs, out_shape = _ret_side_specs(p_s, s0_s, nbs, lambda i: i)
    in_specs.append(pl.BlockSpec((heads, dv), lambda i: (0, 0)))
    o, s = pl.pallas_call(
        functools.partial(_ret_core_kernel, heads=heads, dk=dk, dv=dv, nb=nbs),
        grid=(bs // nbs,), in_specs=in_specs, out_specs=out_specs, out_shape=out_shape,
        compiler_params=_params("parallel"), name="ret_core",
    )(p_s.reshape(bs // nbs, nbs * ls, pw), cos_s, sin_s, s0_s, onorm_w)
    return o.reshape(bs, ls, heads * dv), s


def _ret_layer_kernel(*refs, heads, dk, dv, c, side_seqs):
    x_ref, nw_ref, win_ref, cos_ref, sin_ref, onw_ref, wout_ref, fw_ref = refs[:8]
    if side_seqs:
        ps_ref, cos_s_ref, sin_s_ref, s0s_ref, y_ref, s_ref, os_ref, ss_ref, o_scr = refs[8:]
    else:
        y_ref, s_ref, o_scr = refs[8:]

    x = x_ref[0]
    p = _dot(_rms_scale(x, nw_ref[...]).astype(BF16), win_ref[...])

    def cols(j, lo, hi):
        return p[:, lo:hi]

    def put_out(j, h, value):
        o_scr[:, h * dv:(h + 1) * dv] = value.astype(BF16)

    _ret_chunk(cols, put_out, cos_ref, sin_ref, None, onw_ref, s_ref,
               heads=heads, dk=dk, dv=dv, nb=1, c=c, first_step=pl.program_id(1) == 0)
    if side_seqs:
        _ret_side_chunk(ps_ref, cos_s_ref, sin_s_ref, s0s_ref, onw_ref, os_ref, ss_ref,
                        heads=heads, dk=dk, dv=dv, nb=side_seqs)
    y_ref[0] = _rms_scale(x + _dot(o_scr[...], wout_ref[...]), fw_ref[...])


def _ret_layer(x, norm_w, w_in, cos, sin, onorm_w, w_out, final_w, heads, side=None):
    b, l, d = x.shape
    pw = w_in.shape[1]
    dv = w_out.shape[0] // heads
    dk = (pw - 2 * heads * dv) // (2 * heads)
    c = RET_CHUNK
    steps = l // c
    assert l % c == 0 and dk // 2 == LANES
    resident = dict(pipeline_mode=pl.Buffered(1))
    in_specs = [pl.BlockSpec((1, c, d), lambda i, t: (i, t, 0)),
                pl.BlockSpec((1, d), lambda i, t: (0, 0)),
                pl.BlockSpec((d, pw), lambda i, t: (0, 0), **resident),
                pl.BlockSpec((c, dk // 2), lambda i, t: (t, 0)),
                pl.BlockSpec((c, dk // 2), lambda i, t: (t, 0)),
                pl.BlockSpec((heads, dv), lambda i, t: (0, 0)),
                pl.BlockSpec((heads * dv, d), lambda i, t: (0, 0), **resident),
                pl.BlockSpec((1, d), lambda i, t: (0, 0))]
    out_specs = [pl.BlockSpec((1, c, d), lambda i, t: (i, t, 0)),
                 pl.BlockSpec((1, heads, dk, dv), lambda i, t: (i, 0, 0, 0))]
    out_shape = [jax.ShapeDtypeStruct((b, l, d), F32),
                 jax.ShapeDtypeStruct((b, heads, dk, dv), F32)]
    args = [x, norm_w.reshape(1, d), w_in, cos, sin, onorm_w, w_out, final_w.reshape(1, d)]
    nbs = 0
    if side is not None:
        p_s, cos_s, sin_s, s0_s = side
        bs, ls, _ = p_s.shape
        nbs = RET_SEQS_PER_STEP
        assert bs == nbs * b * steps and p_s.shape[2] == pw
        assert s0_s.shape == (bs, heads, dk, dv)
        side_in, side_out, side_shape = _ret_side_specs(p_s, s0_s, nbs,
                                                        lambda i, t: i * steps + t)
        in_specs += side_in
        out_specs += side_out
        out_shape += side_shape
        args += [p_s.reshape(bs // nbs, nbs * ls, pw), cos_s, sin_s, s0_s]
    kern = functools.partial(_ret_layer_kernel, heads=heads, dk=dk, dv=dv, c=c, side_seqs=nbs)
    outs = pl.pallas_call(
        kern, grid=(b, steps), in_specs=in_specs, out_specs=out_specs, out_shape=out_shape,
        scratch_shapes=[pltpu.VMEM((c, heads * dv), BF16)],
        compiler_params=_params("parallel", "arbitrary"), name="ret_layer",
    )(*args)
    if side is None:
        return outs
    y, s, o_s, s_s = outs
    return y, s, o_s.reshape(bs, ls, heads * dv), s_s


def _row_tile(t):
    for tm in (256, 128, 64, 32, 16, 8):
        if t % tm == 0:
            return tm
    raise ValueError(f"token count {t} is not a multiple of {SUBLANES}")


def _rope_tables(l, half, pos0):
    inv = 1.0 / (ROPE_BASE ** jnp.linspace(0.0, 1.0, half, dtype=F32))
    pos = pos0 + jnp.arange(l, dtype=F32)
    ang = pos[:, None] * inv[None, :]
    return jnp.cos(ang), jnp.sin(ang)


def _beta_decay_weight(w, heads, main):
    zeros = jnp.zeros((w.shape[0], LANES - heads), w.dtype)
    return jnp.concatenate([w[:, main:main + heads], zeros, w[:, main + heads:], zeros],
                           axis=1).astype(BF16)


def kernel(x_prompt, x_sample, state_gdn_ssm, state_gdn_conv, state_ret, norm_w, w_in_a,
           conv_w_a, a_log_a, dt_bias_a, onorm_a, w_out_a, w_in_b, onorm_b, w_out_b,
           final_norm_w):
    assert state_gdn_ssm.shape[0] == 1 and state_ret.shape[0] == 1 and norm_w.shape[0] == 2
    bp, lp, d = x_prompt.shape
    bs, ls, _ = x_sample.shape
    heads_a, dk_a, dv_a = state_gdn_ssm.shape[2:]
    heads_b, dk_b, dv_b = state_ret.shape[2:]
    main_a = 2 * heads_a * dk_a + 2 * heads_a * dv_a
    win_a = w_in_a[0][:, :main_a].astype(BF16)
    wba_a = _beta_decay_weight(w_in_a[0], heads_a, main_a)
    wout_a = w_out_a[0].astype(BF16)
    win_b = w_in_b[0].astype(BF16)
    wout_b = w_out_b[0].astype(BF16)

    ts = bs * ls
    tm = _row_tile(ts)
    xs = x_sample.reshape(ts, d)
    p, ba = _in_proj(xs, norm_w[0], [win_a, wba_a], tm)
    o, sa_s, ca_s = _gdn_core(p.reshape(bs, ls, -1), ba.reshape(bs, ls, -1), state_gdn_ssm[0],
                              state_gdn_conv, conv_w_a[0], a_log_a[0], dt_bias_a[0], onorm_a[0])
    xs = _out_proj(o.reshape(ts, -1), xs, wout_a, final_norm_w, tm, False)
    (p_s,) = _in_proj(xs, norm_w[1], [win_b], tm)
    p_s = p_s.reshape(bs, ls, -1)
    cos_s, sin_s = _rope_tables(ls, dk_b // 2, PAST_LEN)

    x1, sa_p, ca_p = _gdn_layer(x_prompt, norm_w[0], win_a, wba_a, conv_w_a[0], a_log_a[0],
                                dt_bias_a[0], onorm_a[0], wout_a, heads_a)
    cos_p, sin_p = _rope_tables(lp, dk_b // 2, 0.0)
    if bs == RET_SEQS_PER_STEP * bp * (lp // RET_CHUNK):
        y_p, sb_p, o, sb_s = _ret_layer(x1, norm_w[1], win_b, cos_p, sin_p, onorm_b[0], wout_b,
                                        final_norm_w, heads_b,
                                        side=(p_s, cos_s, sin_s, state_ret[0]))
    else:
        y_p, sb_p = _ret_layer(x1, norm_w[1], win_b, cos_p, sin_p, onorm_b[0], wout_b,
                               final_norm_w, heads_b)
        o, sb_s = _ret_core(p_s, cos_s, sin_s, state_ret[0], onorm_b[0])
    y_s = _out_proj(o.reshape(ts, -1), xs, wout_b, final_norm_w, tm, True).reshape(bs, ls, d)
    return (y_p, y_s, sa_p[None], ca_p, sb_p[None], sa_s[None], ca_s, sb_s[None])
```

```python
import functools

import jax
import jax.numpy as jnp
from jax import lax
from jax.experimental import pallas as pl
from jax.experimental.pallas import tpu as pltpu

F32 = jnp.float32
BF16 = jnp.bfloat16
EPS = 1e-6
CHUNK = 64
CONV_W = 4
ROPE_BASE = 10000.0
PAST_LEN = 16384.0
LANES = 128
SUBLANES = 8
HIST0 = SUBLANES - (CONV_W - 1)
VMEM_LIMIT_BYTES = 60 * 1024 * 1024
GDN_CHUNKS_PER_STEP = 4
GDN_BLOCKS_PER_STEP = 2
IN_PROJ_TILE = 256
ROW_STRIDE = 4
RET_CHUNK = 256
RET_SEQS_PER_STEP = 2


def _dot(a, b):
    return jnp.dot(a, b, preferred_element_type=F32)


def _dot_nt(a, b):
    return lax.dot_general(a, b, (((1,), (1,)), ((), ())), preferred_element_type=F32)


def _dot_tn(a, b):
    return lax.dot_general(a, b, (((0,), (0,)), ((), ())), preferred_element_type=F32)


def _split3(x):
    hi = x.astype(BF16)
    r = x - hi.astype(F32)
    mid = r.astype(BF16)
    lo = (r - mid.astype(F32)).astype(BF16)
    return hi, mid, lo


def _sigmoid(x):
    return 1.0 / (1.0 + jnp.exp(-x))


def _silu(x):
    return x * _sigmoid(x)


def _softplus(x):
    return jnp.maximum(x, 0.0) + jnp.log1p(jnp.exp(-jnp.abs(x)))


def _rms_scale(x, w):
    return x * lax.rsqrt(jnp.mean(x * x, axis=-1, keepdims=True) + EPS) * w


def _params(*sem):
    return pltpu.CompilerParams(dimension_semantics=sem, vmem_limit_bytes=VMEM_LIMIT_BYTES)


def _lane_pad(vec):
    return jnp.pad(vec.astype(F32), (0, LANES - vec.shape[0])).reshape(1, LANES)


def _in_proj_kernel(x_ref, nw_ref, *refs):
    n = len(refs) // 2
    xn = _rms_scale(x_ref[...], nw_ref[...]).astype(BF16)
    for w_ref, o_ref in zip(refs[:n], refs[n:]):
        o_ref[...] = _dot(xn, w_ref[...])


def _in_proj(x, norm_w, weights, tm):
    t, d = x.shape
    assert t % tm == 0
    in_specs = [pl.BlockSpec((tm, d), lambda i: (i, 0)),
                pl.BlockSpec((1, d), lambda i: (0, 0))]
    in_specs += [pl.BlockSpec(w.shape, lambda i: (0, 0)) for w in weights]
    return pl.pallas_call(
        _in_proj_kernel, grid=(t // tm,), in_specs=in_specs,
        out_specs=[pl.BlockSpec((tm, w.shape[1]), lambda i: (i, 0)) for w in weights],
        out_shape=[jax.ShapeDtypeStruct((t, w.shape[1]), F32) for w in weights],
        compiler_params=_params("parallel"), name="in_proj",
    )(x, norm_w.reshape(1, d), *weights)


def _out_proj_kernel(o_ref, x_ref, w_ref, fw_ref, y_ref, *, final_norm):
    y = x_ref[...] + _dot(o_ref[...].astype(BF16), w_ref[...])
    y_ref[...] = _rms_scale(y, fw_ref[...]) if final_norm else y


def _out_proj(o, x, w, final_w, tm, final_norm):
    t, k = o.shape
    d = x.shape[1]
    assert t % tm == 0
    return pl.pallas_call(
        functools.partial(_out_proj_kernel, final_norm=final_norm),
        grid=(t // tm,),
        in_specs=[pl.BlockSpec((tm, k), lambda i: (i, 0)),
                  pl.BlockSpec((tm, d), lambda i: (i, 0)),
                  pl.BlockSpec((k, d), lambda i: (0, 0)),
                  pl.BlockSpec((1, d), lambda i: (0, 0))],
        out_specs=pl.BlockSpec((tm, d), lambda i: (i, 0)),
        out_shape=jax.ShapeDtypeStruct((t, d), F32),
        compiler_params=_params("parallel"), name="out_proj",
    )(o, x, w, final_w.reshape(1, d))


def _unit_lower_solves(lows, rhs, eye, cl):
    powers = [(-low).astype(BF16) for low in lows]
    invs = [eye - low for low in lows]
    p = 1
    while 2 * p < cl:
        sq = [_dot(pw, pw) for pw in powers]
        p *= 2
        powers = [s.astype(BF16) for s in sq]
        invs = [inv + _dot(inv.astype(BF16), pw) for inv, pw in zip(invs, powers)]
    return [_dot(inv.astype(BF16), r) for inv, r in zip(invs, rhs)]


def _gdn_step(front, tok, alog_ref, dtb_ref, onw_ref, s_scr, *, heads, dk, dv, nb, cl, nc,
              fill=lambda: None):
    c = nb * cl
    qk_w = heads * dk
    row = lax.broadcasted_iota(jnp.int32, (c, c), 0)
    col = lax.broadcasted_iota(jnp.int32, (c, c), 1)
    shift = cl.bit_length() - 1
    same = lax.shift_right_logical(row, shift) == lax.shift_right_logical(col, shift)
    incl = same & (tok(row) >= tok(col))
    strict = same & (tok(row) > tok(col))
    eye = (row == col).astype(F32)
    cum_masks = jnp.concatenate([incl.astype(BF16), same.astype(BF16)], axis=0)
    sel = (lax.broadcasted_iota(jnp.int32, (SUBLANES, LANES), 0)
           == lax.broadcasted_iota(jnp.int32, (SUBLANES, LANES), 1)).astype(BF16)
    neg_a = -jnp.exp(alog_ref[...])
    units = [(ci, h) for ci in range(nc) for h in range(heads)]
    seqs = range(nb)

    def stack(parts):
        return parts[0] if nb == 1 else jnp.concatenate(parts, axis=0)

    beta, gcum, gtot, gcum_t = [], [], [], []
    for ci in range(nc):
        beta.append(_sigmoid(front.gate(ci, 0)))
        g = neg_a * _softplus(front.gate(ci, 1) + dtb_ref[...])
        g3 = _split3(g)
        gg = _dot(cum_masks, g3[0]) + (_dot(cum_masks, g3[1]) + _dot(cum_masks, g3[2]))
        gcum.append(gg[:c])
        gtot.append(gg[c:])
        gc3 = _split3(gg[:c])
        gcum_t.append(_dot_nt(sel, gc3[0]) + (_dot_nt(sel, gc3[1]) + _dot_nt(sel, gc3[2])))

    q_l, eg_l, lows, rhs_l, qk_l, kd_l = [], [], [], [], [], []
    for ci, h in units:
        q = front.mixed(ci, h * dk, (h + 1) * dk)
        k = front.mixed(ci, qk_w + h * dk, qk_w + (h + 1) * dk)
        v = front.mixed(ci, 2 * qk_w + h * dv, 2 * qk_w + (h + 1) * dv)
        q = q * lax.rsqrt(jnp.sum(q * q, axis=-1, keepdims=True) + EPS) * (dk ** -0.5)
        k = k * lax.rsqrt(jnp.sum(k * k, axis=-1, keepdims=True) + EPS)
        gc = gcum[ci][:, h:h + 1]
        gr = gcum_t[ci][h:h + 1, :]
        bh = beta[ci][:, h:h + 1]
        dec = jnp.where(incl, jnp.exp(jnp.where(incl, gc - gr, 0.0)), 0.0)
        eg = jnp.exp(gc)
        qb = q.astype(BF16)
        kb = k.astype(BF16)
        lows.append(jnp.where(strict, bh * _dot_nt(kb, kb) * dec, 0.0))
        qk_l.append((_dot_nt(qb, kb) * dec).astype(BF16))
        rhs_l.append(jnp.concatenate([bh * v, (bh * eg) * k], axis=-1).astype(BF16))
        kd_l.append(k * jnp.exp(gtot[ci][:, h:h + 1] - gc))
        q_l.append(q)
        eg_l.append(eg)
        fill()
    sols = _unit_lower_solves(lows, rhs_l, eye, cl)

    for ci in range(nc):
        idx = [ci * heads + h for h in range(heads)]
        s_old = [[s_scr[j, h] for j in seqs] for h in range(heads)]
        res = [[_dot(jnp.concatenate([sols[i][j * cl:(j + 1) * cl, dv:],
                                      q_l[i][j * cl:(j + 1) * cl]], axis=0).astype(BF16),
                     s_old[h][j].astype(BF16)) for j in seqs]
               for h, i in enumerate(idx)]
        u = [stack([sols[i][j * cl:(j + 1) * cl, :dv] - res[h][j][:cl] for j in seqs])
             for h, i in enumerate(idx)]
        o = [eg_l[i] * stack([res[h][j][cl:] for j in seqs]) + _dot(qk_l[i], u[h].astype(BF16))
             for h, i in enumerate(idx)]
        for h, i in enumerate(idx):
            for j in seqs:
                r0, r1 = j * cl, (j + 1) * cl
                s_scr[j, h] = (jnp.exp(gtot[ci][r0:r0 + 1, h:h + 1]) * s_old[h][j]
                               + _dot_tn(kd_l[i][r0:r1].astype(BF16), u[h][r0:r1].astype(BF16)))
        for h in range(heads):
            front.put(ci, h, _rms_scale(o[h], onw_ref[...]) * _silu(front.z(ci, h)))


class _RollFront:
    def __init__(self, p_ref, ba_ref, o_ref, xe_scr, cw_ref, *, dv, nb):
        self.p_ref, self.ba_ref, self.o_ref, self.dv = p_ref, ba_ref, o_ref, dv
        self.conv_ch = xe_scr.shape[2]
        conv = []
        for j in range(nb):
            xall = xe_scr[j]
            acc = xall * cw_ref[0:1, :]
            for tap in range(1, CONV_W):
                acc = pltpu.roll(acc, 1, axis=0) + xall * cw_ref[tap:tap + 1, :]
            conv.append(acc[SUBLANES:])
        self.act = _silu(conv[0] if nb == 1 else jnp.concatenate(conv, axis=0))

    def mixed(self, ci, lo, hi):
        return self.act[:, lo:hi]

    def gate(self, ci, k):
        return self.ba_ref[0, :, k * LANES:(k + 1) * LANES]

    def z(self, ci, h):
        return self.p_ref[0, :, self.conv_ch + h * self.dv:self.conv_ch + (h + 1) * self.dv]

    def put(self, ci, h, value):
        self.o_ref[0, :, h * self.dv:(h + 1) * self.dv] = value


def _gdn_core_kernel(p_ref, ba_ref, s0_ref, c0_ref, cw_ref, alog_ref, dtb_ref, onw_ref,
                     o_ref, s_ref, c_ref, s_scr, xe_scr, *, heads, dk, dv, nb, cl):
    conv_ch = xe_scr.shape[2]
    s_scr[...] = s0_ref[...]
    for j in range(nb):
        xe_scr[j, 0:SUBLANES, :] = jnp.zeros((SUBLANES, conv_ch), F32)
        for r in range(CONV_W - 1):
            xe_scr[j, HIST0 + r:HIST0 + r + 1, :] = c0_ref[r, j:j + 1, :]
        xe_scr[j, SUBLANES:SUBLANES + cl, :] = p_ref[0, j * cl:(j + 1) * cl, 0:conv_ch]
    front = _RollFront(p_ref, ba_ref, o_ref, xe_scr, cw_ref, dv=dv, nb=nb)
    _gdn_step(front, lambda r: r & (cl - 1), alog_ref, dtb_ref, onw_ref, s_scr,
              heads=heads, dk=dk, dv=dv, nb=nb, cl=cl, nc=1)
    s_ref[...] = s_scr[...]
    for j in range(nb):
        for r in range(CONV_W - 1):
            c_ref[r, j:j + 1, :] = xe_scr[j, cl + HIST0 + r:cl + HIST0 + r + 1, :]


def _gdn_core(p, ba, s0, c0, conv_w, a_log, dt_bias, onorm_w):
    b, l, pw = p.shape
    heads, dk, dv = s0.shape[1:]
    conv_ch = c0.shape[2]
    assert c0.shape[:2] == (CONV_W - 1, b)
    assert heads <= SUBLANES and dk == LANES and dv == LANES
    assert pw == conv_ch + heads * dv and ba.shape[2] == 2 * LANES
    assert l <= CHUNK and l % SUBLANES == 0 and l & (l - 1) == 0
    nb = max(n for n in range(1, CHUNK // l + 1) if b % n == 0)
    groups, rows = b // nb, nb * l
    kern = functools.partial(_gdn_core_kernel, heads=heads, dk=dk, dv=dv, nb=nb, cl=l)
    o, s, cs = pl.pallas_call(
        kern, grid=(groups, 1),
        in_specs=[pl.BlockSpec((1, rows, pw), lambda i, t: (i, 0, 0)),
                  pl.BlockSpec((1, rows, 2 * LANES), lambda i, t: (i, 0, 0)),
                  pl.BlockSpec((nb, heads, dk, dv), lambda i, t: (i, 0, 0, 0)),
                  pl.BlockSpec((CONV_W - 1, nb, conv_ch), lambda i, t: (0, i, 0)),
                  pl.BlockSpec((CONV_W, conv_ch), lambda i, t: (0, 0)),
                  pl.BlockSpec((1, LANES), lambda i, t: (0, 0)),
                  pl.BlockSpec((1, LANES), lambda i, t: (0, 0)),
                  pl.BlockSpec((1, dv), lambda i, t: (0, 0))],
        out_specs=[pl.BlockSpec((1, rows, heads * dv), lambda i, t: (i, 0, 0)),
                   pl.BlockSpec((nb, heads, dk, dv), lambda i, t: (i, 0, 0, 0)),
                   pl.BlockSpec((CONV_W - 1, nb, conv_ch), lambda i, t: (0, i, 0))],
        out_shape=[jax.ShapeDtypeStruct((groups, rows, heads * dv), F32),
                   jax.ShapeDtypeStruct(s0.shape, F32),
                   jax.ShapeDtypeStruct(c0.shape, F32)],
        scratch_shapes=[pltpu.VMEM((nb, heads, dk, dv), F32),
                        pltpu.VMEM((nb, SUBLANES + l, conv_ch), F32)],
        compiler_params=_params("parallel", "arbitrary"), name="gdn_core",
    )(p.reshape(groups, rows, pw), ba.reshape(groups, rows, 2 * LANES), s0, c0, conv_w,
      _lane_pad(a_log), _lane_pad(dt_bias), onorm_w.reshape(1, dv))
    return o.reshape(b, l, heads * dv), s, cs


class _StridedFront:
    def __init__(self, xe_scr, zb_scr, o_scr, cw_ref, *, heads):
        self.xe, self.zb, self.o, self.cw, self.heads = xe_scr, zb_scr, o_scr, cw_ref, heads

    @staticmethod
    def tok(r):
        span = SUBLANES * ROW_STRIDE
        return ((r & (CHUNK - span)) + (lax.shift_right_logical(r, 3) & (ROW_STRIDE - 1))
                + (r & (SUBLANES - 1)) * ROW_STRIDE)

    @staticmethod
    def _starts(ci):
        return [ci * CHUNK + g * SUBLANES * ROW_STRIDE + s
                for g in range(CHUNK // (SUBLANES * ROW_STRIDE)) for s in range(ROW_STRIDE)]

    def _rows(self, ref, slab, base, ci):
        return jnp.concatenate([ref[slab, pl.ds(base + st, SUBLANES, stride=ROW_STRIDE), :]
                                for st in self._starts(ci)], axis=0)

    def mixed(self, ci, lo, hi):
        assert hi - lo == LANES and lo % LANES == 0
        acc = self._rows(self.xe, lo // LANES, HIST0, ci) * self.cw[0:1, lo:hi]
        for tap in range(1, CONV_W):
            acc = acc + self._rows(self.xe, lo // LANES, HIST0 + tap, ci) * self.cw[tap:tap + 1,
                                                                                   lo:hi]
        return _silu(acc)

    def gate(self, ci, k):
        return self._rows(self.zb, self.heads + k, 0, ci)

    def z(self, ci, h):
        return self._rows(self.zb, h, 0, ci)

    def put(self, ci, h, value):
        for n, st in enumerate(self._starts(ci)):
            self.o[h, pl.ds(st, SUBLANES, stride=ROW_STRIDE), :] = value[n * SUBLANES:
                                                                         (n + 1) * SUBLANES]


def _gdn_layer_kernel(x_ref, xnext_ref, nw_ref, win_ref, wba_ref, cw_ref, alog_ref, dtb_ref,
                      onw_ref, wout_ref, y_ref, s_ref, c_ref, s_scr, xe_scr, zb_scr, o_scr,
                      *, heads, dk, dv, nc):
    rows = nc * CHUNK
    blocks = xe_scr.shape[0]
    n_conv = xe_scr.shape[1]
    t = pl.program_id(1)
    step = pl.program_id(0) * pl.num_programs(1) + t

    def in_proj_tiles(x_rows, blk):
        xn = _rms_scale(x_rows(), nw_ref[...]).astype(BF16)
        for lo in range(0, (n_conv + heads) * LANES, IN_PROJ_TILE):
            r = _dot(xn, win_ref[:, lo:lo + IN_PROJ_TILE])
            for k in range(IN_PROJ_TILE // LANES):
                s = lo // LANES + k
                piece = r[:, k * LANES:(k + 1) * LANES]
                if s < n_conv:
                    xe_scr[blk, s, SUBLANES:SUBLANES + rows, :] = piece
                else:
                    zb_scr[blk, s - n_conv] = piece
            yield
        ba = _dot(xn, wba_ref[...])
        zb_scr[blk, heads] = ba[:, 0:LANES]
        zb_scr[blk, heads + 1] = ba[:, LANES:2 * LANES]
        yield

    def block_rows(blk):
        return lambda: x_ref[0, blk * rows:(blk + 1) * rows, :]

    @pl.when(step == 0)
    def _():
        for _ in in_proj_tiles(block_rows(0), 0):
            pass

    @pl.when(t == 0)
    def _():
        s_scr[...] = jnp.zeros(s_scr.shape, F32)
        for s in range(n_conv):
            xe_scr[0, s, 0:SUBLANES, :] = jnp.zeros((SUBLANES, LANES), F32)

    for blk in range(blocks):
        xe, zb, osc = xe_scr.at[blk], zb_scr.at[blk], o_scr.at[blk]
        if blk + 1 < blocks:
            ahead = in_proj_tiles(block_rows(blk + 1), blk + 1)
        else:
            ahead = in_proj_tiles(lambda: xnext_ref[0], 0)
        front = _StridedFront(xe, zb, osc, cw_ref, heads=heads)
        _gdn_step(front, front.tok, alog_ref, dtb_ref, onw_ref, s_scr, heads=heads, dk=dk,
                  dv=dv, nb=1, cl=CHUNK, nc=nc, fill=lambda: next(ahead, None))
        for _ in ahead:
            pass
        o = jnp.concatenate([osc[h] for h in range(heads)], axis=-1).astype(BF16)
        y_ref[0, blk * rows:(blk + 1) * rows, :] = (x_ref[0, blk * rows:(blk + 1) * rows, :]
                                                    + _dot(o, wout_ref[...]))
        hist = [xe[s, rows + HIST0:rows + SUBLANES, :] for s in range(n_conv)]
        nxt = xe_scr.at[(blk + 1) % blocks]
        for s in range(n_conv):
            nxt[s, HIST0:SUBLANES, :] = hist[s]

    @pl.when(t == pl.num_programs(1) - 1)
    def _():
        s_ref[...] = s_scr[...]
        for s in range(n_conv):
            c_ref[0, 0, :, s * LANES:(s + 1) * LANES] = hist[s]


def _gdn_layer(x, norm_w, w_in, w_ba, conv_w, a_log, dt_bias, onorm_w, w_out, heads):
    b, l, d = x.shape
    dv = w_out.shape[0] // heads
    dk = dv
    conv_ch = conv_w.shape[1]
    pw = w_in.shape[1]
    nc, blocks = GDN_CHUNKS_PER_STEP, GDN_BLOCKS_PER_STEP
    rows = CHUNK * nc
    step_rows = rows * blocks
    steps = l // step_rows
    assert l % step_rows == 0 and heads <= SUBLANES and dk == LANES and dv == LANES
    assert conv_ch == 3 * heads * dk and pw == conv_ch + heads * dv and blocks >= 2
    assert w_ba.shape[1] == 2 * LANES and CHUNK % (SUBLANES * ROW_STRIDE) == 0
    assert pw % IN_PROJ_TILE == 0
    resident = dict(pipeline_mode=pl.Buffered(1))

    def next_first_block(i, t):
        nxt = jnp.minimum(i * steps + t + 1, b * steps - 1)
        return (nxt // steps, (nxt % steps) * blocks, 0)

    kern = functools.partial(_gdn_layer_kernel, heads=heads, dk=dk, dv=dv, nc=nc)
    return pl.pallas_call(
        kern, grid=(b, steps),
        in_specs=[pl.BlockSpec((1, step_rows, d), lambda i, t: (i, t, 0)),
                  pl.BlockSpec((1, rows, d), next_first_block),
                  pl.BlockSpec((1, d), lambda i, t: (0, 0)),
                  pl.BlockSpec((d, pw), lambda i, t: (0, 0), **resident),
                  pl.BlockSpec((d, 2 * LANES), lambda i, t: (0, 0), **resident),
                  pl.BlockSpec((CONV_W, conv_ch), lambda i, t: (0, 0)),
                  pl.BlockSpec((1, LANES), lambda i, t: (0, 0)),
                  pl.BlockSpec((1, LANES), lambda i, t: (0, 0)),
                  pl.BlockSpec((1, dv), lambda i, t: (0, 0)),
                  pl.BlockSpec((heads * dv, d), lambda i, t: (0, 0), **resident)],
        out_specs=[pl.BlockSpec((1, step_rows, d), lambda i, t: (i, t, 0)),
                   pl.BlockSpec((1, heads, dk, dv), lambda i, t: (i, 0, 0, 0)),
                   pl.BlockSpec((1, 1, CONV_W - 1, conv_ch), lambda i, t: (0, i, 0, 0))],
        out_shape=[jax.ShapeDtypeStruct((b, l, d), F32),
                   jax.ShapeDtypeStruct((b, heads, dk, dv), F32),
                   jax.ShapeDtypeStruct((1, b, CONV_W - 1, conv_ch), F32)],
        scratch_shapes=[pltpu.VMEM((1, heads, dk, dv), F32),
                        pltpu.VMEM((blocks, conv_ch // LANES, SUBLANES + rows, LANES), F32),
                        pltpu.VMEM((blocks, heads + 2, rows, LANES), F32),
                        pltpu.VMEM((blocks, heads, rows, LANES), F32)],
        compiler_params=_params("arbitrary", "arbitrary"), name="gdn_layer",
    )(x, x, norm_w.reshape(1, d), w_in, w_ba, conv_w, _lane_pad(a_log), _lane_pad(dt_bias),
      onorm_w.reshape(1, dv), w_out)


def _ret_chunk(cols, put_out, cos_ref, sin_ref, s0_ref, onw_ref, s_ref,
               *, heads, dk, dv, nb, c, first_step):
    qk_w = heads * dk
    g_off = 2 * qk_w + heads * dv
    half = dk // 2

    if first_step is not None:
        @pl.when(first_step)
        def _():
            s_ref[...] = jnp.zeros(s_ref.shape, F32) if s0_ref is None else s0_ref[...]
    state_ref = s0_ref if first_step is None else s_ref

    cos = cos_ref[...]
    sin = sin_ref[...]
    row = lax.broadcasted_iota(jnp.int32, (c, c), 0)
    col = lax.broadcasted_iota(jnp.int32, (c, c), 1)
    diff = (row - col).astype(F32)
    idx = lax.broadcasted_iota(jnp.int32, (c, 1), 0).astype(F32)

    def rotary(x):
        x1, x2 = x[:, :half], x[:, half:]
        return jnp.concatenate([x1 * cos - x2 * sin, x1 * sin + x2 * cos], axis=-1)

    decay, q_dec, k_dec, s_dec = [], [], [], []
    for h in range(heads):
        lg = jnp.log(jnp.full((1, 1), 1.0 - 2.0 ** (-5.0 - h), F32))
        decay.append(jnp.where(diff >= 0, jnp.exp(lg * jnp.maximum(diff, 0.0)), 0.0))
        q_dec.append(jnp.exp(lg * (idx + 1.0)))
        k_dec.append(jnp.exp(lg * (c - 1.0 - idx)))
        s_dec.append(jnp.exp(lg * c))

    units = [(j, h) for j in range(nb) for h in range(heads)]
    qb, kb, kdb, vb = [], [], [], []
    for j, h in units:
        q = rotary(cols(j, h * dk, (h + 1) * dk))
        k = rotary(cols(j, qk_w + h * dk, qk_w + (h + 1) * dk)) * (dk ** -0.5)
        qb.append(q.astype(BF16))
        kb.append(k.astype(BF16))
        kdb.append((k * k_dec[h]).astype(BF16))
        vb.append(cols(j, 2 * qk_w + h * dv, 2 * qk_w + (h + 1) * dv).astype(BF16))
    qk = [(_dot_nt(qb[i], kb[i]) * decay[h]).astype(BF16) for i, (j, h) in enumerate(units)]
    s_old = [state_ref[j, h] for j, h in units]
    qs = [_dot(qb[i], s_old[i].astype(BF16)) for i in range(len(units))]
    o = [_dot(qk[i], vb[i]) + q_dec[h] * qs[i] for i, (j, h) in enumerate(units)]
    for i, (j, h) in enumerate(units):
        s_ref[j, h] = s_dec[h] * s_old[i] + _dot_tn(kdb[i], vb[i])
    for i, (j, h) in enumerate(units):
        put_out(j, h, _rms_scale(o[i], onw_ref[h:h + 1, :])
                * _silu(cols(j, g_off + h * dv, g_off + (h + 1) * dv)))


def _ret_side_chunk(ps_ref, cos_ref, sin_ref, s0_ref, onw_ref, o_ref, s_ref, *, heads, dk, dv,
                    nb):
    c = ps_ref.shape[1] // nb

    def cols(j, lo, hi):
        return ps_ref[0, j * c:(j + 1) * c, lo:hi]

    def put_out(j, h, value):
        o_ref[0, j * c:(j + 1) * c, h * dv:(h + 1) * dv] = value

    _ret_chunk(cols, put_out, cos_ref, sin_ref, s0_ref, onw_ref, s_ref,
               heads=heads, dk=dk, dv=dv, nb=nb, c=c, first_step=None)


def _ret_core_kernel(ps_ref, cos_ref, sin_ref, s0_ref, onw_ref, o_ref, s_ref,
                     *, heads, dk, dv, nb):
    _ret_side_chunk(ps_ref, cos_ref, sin_ref, s0_ref, onw_ref, o_ref, s_ref,
                    heads=heads, dk=dk, dv=dv, nb=nb)


def _ret_side_specs(p_s, s0_s, nbs, group):
    bs, ls, pw = p_s.shape
    heads, dk, dv = s0_s.shape[1:]
    assert bs % nbs == 0 and ls <= CHUNK and ls % SUBLANES == 0 and dk // 2 == LANES
    assert pw == 2 * heads * dk + 2 * heads * dv
    in_specs = [pl.BlockSpec((1, nbs * ls, pw), lambda *g: (group(*g), 0, 0)),
                pl.BlockSpec((ls, dk // 2), lambda *g: (0, 0)),
                pl.BlockSpec((ls, dk // 2), lambda *g: (0, 0)),
                pl.BlockSpec((nbs, heads, dk, dv), lambda *g: (group(*g), 0, 0, 0))]
    out_specs = [pl.BlockSpec((1, nbs * ls, heads * dv), lambda *g: (group(*g), 0, 0)),
                 pl.BlockSpec((nbs, heads, dk, dv), lambda *g: (group(*g), 0, 0, 0))]
    out_shape = [jax.ShapeDtypeStruct((bs // nbs, nbs * ls, heads * dv), F32),
                 jax.ShapeDtypeStruct(s0_s.shape, F32)]
    return in_specs, out_specs, out_shape


def _ret_core(p_s, cos_s, sin_s, s0_s, onorm_w):
    bs, ls, pw = p_s.shape
    heads, dk, dv = s0_s.shape[1:]
    nbs = RET_SEQS_PER_STEP if bs % RET_SEQS_PER_STEP == 0 else 1
    in_specs, out_specs, out_shape = _ret_side_specs(p_s, s0_s, nbs, lambda i: i)
    in_specs.append(pl.BlockSpec((heads, dv), lambda i: (0, 0)))
    o, s = pl.pallas_call(
        functools.partial(_ret_core_kernel, heads=heads, dk=dk, dv=dv, nb=nbs),
        grid=(bs // nbs,), in_specs=in_specs, out_specs=out_specs, out_shape=out_shape,
        compiler_params=_params("parallel"), name="ret_core",
    )(p_s.reshape(bs // nbs, nbs * ls, pw), cos_s, sin_s, s0_s, onorm_w)
    return o.reshape(bs, ls, heads * dv), s


def _ret_layer_kernel(*refs, heads, dk, dv, c, side_seqs):
    x_ref, nw_ref, win_ref, cos_ref, sin_ref, onw_ref, wout_ref, fw_ref = refs[:8]
    if side_seqs:
        ps_ref, cos_s_ref, sin_s_ref, s0s_ref, y_ref, s_ref, os_ref, ss_ref, o_scr = refs[8:]
    else:
        y_ref, s_ref, o_scr = refs[8:]

    x = x_ref[0]
    p = _dot(_rms_scale(x, nw_ref[...]).astype(BF16), win_ref[...])

    def cols(j, lo, hi):
        return p[:, lo:hi]

    def put_out(j, h, value):
        o_scr[:, h * dv:(h + 1) * dv] = value.astype(BF16)

    _ret_chunk(cols, put_out, cos_ref, sin_ref, None, onw_ref, s_ref,
               heads=heads, dk=dk, dv=dv, nb=1, c=c, first_step=pl.program_id(1) == 0)
    if side_seqs:
        _ret_side_chunk(ps_ref, cos_s_ref, sin_s_ref, s0s_ref, onw_ref, os_ref, ss_ref,
                        heads=heads, dk=dk, dv=dv, nb=side_seqs)
    y_ref[0] = _rms_scale(x + _dot(o_scr[...], wout_ref[...]), fw_ref[...])


def _ret_layer(x, norm_w, w_in, cos, sin, onorm_w, w_out, final_w, heads, side=None):
    b, l, d = x.shape
    pw = w_in.shape[1]
    dv = w_out.shape[0] // heads
    dk = (pw - 2 * heads * dv) // (2 * heads)
    c = RET_CHUNK
    steps = l // c
    assert l % c == 0 and dk // 2 == LANES
    resident = dict(pipeline_mode=pl.Buffered(1))
    in_specs = [pl.BlockSpec((1, c, d), lambda i, t: (i, t, 0)),
                pl.BlockSpec((1, d), lambda i, t: (0, 0)),
                pl.BlockSpec((d, pw), lambda i, t: (0, 0), **resident),
                pl.BlockSpec((c, dk // 2), lambda i, t: (t, 0)),
                pl.BlockSpec((c, dk // 2), lambda i, t: (t, 0)),
                pl.BlockSpec((heads, dv), lambda i, t: (0, 0)),
                pl.BlockSpec((heads * dv, d), lambda i, t: (0, 0), **resident),
                pl.BlockSpec((1, d), lambda i, t: (0, 0))]
    out_specs = [pl.BlockSpec((1, c, d), lambda i, t: (i, t, 0)),
                 pl.BlockSpec((1, heads, dk, dv), lambda i, t: (i, 0, 0, 0))]
    out_shape = [jax.ShapeDtypeStruct((b, l, d), F32),
                 jax.ShapeDtypeStruct((b, heads, dk, dv), F32)]
    args = [x, norm_w.reshape(1, d), w_in, cos, sin, onorm_w, w_out, final_w.reshape(1, d)]
    nbs = 0
    if side is not None:
        p_s, cos_s, sin_s, s0_s = side
        bs, ls, _ = p_s.shape
        nbs = RET_SEQS_PER_STEP
        assert bs == nbs * b * steps and p_s.shape[2] == pw
        assert s0_s.shape == (bs, heads, dk, dv)
        side_in, side_out, side_shape = _ret_side_specs(p_s, s0_s, nbs,
                                                        lambda i, t: i * steps + t)
        in_specs += side_in
        out_specs += side_out
        out_shape += side_shape
        args += [p_s.reshape(bs // nbs, nbs * ls, pw), cos_s, sin_s, s0_s]
    kern = functools.partial(_ret_layer_kernel, heads=heads, dk=dk, dv=dv, c=c, side_seqs=nbs)
    outs = pl.pallas_call(
        kern, grid=(b, steps), in_specs=in_specs, out_specs=out_specs, out_shape=out_shape,
        scratch_shapes=[pltpu.VMEM((c, heads * dv), BF16)],
        compiler_params=_params("parallel", "arbitrary"), name="ret_layer",
    )(*args)
    if side is None:
        return outs
    y, s, o_s, s_s = outs
    return y, s, o_s.reshape(bs, ls, heads * dv), s_s


def _row_tile(t):
    for tm in (256, 128, 64, 32, 16, 8):
        if t % tm == 0:
            return tm
    raise ValueError(f"token count {t} is not a multiple of {SUBLANES}")


def _rope_tables(l, half, pos0):
    inv = 1.0 / (ROPE_BASE ** jnp.linspace(0.0, 1.0, half, dtype=F32))
    pos = pos0 + jnp.arange(l, dtype=F32)
    ang = pos[:, None] * inv[None, :]
    return jnp.cos(ang), jnp.sin(ang)


def _beta_decay_weight(w, heads, main):
    zeros = jnp.zeros((w.shape[0], LANES - heads), w.dtype)
    return jnp.concatenate([w[:, main:main + heads], zeros, w[:, main + heads:], zeros],
                           axis=1).astype(BF16)


def kernel(x_prompt, x_sample, state_gdn_ssm, state_gdn_conv, state_ret, norm_w, w_in_a,
           conv_w_a, a_log_a, dt_bias_a, onorm_a, w_out_a, w_in_b, onorm_b, w_out_b,
           final_norm_w):
    assert state_gdn_ssm.shape[0] == 1 and state_ret.shape[0] == 1 and norm_w.shape[0] == 2
    bp, lp, d = x_prompt.shape
    bs, ls, _ = x_sample.shape
    heads_a, dk_a, dv_a = state_gdn_ssm.shape[2:]
    heads_b, dk_b, dv_b = state_ret.shape[2:]
    main_a = 2 * heads_a * dk_a + 2 * heads_a * dv_a
    win_a = w_in_a[0][:, :main_a].astype(BF16)
    wba_a = _beta_decay_weight(w_in_a[0], heads_a, main_a)
    wout_a = w_out_a[0].astype(BF16)
    win_b = w_in_b[0].astype(BF16)
    wout_b = w_out_b[0].astype(BF16)

    ts = bs * ls
    tm = _row_tile(ts)
    xs = x_sample.reshape(ts, d)
    p, ba = _in_proj(xs, norm_w[0], [win_a, wba_a], tm)
    o, sa_s, ca_s = _gdn_core(p.reshape(bs, ls, -1), ba.reshape(bs, ls, -1), state_gdn_ssm[0],
                              jnp.swapaxes(state_gdn_conv[0], 0, 1), conv_w_a[0], a_log_a[0],
                              dt_bias_a[0], onorm_a[0])
    xs = _out_proj(o.reshape(ts, -1), xs, wout_a, final_norm_w, tm, False)
    (p_s,) = _in_proj(xs, norm_w[1], [win_b], tm)
    p_s = p_s.reshape(bs, ls, -1)
    cos_s, sin_s = _rope_tables(ls, dk_b // 2, PAST_LEN)

    x1, sa_p, ca_p = _gdn_layer(x_prompt, norm_w[0], win_a, wba_a, conv_w_a[0], a_log_a[0],
                                dt_bias_a[0], onorm_a[0], wout_a, heads_a)
    cos_p, sin_p = _rope_tables(lp, dk_b // 2, 0.0)
    if bs == RET_SEQS_PER_STEP * bp * (lp // RET_CHUNK):
        y_p, sb_p, o, sb_s = _ret_layer(x1, norm_w[1], win_b, cos_p, sin_p, onorm_b[0], wout_b,
                                        final_norm_w, heads_b,
                                        side=(p_s, cos_s, sin_s, state_ret[0]))
    else:
        y_p, sb_p = _ret_layer(x1, norm_w[1], win_b, cos_p, sin_p, onorm_b[0], wout_b,
                               final_norm_w, heads_b)
        o, sb_s = _ret_core(p_s, cos_s, sin_s, state_ret[0], onorm_b[0])
    y_s = _out_proj(o.reshape(ts, -1), xs, wout_b, final_norm_w, tm, True).reshape(bs, ls, d)
    ca_s = jnp.swapaxes(ca_s, 0, 1)[None]
    return (y_p, y_s, sa_p[None], ca_p, sb_p[None], sa_s[None], ca_s, sb_s[None])
```

```python
import functools

import jax
import jax.numpy as jnp
from jax import lax
from jax.experimental import pallas as pl
from jax.experimental.pallas import tpu as pltpu

F32 = jnp.float32
BF16 = jnp.bfloat16
EPS = 1e-6
CHUNK = 64
CONV_W = 4
ROPE_BASE = 10000.0
PAST_LEN = 16384.0
LANES = 128
SUBLANES = 8
HIST0 = SUBLANES - (CONV_W - 1)
VMEM_LIMIT_BYTES = 60 * 1024 * 1024
GDN_CHUNKS_PER_STEP = 4
GDN_BLOCKS_PER_STEP = 2
IN_PROJ_TILE = 256
ROW_STRIDE = 4
RET_CHUNK = 256
RET_SEQS_PER_STEP = 2


def _dot(a, b):
    return jnp.dot(a, b, preferred_element_type=F32)


def _dot_nt(a, b):
    return lax.dot_general(a, b, (((1,), (1,)), ((), ())), preferred_element_type=F32)


def _dot_tn(a, b):
    return lax.dot_general(a, b, (((0,), (0,)), ((), ())), preferred_element_type=F32)


def _split3(x):
    hi = x.astype(BF16)
    r = x - hi.astype(F32)
    mid = r.astype(BF16)
    lo = (r - mid.astype(F32)).astype(BF16)
    return hi, mid, lo


def _sigmoid(x):
    return 1.0 / (1.0 + jnp.exp(-x))


def _silu(x):
    return x * _sigmoid(x)


def _softplus(x):
    return jnp.maximum(x, 0.0) + jnp.log1p(jnp.exp(-jnp.abs(x)))


def _rms_scale(x, w):
    return x * lax.rsqrt(jnp.mean(x * x, axis=-1, keepdims=True) + EPS) * w


def _params(*sem):
    return pltpu.CompilerParams(dimension_semantics=sem, vmem_limit_bytes=VMEM_LIMIT_BYTES)


def _lane_pad(vec):
    return jnp.pad(vec.astype(F32), (0, LANES - vec.shape[0])).reshape(1, LANES)


def _in_proj_kernel(x_ref, nw_ref, *refs):
    n = len(refs) // 2
    xn = _rms_scale(x_ref[...], nw_ref[...]).astype(BF16)
    for w_ref, o_ref in zip(refs[:n], refs[n:]):
        o_ref[...] = _dot(xn, w_ref[...])


def _in_proj(x, norm_w, weights, tm):
    t, d = x.shape
    assert t % tm == 0 and all(n % LANES == 0 for _, n in weights)
    in_specs = [pl.BlockSpec((tm, d), lambda i: (i, 0)),
                pl.BlockSpec((1, d), lambda i: (0, 0))]
    in_specs += [pl.BlockSpec((d, n), lambda i: (0, 0)) for _, n in weights]
    return pl.pallas_call(
        _in_proj_kernel, grid=(t // tm,), in_specs=in_specs,
        out_specs=[pl.BlockSpec((tm, n), lambda i: (i, 0)) for _, n in weights],
        out_shape=[jax.ShapeDtypeStruct((t, n), F32) for _, n in weights],
        compiler_params=_params("parallel"), name="in_proj",
    )(x, norm_w.reshape(1, d), *[w for w, _ in weights])


def _out_proj_kernel(o_ref, x_ref, w_ref, fw_ref, y_ref, *, final_norm):
    y = x_ref[...] + _dot(o_ref[...].astype(BF16), w_ref[...])
    y_ref[...] = _rms_scale(y, fw_ref[...]) if final_norm else y


def _out_proj(o, x, w, final_w, tm, final_norm):
    t, k = o.shape
    d = x.shape[1]
    assert t % tm == 0
    return pl.pallas_call(
        functools.partial(_out_proj_kernel, final_norm=final_norm),
        grid=(t // tm,),
        in_specs=[pl.BlockSpec((tm, k), lambda i: (i, 0)),
                  pl.BlockSpec((tm, d), lambda i: (i, 0)),
                  pl.BlockSpec((k, d), lambda i: (0, 0)),
                  pl.BlockSpec((1, d), lambda i: (0, 0))],
        out_specs=pl.BlockSpec((tm, d), lambda i: (i, 0)),
        out_shape=jax.ShapeDtypeStruct((t, d), F32),
        compiler_params=_params("parallel"), name="out_proj",
    )(o, x, w, final_w.reshape(1, d))


def _unit_lower_solves(lows, rhs, eye, cl):
    powers = [(-low).astype(BF16) for low in lows]
    invs = [eye - low for low in lows]
    p = 1
    while 2 * p < cl:
        sq = [_dot(pw, pw) for pw in powers]
        p *= 2
        powers = [s.astype(BF16) for s in sq]
        invs = [inv + _dot(inv.astype(BF16), pw) for inv, pw in zip(invs, powers)]
    return [_dot(inv.astype(BF16), r) for inv, r in zip(invs, rhs)]


def _gdn_step(front, tok, alog_ref, dtb_ref, onw_ref, s_scr, *, heads, dk, dv, nb, cl, nc,
              fill=lambda: None):
    c = nb * cl
    qk_w = heads * dk
    row = lax.broadcasted_iota(jnp.int32, (c, c), 0)
    col = lax.broadcasted_iota(jnp.int32, (c, c), 1)
    shift = cl.bit_length() - 1
    same = lax.shift_right_logical(row, shift) == lax.shift_right_logical(col, shift)
    incl = same & (tok(row) >= tok(col))
    strict = same & (tok(row) > tok(col))
    eye = (row == col).astype(F32)
    cum_masks = jnp.concatenate([incl.astype(BF16), same.astype(BF16)], axis=0)
    sel = (lax.broadcasted_iota(jnp.int32, (SUBLANES, LANES), 0)
           == lax.broadcasted_iota(jnp.int32, (SUBLANES, LANES), 1)).astype(BF16)
    neg_a = -jnp.exp(alog_ref[...])
    units = [(ci, h) for ci in range(nc) for h in range(heads)]
    seqs = range(nb)

    def stack(parts):
        return parts[0] if nb == 1 else jnp.concatenate(parts, axis=0)

    beta, gcum, gtot, gcum_t = [], [], [], []
    for ci in range(nc):
        beta.append(_sigmoid(front.gate(ci, 0)))
        g = neg_a * _softplus(front.gate(ci, 1) + dtb_ref[...])
        g3 = _split3(g)
        gg = _dot(cum_masks, g3[0]) + (_dot(cum_masks, g3[1]) + _dot(cum_masks, g3[2]))
        gcum.append(gg[:c])
        gtot.append(gg[c:])
        gc3 = _split3(gg[:c])
        gcum_t.append(_dot_nt(sel, gc3[0]) + (_dot_nt(sel, gc3[1]) + _dot_nt(sel, gc3[2])))

    q_l, eg_l, lows, rhs_l, qk_l, kd_l = [], [], [], [], [], []
    for ci, h in units:
        q = front.mixed(ci, h * dk, (h + 1) * dk)
        k = front.mixed(ci, qk_w + h * dk, qk_w + (h + 1) * dk)
        v = front.mixed(ci, 2 * qk_w + h * dv, 2 * qk_w + (h + 1) * dv)
        q = q * lax.rsqrt(jnp.sum(q * q, axis=-1, keepdims=True) + EPS) * (dk ** -0.5)
        k = k * lax.rsqrt(jnp.sum(k * k, axis=-1, keepdims=True) + EPS)
        gc = gcum[ci][:, h:h + 1]
        gr = gcum_t[ci][h:h + 1, :]
        bh = beta[ci][:, h:h + 1]
        dec = jnp.where(incl, jnp.exp(jnp.where(incl, gc - gr, 0.0)), 0.0)
        eg = jnp.exp(gc)
        qb = q.astype(BF16)
        kb = k.astype(BF16)
        lows.append(jnp.where(strict, bh * _dot_nt(kb, kb) * dec, 0.0))
        qk_l.append((_dot_nt(qb, kb) * dec).astype(BF16))
        rhs_l.append(jnp.concatenate([bh * v, (bh * eg) * k], axis=-1).astype(BF16))
        kd_l.append(k * jnp.exp(gtot[ci][:, h:h + 1] - gc))
        q_l.append(q)
        eg_l.append(eg)
        fill()
    sols = _unit_lower_solves(lows, rhs_l, eye, cl)

    for ci in range(nc):
        idx = [ci * heads + h for h in range(heads)]
        s_old = [[s_scr[j, h] for j in seqs] for h in range(heads)]
        res = [[_dot(jnp.concatenate([sols[i][j * cl:(j + 1) * cl, dv:],
                                      q_l[i][j * cl:(j + 1) * cl]], axis=0).astype(BF16),
                     s_old[h][j].astype(BF16)) for j in seqs]
               for h, i in enumerate(idx)]
        u = [stack([sols[i][j * cl:(j + 1) * cl, :dv] - res[h][j][:cl] for j in seqs])
             for h, i in enumerate(idx)]
        o = [eg_l[i] * stack([res[h][j][cl:] for j in seqs]) + _dot(qk_l[i], u[h].astype(BF16))
             for h, i in enumerate(idx)]
        for h, i in enumerate(idx):
            for j in seqs:
                r0, r1 = j * cl, (j + 1) * cl
                s_scr[j, h] = (jnp.exp(gtot[ci][r0:r0 + 1, h:h + 1]) * s_old[h][j]
                               + _dot_tn(kd_l[i][r0:r1].astype(BF16), u[h][r0:r1].astype(BF16)))
        for h in range(heads):
            front.put(ci, h, _rms_scale(o[h], onw_ref[...]) * _silu(front.z(ci, h)))


class _RollFront:
    def __init__(self, p_ref, ba_ref, o_ref, xe_scr, cw_ref, *, dv, nb):
        self.p_ref, self.ba_ref, self.o_ref, self.dv = p_ref, ba_ref, o_ref, dv
        self.conv_ch = xe_scr.shape[2]
        conv = []
        for j in range(nb):
            xall = xe_scr[j]
            acc = xall * cw_ref[0:1, :]
            for tap in range(1, CONV_W):
                acc = pltpu.roll(acc, 1, axis=0) + xall * cw_ref[tap:tap + 1, :]
            conv.append(acc[SUBLANES:])
        self.act = _silu(conv[0] if nb == 1 else jnp.concatenate(conv, axis=0))

    def mixed(self, ci, lo, hi):
        return self.act[:, lo:hi]

    def gate(self, ci, k):
        return self.ba_ref[0, :, k * LANES:(k + 1) * LANES]

    def z(self, ci, h):
        return self.p_ref[0, :, self.conv_ch + h * self.dv:self.conv_ch + (h + 1) * self.dv]

    def put(self, ci, h, value):
        self.o_ref[0, :, h * self.dv:(h + 1) * self.dv] = value


def _gdn_core_kernel(p_ref, ba_ref, s0_ref, c0_ref, cw_ref, alog_ref, dtb_ref, onw_ref,
                     o_ref, s_ref, c_ref, s_scr, xe_scr, *, heads, dk, dv, nb, cl):
    conv_ch = xe_scr.shape[2]
    s_scr[...] = s0_ref[...]
    for j in range(nb):
        xe_scr[j, 0:SUBLANES, :] = jnp.zeros((SUBLANES, conv_ch), F32)
        for r in range(CONV_W - 1):
            xe_scr[j, HIST0 + r:HIST0 + r + 1, :] = c0_ref[r, j:j + 1, :]
        xe_scr[j, SUBLANES:SUBLANES + cl, :] = p_ref[0, j * cl:(j + 1) * cl, 0:conv_ch]
    front = _RollFront(p_ref, ba_ref, o_ref, xe_scr, cw_ref, dv=dv, nb=nb)
    _gdn_step(front, lambda r: r & (cl - 1), alog_ref, dtb_ref, onw_ref, s_scr,
              heads=heads, dk=dk, dv=dv, nb=nb, cl=cl, nc=1)
    s_ref[...] = s_scr[...]
    for j in range(nb):
        for r in range(CONV_W - 1):
            c_ref[r, j:j + 1, :] = xe_scr[j, cl + HIST0 + r:cl + HIST0 + r + 1, :]


def _gdn_core(p, ba, s0, c0, conv_w, a_log, dt_bias, onorm_w):
    b, l, pw = p.shape
    heads, dk, dv = s0.shape[1:]
    conv_ch = c0.shape[2]
    assert c0.shape[:2] == (CONV_W - 1, b)
    assert heads <= SUBLANES and dk == LANES and dv == LANES
    assert pw == conv_ch + heads * dv and ba.shape[2] == 2 * LANES
    assert l <= CHUNK and l % SUBLANES == 0 and l & (l - 1) == 0
    nb = max(n for n in range(1, CHUNK // l + 1) if b % n == 0)
    groups, rows = b // nb, nb * l
    kern = functools.partial(_gdn_core_kernel, heads=heads, dk=dk, dv=dv, nb=nb, cl=l)
    o, s, cs = pl.pallas_call(
        kern, grid=(groups, 1),
        in_specs=[pl.BlockSpec((1, rows, pw), lambda i, t: (i, 0, 0)),
                  pl.BlockSpec((1, rows, 2 * LANES), lambda i, t: (i, 0, 0)),
                  pl.BlockSpec((nb, heads, dk, dv), lambda i, t: (i, 0, 0, 0)),
                  pl.BlockSpec((CONV_W - 1, nb, conv_ch), lambda i, t: (0, i, 0)),
                  pl.BlockSpec((CONV_W, conv_ch), lambda i, t: (0, 0)),
                  pl.BlockSpec((1, LANES), lambda i, t: (0, 0)),
                  pl.BlockSpec((1, LANES), lambda i, t: (0, 0)),
                  pl.BlockSpec((1, dv), lambda i, t: (0, 0))],
        out_specs=[pl.BlockSpec((1, rows, heads * dv), lambda i, t: (i, 0, 0)),
                   pl.BlockSpec((nb, heads, dk, dv), lambda i, t: (i, 0, 0, 0)),
                   pl.BlockSpec((CONV_W - 1, nb, conv_ch), lambda i, t: (0, i, 0))],
        out_shape=[jax.ShapeDtypeStruct((groups, rows, heads * dv), F32),
                   jax.ShapeDtypeStruct(s0.shape, F32),
                   jax.ShapeDtypeStruct(c0.shape, F32)],
        scratch_shapes=[pltpu.VMEM((nb, heads, dk, dv), F32),
                        pltpu.VMEM((nb, SUBLANES + l, conv_ch), F32)],
        compiler_params=_params("parallel", "arbitrary"), name="gdn_core",
    )(p.reshape(groups, rows, pw), ba.reshape(groups, rows, 2 * LANES), s0, c0, conv_w,
      _lane_pad(a_log), _lane_pad(dt_bias), onorm_w.reshape(1, dv))
    return o.reshape(b, l, heads * dv), s, cs


class _StridedFront:
    def __init__(self, xe_scr, zb_scr, o_scr, cw_ref, *, heads):
        self.xe, self.zb, self.o, self.cw, self.heads = xe_scr, zb_scr, o_scr, cw_ref, heads

    @staticmethod
    def tok(r):
        span = SUBLANES * ROW_STRIDE
        return ((r & (CHUNK - span)) + (lax.shift_right_logical(r, 3) & (ROW_STRIDE - 1))
                + (r & (SUBLANES - 1)) * ROW_STRIDE)

    @staticmethod
    def _starts(ci):
        return [ci * CHUNK + g * SUBLANES * ROW_STRIDE + s
                for g in range(CHUNK // (SUBLANES * ROW_STRIDE)) for s in range(ROW_STRIDE)]

    def _rows(self, ref, slab, base, ci):
        return jnp.concatenate([ref[slab, pl.ds(base + st, SUBLANES, stride=ROW_STRIDE), :]
                                for st in self._starts(ci)], axis=0)

    def mixed(self, ci, lo, hi):
        assert hi - lo == LANES and lo % LANES == 0
        acc = self._rows(self.xe, lo // LANES, HIST0, ci) * self.cw[0:1, lo:hi]
        for tap in range(1, CONV_W):
            acc = acc + self._rows(self.xe, lo // LANES, HIST0 + tap, ci) * self.cw[tap:tap + 1,
                                                                                   lo:hi]
        return _silu(acc)

    def gate(self, ci, k):
        return self._rows(self.zb, self.heads + k, 0, ci)

    def z(self, ci, h):
        return self._rows(self.zb, h, 0, ci)

    def put(self, ci, h, value):
        for n, st in enumerate(self._starts(ci)):
            self.o[h, pl.ds(st, SUBLANES, stride=ROW_STRIDE), :] = value[n * SUBLANES:
                                                                         (n + 1) * SUBLANES]


def _gdn_layer_kernel(x_ref, xnext_ref, nw_ref, win_ref, wba_ref, cw_ref, alog_ref, dtb_ref,
                      onw_ref, wout_ref, y_ref, s_ref, c_ref, s_scr, xe_scr, zb_scr, o_scr,
                      *, heads, dk, dv, nc):
    rows = nc * CHUNK
    blocks = xe_scr.shape[0]
    n_conv = xe_scr.shape[1]
    t = pl.program_id(1)
    step = pl.program_id(0) * pl.num_programs(1) + t

    def in_proj_tiles(x_rows, blk):
        xn = _rms_scale(x_rows(), nw_ref[...]).astype(BF16)
        for lo in range(0, (n_conv + heads) * LANES, IN_PROJ_TILE):
            r = _dot(xn, win_ref[:, lo:lo + IN_PROJ_TILE])
            for k in range(IN_PROJ_TILE // LANES):
                s = lo // LANES + k
                piece = r[:, k * LANES:(k + 1) * LANES]
                if s < n_conv:
                    xe_scr[blk, s, SUBLANES:SUBLANES + rows, :] = piece
                else:
                    zb_scr[blk, s - n_conv] = piece
            yield
        ba = _dot(xn, wba_ref[...])
        zb_scr[blk, heads] = ba[:, 0:LANES]
        zb_scr[blk, heads + 1] = ba[:, LANES:2 * LANES]
        yield

    def block_rows(blk):
        return lambda: x_ref[0, blk * rows:(blk + 1) * rows, :]

    @pl.when(step == 0)
    def _():
        for _ in in_proj_tiles(block_rows(0), 0):
            pass

    @pl.when(t == 0)
    def _():
        s_scr[...] = jnp.zeros(s_scr.shape, F32)
        for s in range(n_conv):
            xe_scr[0, s, 0:SUBLANES, :] = jnp.zeros((SUBLANES, LANES), F32)

    for blk in range(blocks):
        xe, zb, osc = xe_scr.at[blk], zb_scr.at[blk], o_scr.at[blk]
        if blk + 1 < blocks:
            ahead = in_proj_tiles(block_rows(blk + 1), blk + 1)
        else:
            ahead = in_proj_tiles(lambda: xnext_ref[0], 0)
        front = _StridedFront(xe, zb, osc, cw_ref, heads=heads)
        _gdn_step(front, front.tok, alog_ref, dtb_ref, onw_ref, s_scr, heads=heads, dk=dk,
                  dv=dv, nb=1, cl=CHUNK, nc=nc, fill=lambda: next(ahead, None))
        for _ in ahead:
            pass
        o = jnp.concatenate([osc[h] for h in range(heads)], axis=-1).astype(BF16)
        y_ref[0, blk * rows:(blk + 1) * rows, :] = (x_ref[0, blk * rows:(blk + 1) * rows, :]
                                                    + _dot(o, wout_ref[...]))
        hist = [xe[s, rows + HIST0:rows + SUBLANES, :] for s in range(n_conv)]
        nxt = xe_scr.at[(blk + 1) % blocks]
        for s in range(n_conv):
            nxt[s, HIST0:SUBLANES, :] = hist[s]

    @pl.when(t == pl.num_programs(1) - 1)
    def _():
        s_ref[...] = s_scr[...]
        for s in range(n_conv):
            c_ref[0, 0, :, s * LANES:(s + 1) * LANES] = hist[s]


def _gdn_layer(x, norm_w, w_in, w_ba, conv_w, a_log, dt_bias, onorm_w, w_out, heads):
    b, l, d = x.shape
    dv = w_out.shape[0] // heads
    dk = dv
    conv_ch = conv_w.shape[1]
    pw = conv_ch + heads * dv
    nc, blocks = GDN_CHUNKS_PER_STEP, GDN_BLOCKS_PER_STEP
    rows = CHUNK * nc
    step_rows = rows * blocks
    steps = l // step_rows
    assert l % step_rows == 0 and heads <= SUBLANES and dk == LANES and dv == LANES
    assert conv_ch == 3 * heads * dk and w_in.shape[1] >= pw and blocks >= 2
    assert w_ba.shape[1] == 2 * LANES and CHUNK % (SUBLANES * ROW_STRIDE) == 0
    assert pw % IN_PROJ_TILE == 0
    resident = dict(pipeline_mode=pl.Buffered(1))

    def next_first_block(i, t):
        nxt = jnp.minimum(i * steps + t + 1, b * steps - 1)
        return (nxt // steps, (nxt % steps) * blocks, 0)

    kern = functools.partial(_gdn_layer_kernel, heads=heads, dk=dk, dv=dv, nc=nc)
    return pl.pallas_call(
        kern, grid=(b, steps),
        in_specs=[pl.BlockSpec((1, step_rows, d), lambda i, t: (i, t, 0)),
                  pl.BlockSpec((1, rows, d), next_first_block),
                  pl.BlockSpec((1, d), lambda i, t: (0, 0)),
                  pl.BlockSpec((d, pw), lambda i, t: (0, 0), **resident),
                  pl.BlockSpec((d, 2 * LANES), lambda i, t: (0, 0), **resident),
                  pl.BlockSpec((CONV_W, conv_ch), lambda i, t: (0, 0)),
                  pl.BlockSpec((1, LANES), lambda i, t: (0, 0)),
                  pl.BlockSpec((1, LANES), lambda i, t: (0, 0)),
                  pl.BlockSpec((1, dv), lambda i, t: (0, 0)),
                  pl.BlockSpec((heads * dv, d), lambda i, t: (0, 0), **resident)],
        out_specs=[pl.BlockSpec((1, step_rows, d), lambda i, t: (i, t, 0)),
                   pl.BlockSpec((1, heads, dk, dv), lambda i, t: (i, 0, 0, 0)),
                   pl.BlockSpec((1, 1, CONV_W - 1, conv_ch), lambda i, t: (0, i, 0, 0))],
        out_shape=[jax.ShapeDtypeStruct((b, l, d), F32),
                   jax.ShapeDtypeStruct((b, heads, dk, dv), F32),
                   jax.ShapeDtypeStruct((1, b, CONV_W - 1, conv_ch), F32)],
        scratch_shapes=[pltpu.VMEM((1, heads, dk, dv), F32),
                        pltpu.VMEM((blocks, conv_ch // LANES, SUBLANES + rows, LANES), F32),
                        pltpu.VMEM((blocks, heads + 2, rows, LANES), F32),
                        pltpu.VMEM((blocks, heads, rows, LANES), F32)],
        compiler_params=_params("arbitrary", "arbitrary"), name="gdn_layer",
    )(x, x, norm_w.reshape(1, d), w_in, w_ba, conv_w, _lane_pad(a_log), _lane_pad(dt_bias),
      onorm_w.reshape(1, dv), w_out)


def _ret_chunk(cols, put_out, cos_ref, sin_ref, s0_ref, onw_ref, s_ref,
               *, heads, dk, dv, nb, c, first_step):
    qk_w = heads * dk
    g_off = 2 * qk_w + heads * dv
    half = dk // 2

    if first_step is not None:
        @pl.when(first_step)
        def _():
            s_ref[...] = jnp.zeros(s_ref.shape, F32) if s0_ref is None else s0_ref[...]
    state_ref = s0_ref if first_step is None else s_ref

    cos = cos_ref[...]
    sin = sin_ref[...]
    row = lax.broadcasted_iota(jnp.int32, (c, c), 0)
    col = lax.broadcasted_iota(jnp.int32, (c, c), 1)
    diff = (row - col).astype(F32)
    idx = lax.broadcasted_iota(jnp.int32, (c, 1), 0).astype(F32)

    def rotary(x):
        x1, x2 = x[:, :half], x[:, half:]
        return jnp.concatenate([x1 * cos - x2 * sin, x1 * sin + x2 * cos], axis=-1)

    decay, q_dec, k_dec, s_dec = [], [], [], []
    for h in range(heads):
        lg = jnp.log(jnp.full((1, 1), 1.0 - 2.0 ** (-5.0 - h), F32))
        decay.append(jnp.where(diff >= 0, jnp.exp(lg * jnp.maximum(diff, 0.0)), 0.0))
        q_dec.append(jnp.exp(lg * (idx + 1.0)))
        k_dec.append(jnp.exp(lg * (c - 1.0 - idx)))
        s_dec.append(jnp.exp(lg * c))

    units = [(j, h) for j in range(nb) for h in range(heads)]
    qb, kb, kdb, vb = [], [], [], []
    for j, h in units:
        q = rotary(cols(j, h * dk, (h + 1) * dk))
        k = rotary(cols(j, qk_w + h * dk, qk_w + (h + 1) * dk)) * (dk ** -0.5)
        qb.append(q.astype(BF16))
        kb.append(k.astype(BF16))
        kdb.append((k * k_dec[h]).astype(BF16))
        vb.append(cols(j, 2 * qk_w + h * dv, 2 * qk_w + (h + 1) * dv).astype(BF16))
    qk = [(_dot_nt(qb[i], kb[i]) * decay[h]).astype(BF16) for i, (j, h) in enumerate(units)]
    s_old = [state_ref[j, h] for j, h in units]
    qs = [_dot(qb[i], s_old[i].astype(BF16)) for i in range(len(units))]
    o = [_dot(qk[i], vb[i]) + q_dec[h] * qs[i] for i, (j, h) in enumerate(units)]
    for i, (j, h) in enumerate(units):
        s_ref[j, h] = s_dec[h] * s_old[i] + _dot_tn(kdb[i], vb[i])
    for i, (j, h) in enumerate(units):
        put_out(j, h, _rms_scale(o[i], onw_ref[h:h + 1, :])
                * _silu(cols(j, g_off + h * dv, g_off + (h + 1) * dv)))


def _ret_side_chunk(ps_ref, cos_ref, sin_ref, s0_ref, onw_ref, o_ref, s_ref, *, heads, dk, dv,
                    nb):
    c = ps_ref.shape[1] // nb

    def cols(j, lo, hi):
        return ps_ref[0, j * c:(j + 1) * c, lo:hi]

    def put_out(j, h, value):
        o_ref[0, j * c:(j + 1) * c, h * dv:(h + 1) * dv] = value

    _ret_chunk(cols, put_out, cos_ref, sin_ref, s0_ref, onw_ref, s_ref,
               heads=heads, dk=dk, dv=dv, nb=nb, c=c, first_step=None)


def _ret_core_kernel(ps_ref, cos_ref, sin_ref, s0_ref, onw_ref, o_ref, s_ref,
                     *, heads, dk, dv, nb):
    _ret_side_chunk(ps_ref, cos_ref, sin_ref, s0_ref, onw_ref, o_ref, s_ref,
                    heads=heads, dk=dk, dv=dv, nb=nb)


def _ret_side_specs(p_s, s0_s, nbs, group):
    bs, ls, pw = p_s.shape
    heads, dk, dv = s0_s.shape[1:]
    assert bs % nbs == 0 and ls <= CHUNK and ls % SUBLANES == 0 and dk // 2 == LANES
    assert pw == 2 * heads * dk + 2 * heads * dv
    in_specs = [pl.BlockSpec((1, nbs * ls, pw), lambda *g: (group(*g), 0, 0)),
                pl.BlockSpec((ls, dk // 2), lambda *g: (0, 0)),
                pl.BlockSpec((ls, dk // 2), lambda *g: (0, 0)),
                pl.BlockSpec((nbs, heads, dk, dv), lambda *g: (group(*g), 0, 0, 0))]
    out_specs = [pl.BlockSpec((1, nbs * ls, heads * dv), lambda *g: (group(*g), 0, 0)),
                 pl.BlockSpec((nbs, heads, dk, dv), lambda *g: (group(*g), 0, 0, 0))]
    out_shape = [jax.ShapeDtypeStruct((bs // nbs, nbs * ls, heads * dv), F32),
                 jax.ShapeDtypeStruct(s0_s.shape, F32)]
    return in_specs, out_specs, out_shape


def _ret_core(p_s, cos_s, sin_s, s0_s, onorm_w):
    bs, ls, pw = p_s.shape
    heads, dk, dv = s0_s.shape[1:]
    nbs = RET_SEQS_PER_STEP if bs % RET_SEQS_PER_STEP == 0 else 1
    in_specs, out_specs, out_shape = _ret_side_specs(p_s, s0_s, nbs, lambda i: i)
    in_specs.append(pl.BlockSpec((heads, dv), lambda i: (0, 0)))
    o, s = pl.pallas_call(
        functools.partial(_ret_core_kernel, heads=heads, dk=dk, dv=dv, nb=nbs),
        grid=(bs // nbs,), in_specs=in_specs, out_specs=out_specs, out_shape=out_shape,
        compiler_params=_params("parallel"), name="ret_core",
    )(p_s.reshape(bs // nbs, nbs * ls, pw), cos_s, sin_s, s0_s, onorm_w)
    return o.reshape(bs, ls, heads * dv), s


def _ret_layer_kernel(*refs, heads, dk, dv, c, side_seqs):
    x_ref, nw_ref, win_ref, cos_ref, sin_ref, onw_ref, wout_ref, fw_ref = refs[:8]
    if side_seqs:
        ps_ref, cos_s_ref, sin_s_ref, s0s_ref, y_ref, s_ref, os_ref, ss_ref, o_scr = refs[8:]
    else:
        y_ref, s_ref, o_scr = refs[8:]

    x = x_ref[0]
    p = _dot(_rms_scale(x, nw_ref[...]).astype(BF16), win_ref[...])

    def cols(j, lo, hi):
        return p[:, lo:hi]

    def put_out(j, h, value):
        o_scr[:, h * dv:(h + 1) * dv] = value.astype(BF16)

    _ret_chunk(cols, put_out, cos_ref, sin_ref, None, onw_ref, s_ref,
               heads=heads, dk=dk, dv=dv, nb=1, c=c, first_step=pl.program_id(1) == 0)
    if side_seqs:
        _ret_side_chunk(ps_ref, cos_s_ref, sin_s_ref, s0s_ref, onw_ref, os_ref, ss_ref,
                        heads=heads, dk=dk, dv=dv, nb=side_seqs)
    y_ref[0] = _rms_scale(x + _dot(o_scr[...], wout_ref[...]), fw_ref[...])


def _ret_layer(x, norm_w, w_in, cos, sin, onorm_w, w_out, final_w, heads, side=None):
    b, l, d = x.shape
    pw = w_in.shape[1]
    dv = w_out.shape[0] // heads
    dk = (pw - 2 * heads * dv) // (2 * heads)
    c = RET_CHUNK
    steps = l // c
    assert l % c == 0 and dk // 2 == LANES
    resident = dict(pipeline_mode=pl.Buffered(1))
    in_specs = [pl.BlockSpec((1, c, d), lambda i, t: (i, t, 0)),
                pl.BlockSpec((1, d), lambda i, t: (0, 0)),
                pl.BlockSpec((d, pw), lambda i, t: (0, 0), **resident),
                pl.BlockSpec((c, dk // 2), lambda i, t: (t, 0)),
                pl.BlockSpec((c, dk // 2), lambda i, t: (t, 0)),
                pl.BlockSpec((heads, dv), lambda i, t: (0, 0)),
                pl.BlockSpec((heads * dv, d), lambda i, t: (0, 0), **resident),
                pl.BlockSpec((1, d), lambda i, t: (0, 0))]
    out_specs = [pl.BlockSpec((1, c, d), lambda i, t: (i, t, 0)),
                 pl.BlockSpec((1, heads, dk, dv), lambda i, t: (i, 0, 0, 0))]
    out_shape = [jax.ShapeDtypeStruct((b, l, d), F32),
                 jax.ShapeDtypeStruct((b, heads, dk, dv), F32)]
    args = [x, norm_w.reshape(1, d), w_in, cos, sin, onorm_w, w_out, final_w.reshape(1, d)]
    nbs = 0
    if side is not None:
        p_s, cos_s, sin_s, s0_s = side
        bs, ls, _ = p_s.shape
        nbs = RET_SEQS_PER_STEP
        assert bs == nbs * b * steps and p_s.shape[2] == pw
        assert s0_s.shape == (bs, heads, dk, dv)
        side_in, side_out, side_shape = _ret_side_specs(p_s, s0_s, nbs,
                                                        lambda i, t: i * steps + t)
        in_specs += side_in
        out_specs += side_out
        out_shape += side_shape
        args += [p_s.reshape(bs // nbs, nbs * ls, pw), cos_s, sin_s, s0_s]
    kern = functools.partial(_ret_layer_kernel, heads=heads, dk=dk, dv=dv, c=c, side_seqs=nbs)
    outs = pl.pallas_call(
        kern, grid=(b, steps), in_specs=in_specs, out_specs=out_specs, out_shape=out_shape,
        scratch_shapes=[pltpu.VMEM((c, heads * dv), BF16)],
        compiler_params=_params("parallel", "arbitrary"), name="ret_layer",
    )(*args)
    if side is None:
        return outs
    y, s, o_s, s_s = outs
    return y, s, o_s.reshape(bs, ls, heads * dv), s_s


def _row_tile(t):
    for tm in (256, 128, 64, 32, 16, 8):
        if t % tm == 0:
            return tm
    raise ValueError(f"token count {t} is not a multiple of {SUBLANES}")


def _rope_tables(l, half, pos0):
    inv = 1.0 / (ROPE_BASE ** jnp.linspace(0.0, 1.0, half, dtype=F32))
    pos = pos0 + jnp.arange(l, dtype=F32)
    ang = pos[:, None] * inv[None, :]
    return jnp.cos(ang), jnp.sin(ang)


def _beta_decay_weight(w, heads, main):
    zeros = jnp.zeros((w.shape[0], LANES - heads), w.dtype)
    return jnp.concatenate([w[:, main:main + heads], zeros, w[:, main + heads:], zeros],
                           axis=1).astype(BF16)


def kernel(x_prompt, x_sample, state_gdn_ssm, state_gdn_conv, state_ret, norm_w, w_in_a,
           conv_w_a, a_log_a, dt_bias_a, onorm_a, w_out_a, w_in_b, onorm_b, w_out_b,
           final_norm_w):
    assert state_gdn_ssm.shape[0] == 1 and state_ret.shape[0] == 1 and norm_w.shape[0] == 2
    bp, lp, d = x_prompt.shape
    bs, ls, _ = x_sample.shape
    heads_a, dk_a, dv_a = state_gdn_ssm.shape[2:]
    heads_b, dk_b, dv_b = state_ret.shape[2:]
    main_a = 2 * heads_a * dk_a + 2 * heads_a * dv_a
    win_a = w_in_a[0].astype(BF16)
    wba_a = _beta_decay_weight(w_in_a[0], heads_a, main_a)
    wout_a = w_out_a[0].astype(BF16)
    win_b = w_in_b[0].astype(BF16)
    wout_b = w_out_b[0].astype(BF16)

    ts = bs * ls
    tm = _row_tile(ts)
    xs = x_sample.reshape(ts, d)
    p, ba = _in_proj(xs, norm_w[0], [(win_a, main_a), (wba_a, 2 * LANES)], tm)
    o, sa_s, ca_s = _gdn_core(p.reshape(bs, ls, -1), ba.reshape(bs, ls, -1), state_gdn_ssm[0],
                              jnp.swapaxes(state_gdn_conv[0], 0, 1), conv_w_a[0], a_log_a[0],
                              dt_bias_a[0], onorm_a[0])
    xs = _out_proj(o.reshape(ts, -1), xs, wout_a, final_norm_w, tm, False)
    (p_s,) = _in_proj(xs, norm_w[1], [(win_b, win_b.shape[1])], tm)
    p_s = p_s.reshape(bs, ls, -1)
    cos_s, sin_s = _rope_tables(ls, dk_b // 2, PAST_LEN)

    x1, sa_p, ca_p = _gdn_layer(x_prompt, norm_w[0], win_a, wba_a, conv_w_a[0], a_log_a[0],
                                dt_bias_a[0], onorm_a[0], wout_a, heads_a)
    cos_p, sin_p = _rope_tables(lp, dk_b // 2, 0.0)
    if bs == RET_SEQS_PER_STEP * bp * (lp // RET_CHUNK):
        y_p, sb_p, o, sb_s = _ret_layer(x1, norm_w[1], win_b, cos_p, sin_p, onorm_b[0], wout_b,
                                        final_norm_w, heads_b,
                                        side=(p_s, cos_s, sin_s, state_ret[0]))
    else:
        y_p, sb_p = _ret_layer(x1, norm_w[1], win_b, cos_p, sin_p, onorm_b[0], wout_b,
                               final_norm_w, heads_b)
        o, sb_s = _ret_core(p_s, cos_s, sin_s, state_ret[0], onorm_b[0])
    y_s = _out_proj(o.reshape(ts, -1), xs, wout_b, final_norm_w, tm, True).reshape(bs, ls, d)
    ca_s = jnp.swapaxes(ca_s, 0, 1)[None]
    return (y_p, y_s, sa_p[None], ca_p, sb_p[None], sa_s[None], ca_s, sb_s[None])
```

```python
import functools

import jax
import jax.numpy as jnp
from jax import lax
from jax.experimental import pallas as pl
from jax.experimental.pallas import tpu as pltpu

F32 = jnp.float32
BF16 = jnp.bfloat16
EPS = 1e-6
CHUNK = 64
CONV_W = 4
ROPE_BASE = 10000.0
PAST_LEN = 16384.0
LANES = 128
SUBLANES = 8
HIST0 = SUBLANES - (CONV_W - 1)
VMEM_LIMIT_BYTES = 60 * 1024 * 1024
GDN_CHUNKS_PER_STEP = 4
GDN_BLOCKS_PER_STEP = 2
IN_PROJ_COLS = 1024
IN_PROJ_TILE = 256
ROW_STRIDE = 4
RET_CHUNK = 256
RET_SEQS_PER_STEP = 2


def _dot(a, b):
    return jnp.dot(a, b, preferred_element_type=F32)


def _dot_nt(a, b):
    return lax.dot_general(a, b, (((1,), (1,)), ((), ())), preferred_element_type=F32)


def _dot_tn(a, b):
    return lax.dot_general(a, b, (((0,), (0,)), ((), ())), preferred_element_type=F32)


def _split3(x):
    hi = x.astype(BF16)
    r = x - hi.astype(F32)
    mid = r.astype(BF16)
    lo = (r - mid.astype(F32)).astype(BF16)
    return hi, mid, lo


def _sigmoid(x):
    return 1.0 / (1.0 + jnp.exp(-x))


def _silu(x):
    return x * _sigmoid(x)


def _softplus(x):
    return jnp.maximum(x, 0.0) + jnp.log1p(jnp.exp(-jnp.abs(x)))


def _rms_scale(x, w):
    return x * lax.rsqrt(jnp.mean(x * x, axis=-1, keepdims=True) + EPS) * w


def _params(*sem):
    return pltpu.CompilerParams(dimension_semantics=sem, vmem_limit_bytes=VMEM_LIMIT_BYTES)


def _lane_pad(vec):
    return jnp.pad(vec.astype(F32), (0, LANES - vec.shape[0])).reshape(1, LANES)


def _in_proj_kernel(x_ref, nw_ref, w_ref, *refs):
    xn_scr = refs[-1]
    o_ref = refs[-2] if len(refs) == 2 else refs[1]

    @pl.when(pl.program_id(0) == 0)
    def _():
        xn_scr[...] = _rms_scale(x_ref[...], nw_ref[...]).astype(BF16)
        if len(refs) == 4:
            side_w_ref, _, side_o_ref, _ = refs
            side_o_ref[...] = _dot(xn_scr[...], side_w_ref[...])

    o_ref[...] = _dot(xn_scr[...], w_ref[...])


def _in_proj(x, norm_w, w, n, side_w=None):
    t, d = x.shape
    cols = IN_PROJ_COLS
    assert n % cols == 0 and t % SUBLANES == 0 and t * d * 4 <= 8 * 1024 * 1024
    in_specs = [pl.BlockSpec((t, d), lambda j: (0, 0)),
                pl.BlockSpec((1, d), lambda j: (0, 0)),
                pl.BlockSpec((d, cols), lambda j: (0, j))]
    out_specs = [pl.BlockSpec((t, cols), lambda j: (0, j))]
    out_shape = [jax.ShapeDtypeStruct((t, n), F32)]
    args = [x, norm_w.reshape(1, d), w]
    if side_w is not None:
        in_specs.append(pl.BlockSpec(side_w.shape, lambda j: (0, 0)))
        out_specs.append(pl.BlockSpec((t, side_w.shape[1]), lambda j: (0, 0)))
        out_shape.append(jax.ShapeDtypeStruct((t, side_w.shape[1]), F32))
        args.append(side_w)
    return pl.pallas_call(
        _in_proj_kernel, grid=(n // cols,), in_specs=in_specs, out_specs=out_specs,
        out_shape=out_shape, scratch_shapes=[pltpu.VMEM((t, d), BF16)],
        compiler_params=_params("arbitrary"), name="in_proj",
    )(*args)


def _out_proj_kernel(o_ref, x_ref, w_ref, fw_ref, y_ref, *, final_norm):
    y = x_ref[...] + _dot(o_ref[...].astype(BF16), w_ref[...])
    y_ref[...] = _rms_scale(y, fw_ref[...]) if final_norm else y


def _out_proj(o, x, w, final_w, tm, final_norm):
    t, k = o.shape
    d = x.shape[1]
    assert t % tm == 0
    return pl.pallas_call(
        functools.partial(_out_proj_kernel, final_norm=final_norm),
        grid=(t // tm,),
        in_specs=[pl.BlockSpec((tm, k), lambda i: (i, 0)),
                  pl.BlockSpec((tm, d), lambda i: (i, 0)),
                  pl.BlockSpec((k, d), lambda i: (0, 0)),
                  pl.BlockSpec((1, d), lambda i: (0, 0))],
        out_specs=pl.BlockSpec((tm, d), lambda i: (i, 0)),
        out_shape=jax.ShapeDtypeStruct((t, d), F32),
        compiler_params=_params("parallel"), name="out_proj",
    )(o, x, w, final_w.reshape(1, d))


def _unit_lower_solves(lows, rhs, eye, cl):
    powers = [(-low).astype(BF16) for low in lows]
    invs = [eye - low for low in lows]
    p = 1
    while 2 * p < cl:
        sq = [_dot(pw, pw) for pw in powers]
        p *= 2
        powers = [s.astype(BF16) for s in sq]
        invs = [inv + _dot(inv.astype(BF16), pw) for inv, pw in zip(invs, powers)]
    return [_dot(inv.astype(BF16), r) for inv, r in zip(invs, rhs)]


def _gdn_step(front, tok, alog_ref, dtb_ref, onw_ref, s_scr, *, heads, dk, dv, nb, cl, nc,
              fill=lambda: None):
    c = nb * cl
    qk_w = heads * dk
    row = lax.broadcasted_iota(jnp.int32, (c, c), 0)
    col = lax.broadcasted_iota(jnp.int32, (c, c), 1)
    shift = cl.bit_length() - 1
    same = lax.shift_right_logical(row, shift) == lax.shift_right_logical(col, shift)
    incl = same & (tok(row) >= tok(col))
    strict = same & (tok(row) > tok(col))
    eye = (row == col).astype(F32)
    cum_masks = jnp.concatenate([incl.astype(BF16), same.astype(BF16)], axis=0)
    sel = (lax.broadcasted_iota(jnp.int32, (SUBLANES, LANES), 0)
           == lax.broadcasted_iota(jnp.int32, (SUBLANES, LANES), 1)).astype(BF16)
    neg_a = -jnp.exp(alog_ref[...])
    units = [(ci, h) for ci in range(nc) for h in range(heads)]
    seqs = range(nb)

    def stack(parts):
        return parts[0] if nb == 1 else jnp.concatenate(parts, axis=0)

    beta, gcum, gtot, gcum_t = [], [], [], []
    for ci in range(nc):
        beta.append(_sigmoid(front.gate(ci, 0)))
        g = neg_a * _softplus(front.gate(ci, 1) + dtb_ref[...])
        g3 = _split3(g)
        gg = _dot(cum_masks, g3[0]) + (_dot(cum_masks, g3[1]) + _dot(cum_masks, g3[2]))
        gcum.append(gg[:c])
        gtot.append(gg[c:])
        gc3 = _split3(gg[:c])
        gcum_t.append(_dot_nt(sel, gc3[0]) + (_dot_nt(sel, gc3[1]) + _dot_nt(sel, gc3[2])))

    q_l, eg_l, lows, rhs_l, qk_l, kd_l = [], [], [], [], [], []
    for ci, h in units:
        q = front.mixed(ci, h * dk, (h + 1) * dk)
        k = front.mixed(ci, qk_w + h * dk, qk_w + (h + 1) * dk)
        v = front.mixed(ci, 2 * qk_w + h * dv, 2 * qk_w + (h + 1) * dv)
        q = q * lax.rsqrt(jnp.sum(q * q, axis=-1, keepdims=True) + EPS) * (dk ** -0.5)
        k = k * lax.rsqrt(jnp.sum(k * k, axis=-1, keepdims=True) + EPS)
        gc = gcum[ci][:, h:h + 1]
        gr = gcum_t[ci][h:h + 1, :]
        bh = beta[ci][:, h:h + 1]
        dec = jnp.where(incl, jnp.exp(jnp.where(incl, gc - gr, 0.0)), 0.0)
        eg = jnp.exp(gc)
        qb = q.astype(BF16)
        kb = k.astype(BF16)
        lows.append(jnp.where(strict, bh * _dot_nt(kb, kb) * dec, 0.0))
        qk_l.append((_dot_nt(qb, kb) * dec).astype(BF16))
        rhs_l.append(jnp.concatenate([bh * v, (bh * eg) * k], axis=-1).astype(BF16))
        kd_l.append(k * jnp.exp(gtot[ci][:, h:h + 1] - gc))
        q_l.append(q)
        eg_l.append(eg)
        fill()
    sols = _unit_lower_solves(lows, rhs_l, eye, cl)

    for ci in range(nc):
        idx = [ci * heads + h for h in range(heads)]
        s_old = [[s_scr[j, h] for j in seqs] for h in range(heads)]
        res = [[_dot(jnp.concatenate([sols[i][j * cl:(j + 1) * cl, dv:],
                                      q_l[i][j * cl:(j + 1) * cl]], axis=0).astype(BF16),
                     s_old[h][j].astype(BF16)) for j in seqs]
               for h, i in enumerate(idx)]
        u = [stack([sols[i][j * cl:(j + 1) * cl, :dv] - res[h][j][:cl] for j in seqs])
             for h, i in enumerate(idx)]
        o = [eg_l[i] * stack([res[h][j][cl:] for j in seqs]) + _dot(qk_l[i], u[h].astype(BF16))
             for h, i in enumerate(idx)]
        for h, i in enumerate(idx):
            for j in seqs:
                r0, r1 = j * cl, (j + 1) * cl
                s_scr[j, h] = (jnp.exp(gtot[ci][r0:r0 + 1, h:h + 1]) * s_old[h][j]
                               + _dot_tn(kd_l[i][r0:r1].astype(BF16), u[h][r0:r1].astype(BF16)))
        for h in range(heads):
            front.put(ci, h, _rms_scale(o[h], onw_ref[...]) * _silu(front.z(ci, h)))


class _RollFront:
    def __init__(self, p_ref, ba_ref, o_ref, xe_scr, cw_ref, *, dv, nb):
        self.p_ref, self.ba_ref, self.o_ref, self.dv = p_ref, ba_ref, o_ref, dv
        self.conv_ch = xe_scr.shape[2]
        conv = []
        for j in range(nb):
            xall = xe_scr[j]
            acc = xall * cw_ref[0:1, :]
            for tap in range(1, CONV_W):
                acc = pltpu.roll(acc, 1, axis=0) + xall * cw_ref[tap:tap + 1, :]
            conv.append(acc[SUBLANES:])
        self.act = _silu(conv[0] if nb == 1 else jnp.concatenate(conv, axis=0))

    def mixed(self, ci, lo, hi):
        return self.act[:, lo:hi]

    def gate(self, ci, k):
        return self.ba_ref[0, :, k * LANES:(k + 1) * LANES]

    def z(self, ci, h):
        return self.p_ref[0, :, self.conv_ch + h * self.dv:self.conv_ch + (h + 1) * self.dv]

    def put(self, ci, h, value):
        self.o_ref[0, :, h * self.dv:(h + 1) * self.dv] = value


def _gdn_core_kernel(p_ref, ba_ref, s0_ref, c0_ref, cw_ref, alog_ref, dtb_ref, onw_ref,
                     o_ref, s_ref, c_ref, s_scr, xe_scr, *, heads, dk, dv, nb, cl):
    conv_ch = xe_scr.shape[2]
    s_scr[...] = s0_ref[...]
    for j in range(nb):
        xe_scr[j, 0:SUBLANES, :] = jnp.zeros((SUBLANES, conv_ch), F32)
        for r in range(CONV_W - 1):
            xe_scr[j, HIST0 + r:HIST0 + r + 1, :] = c0_ref[r, j:j + 1, :]
        xe_scr[j, SUBLANES:SUBLANES + cl, :] = p_ref[0, j * cl:(j + 1) * cl, 0:conv_ch]
    front = _RollFront(p_ref, ba_ref, o_ref, xe_scr, cw_ref, dv=dv, nb=nb)
    _gdn_step(front, lambda r: r & (cl - 1), alog_ref, dtb_ref, onw_ref, s_scr,
              heads=heads, dk=dk, dv=dv, nb=nb, cl=cl, nc=1)
    s_ref[...] = s_scr[...]
    for j in range(nb):
        for r in range(CONV_W - 1):
            c_ref[r, j:j + 1, :] = xe_scr[j, cl + HIST0 + r:cl + HIST0 + r + 1, :]


def _gdn_core(p, ba, s0, c0, conv_w, a_log, dt_bias, onorm_w):
    b, l, pw = p.shape
    heads, dk, dv = s0.shape[1:]
    conv_ch = c0.shape[2]
    assert c0.shape[:2] == (CONV_W - 1, b)
    assert heads <= SUBLANES and dk == LANES and dv == LANES
    assert pw == conv_ch + heads * dv and ba.shape[2] == 2 * LANES
    assert l <= CHUNK and l % SUBLANES == 0 and l & (l - 1) == 0
    nb = max(n for n in range(1, CHUNK // l + 1) if b % n == 0)
    groups, rows = b // nb, nb * l
    kern = functools.partial(_gdn_core_kernel, heads=heads, dk=dk, dv=dv, nb=nb, cl=l)
    o, s, cs = pl.pallas_call(
        kern, grid=(groups, 1),
        in_specs=[pl.BlockSpec((1, rows, pw), lambda i, t: (i, 0, 0)),
                  pl.BlockSpec((1, rows, 2 * LANES), lambda i, t: (i, 0, 0)),
                  pl.BlockSpec((nb, heads, dk, dv), lambda i, t: (i, 0, 0, 0)),
                  pl.BlockSpec((CONV_W - 1, nb, conv_ch), lambda i, t: (0, i, 0)),
                  pl.BlockSpec((CONV_W, conv_ch), lambda i, t: (0, 0)),
                  pl.BlockSpec((1, LANES), lambda i, t: (0, 0)),
                  pl.BlockSpec((1, LANES), lambda i, t: (0, 0)),
                  pl.BlockSpec((1, dv), lambda i, t: (0, 0))],
        out_specs=[pl.BlockSpec((1, rows, heads * dv), lambda i, t: (i, 0, 0)),
                   pl.BlockSpec((nb, heads, dk, dv), lambda i, t: (i, 0, 0, 0)),
                   pl.BlockSpec((CONV_W - 1, nb, conv_ch), lambda i, t: (0, i, 0))],
        out_shape=[jax.ShapeDtypeStruct((groups, rows, heads * dv), F32),
                   jax.ShapeDtypeStruct(s0.shape, F32),
                   jax.ShapeDtypeStruct(c0.shape, F32)],
        scratch_shapes=[pltpu.VMEM((nb, heads, dk, dv), F32),
                        pltpu.VMEM((nb, SUBLANES + l, conv_ch), F32)],
        compiler_params=_params("parallel", "arbitrary"), name="gdn_core",
    )(p.reshape(groups, rows, pw), ba.reshape(groups, rows, 2 * LANES), s0, c0, conv_w,
      _lane_pad(a_log), _lane_pad(dt_bias), onorm_w.reshape(1, dv))
    return o.reshape(b, l, heads * dv), s, cs


class _StridedFront:
    def __init__(self, xe_scr, zb_scr, o_scr, cw_ref, *, heads):
        self.xe, self.zb, self.o, self.cw, self.heads = xe_scr, zb_scr, o_scr, cw_ref, heads

    @staticmethod
    def tok(r):
        span = SUBLANES * ROW_STRIDE
        return ((r & (CHUNK - span)) + (lax.shift_right_logical(r, 3) & (ROW_STRIDE - 1))
                + (r & (SUBLANES - 1)) * ROW_STRIDE)

    @staticmethod
    def _starts(ci):
        return [ci * CHUNK + g * SUBLANES * ROW_STRIDE + s
                for g in range(CHUNK // (SUBLANES * ROW_STRIDE)) for s in range(ROW_STRIDE)]

    def _rows(self, ref, slab, base, ci):
        return jnp.concatenate([ref[slab, pl.ds(base + st, SUBLANES, stride=ROW_STRIDE), :]
                                for st in self._starts(ci)], axis=0)

    def mixed(self, ci, lo, hi):
        assert hi - lo == LANES and lo % LANES == 0
        acc = self._rows(self.xe, lo // LANES, HIST0, ci) * self.cw[0:1, lo:hi]
        for tap in range(1, CONV_W):
            acc = acc + self._rows(self.xe, lo // LANES, HIST0 + tap, ci) * self.cw[tap:tap + 1,
                                                                                   lo:hi]
        return _silu(acc)

    def gate(self, ci, k):
        return self._rows(self.zb, self.heads + k, 0, ci)

    def z(self, ci, h):
        return self._rows(self.zb, h, 0, ci)

    def put(self, ci, h, value):
        for n, st in enumerate(self._starts(ci)):
            self.o[h, pl.ds(st, SUBLANES, stride=ROW_STRIDE), :] = value[n * SUBLANES:
                                                                         (n + 1) * SUBLANES]


def _gdn_layer_kernel(x_ref, xnext_ref, nw_ref, win_ref, wba_ref, cw_ref, alog_ref, dtb_ref,
                      onw_ref, wout_ref, y_ref, s_ref, c_ref, s_scr, xe_scr, zb_scr, o_scr,
                      *, heads, dk, dv, nc):
    rows = nc * CHUNK
    blocks = xe_scr.shape[0]
    n_conv = xe_scr.shape[1]
    t = pl.program_id(1)
    step = pl.program_id(0) * pl.num_programs(1) + t

    def in_proj_tiles(x_rows, blk):
        xn = _rms_scale(x_rows(), nw_ref[...]).astype(BF16)
        for lo in range(0, (n_conv + heads) * LANES, IN_PROJ_TILE):
            r = _dot(xn, win_ref[:, lo:lo + IN_PROJ_TILE])
            for k in range(IN_PROJ_TILE // LANES):
                s = lo // LANES + k
                piece = r[:, k * LANES:(k + 1) * LANES]
                if s < n_conv:
                    xe_scr[blk, s, SUBLANES:SUBLANES + rows, :] = piece
                else:
                    zb_scr[blk, s - n_conv] = piece
            yield
        ba = _dot(xn, wba_ref[...])
        zb_scr[blk, heads] = ba[:, 0:LANES]
        zb_scr[blk, heads + 1] = ba[:, LANES:2 * LANES]
        yield

    def block_rows(blk):
        return lambda: x_ref[0, blk * rows:(blk + 1) * rows, :]

    @pl.when(step == 0)
    def _():
        for _ in in_proj_tiles(block_rows(0), 0):
            pass

    @pl.when(t == 0)
    def _():
        s_scr[...] = jnp.zeros(s_scr.shape, F32)
        for s in range(n_conv):
            xe_scr[0, s, 0:SUBLANES, :] = jnp.zeros((SUBLANES, LANES), F32)

    for blk in range(blocks):
        xe, zb, osc = xe_scr.at[blk], zb_scr.at[blk], o_scr.at[blk]
        if blk + 1 < blocks:
            ahead = in_proj_tiles(block_rows(blk + 1), blk + 1)
        else:
            ahead = in_proj_tiles(lambda: xnext_ref[0], 0)
        front = _StridedFront(xe, zb, osc, cw_ref, heads=heads)
        _gdn_step(front, front.tok, alog_ref, dtb_ref, onw_ref, s_scr, heads=heads, dk=dk,
                  dv=dv, nb=1, cl=CHUNK, nc=nc, fill=lambda: next(ahead, None))
        for _ in ahead:
            pass
        o = jnp.concatenate([osc[h] for h in range(heads)], axis=-1).astype(BF16)
        y_ref[0, blk * rows:(blk + 1) * rows, :] = (x_ref[0, blk * rows:(blk + 1) * rows, :]
                                                    + _dot(o, wout_ref[...]))
        hist = [xe[s, rows + HIST0:rows + SUBLANES, :] for s in range(n_conv)]
        nxt = xe_scr.at[(blk + 1) % blocks]
        for s in range(n_conv):
            nxt[s, HIST0:SUBLANES, :] = hist[s]

    @pl.when(t == pl.num_programs(1) - 1)
    def _():
        s_ref[...] = s_scr[...]
        for s in range(n_conv):
            c_ref[0, 0, :, s * LANES:(s + 1) * LANES] = hist[s]


def _gdn_layer(x, norm_w, w_in, w_ba, conv_w, a_log, dt_bias, onorm_w, w_out, heads):
    b, l, d = x.shape
    dv = w_out.shape[0] // heads
    dk = dv
    conv_ch = conv_w.shape[1]
    pw = conv_ch + heads * dv
    nc, blocks = GDN_CHUNKS_PER_STEP, GDN_BLOCKS_PER_STEP
    rows = CHUNK * nc
    step_rows = rows * blocks
    steps = l // step_rows
    assert l % step_rows == 0 and heads <= SUBLANES and dk == LANES and dv == LANES
    assert conv_ch == 3 * heads * dk and w_in.shape[1] >= pw and blocks >= 2
    assert w_ba.shape[1] == 2 * LANES and CHUNK % (SUBLANES * ROW_STRIDE) == 0
    assert pw % IN_PROJ_TILE == 0
    resident = dict(pipeline_mode=pl.Buffered(1))

    def next_first_block(i, t):
        nxt = jnp.minimum(i * steps + t + 1, b * steps - 1)
        return (nxt // steps, (nxt % steps) * blocks, 0)

    kern = functools.partial(_gdn_layer_kernel, heads=heads, dk=dk, dv=dv, nc=nc)
    return pl.pallas_call(
        kern, grid=(b, steps),
        in_specs=[pl.BlockSpec((1, step_rows, d), lambda i, t: (i, t, 0)),
                  pl.BlockSpec((1, rows, d), next_first_block),
                  pl.BlockSpec((1, d), lambda i, t: (0, 0)),
                  pl.BlockSpec((d, pw), lambda i, t: (0, 0), **resident),
                  pl.BlockSpec((d, 2 * LANES), lambda i, t: (0, 0), **resident),
                  pl.BlockSpec((CONV_W, conv_ch), lambda i, t: (0, 0)),
                  pl.BlockSpec((1, LANES), lambda i, t: (0, 0)),
                  pl.BlockSpec((1, LANES), lambda i, t: (0, 0)),
                  pl.BlockSpec((1, dv), lambda i, t: (0, 0)),
                  pl.BlockSpec((heads * dv, d), lambda i, t: (0, 0), **resident)],
        out_specs=[pl.BlockSpec((1, step_rows, d), lambda i, t: (i, t, 0)),
                   pl.BlockSpec((1, heads, dk, dv), lambda i, t: (i, 0, 0, 0)),
                   pl.BlockSpec((1, 1, CONV_W - 1, conv_ch), lambda i, t: (0, i, 0, 0))],
        out_shape=[jax.ShapeDtypeStruct((b, l, d), F32),
                   jax.ShapeDtypeStruct((b, heads, dk, dv), F32),
                   jax.ShapeDtypeStruct((1, b, CONV_W - 1, conv_ch), F32)],
        scratch_shapes=[pltpu.VMEM((1, heads, dk, dv), F32),
                        pltpu.VMEM((blocks, conv_ch // LANES, SUBLANES + rows, LANES), F32),
                        pltpu.VMEM((blocks, heads + 2, rows, LANES), F32),
                        pltpu.VMEM((blocks, heads, rows, LANES), F32)],
        compiler_params=_params("arbitrary", "arbitrary"), name="gdn_layer",
    )(x, x, norm_w.reshape(1, d), w_in, w_ba, conv_w, _lane_pad(a_log), _lane_pad(dt_bias),
      onorm_w.reshape(1, dv), w_out)


def _ret_chunk(cols, put_out, cos_ref, sin_ref, s0_ref, onw_ref, s_ref,
               *, heads, dk, dv, nb, c, first_step):
    qk_w = heads * dk
    g_off = 2 * qk_w + heads * dv
    half = dk // 2

    if first_step is not None:
        @pl.when(first_step)
        def _():
            s_ref[...] = jnp.zeros(s_ref.shape, F32) if s0_ref is None else s0_ref[...]
    state_ref = s0_ref if first_step is None else s_ref

    cos = cos_ref[...]
    sin = sin_ref[...]
    row = lax.broadcasted_iota(jnp.int32, (c, c), 0)
    col = lax.broadcasted_iota(jnp.int32, (c, c), 1)
    diff = (row - col).astype(F32)
    idx = lax.broadcasted_iota(jnp.int32, (c, 1), 0).astype(F32)

    def rotary(x):
        x1, x2 = x[:, :half], x[:, half:]
        return jnp.concatenate([x1 * cos - x2 * sin, x1 * sin + x2 * cos], axis=-1)

    decay, q_dec, k_dec, s_dec = [], [], [], []
    for h in range(heads):
        lg = jnp.log(jnp.full((1, 1), 1.0 - 2.0 ** (-5.0 - h), F32))
        decay.append(jnp.where(diff >= 0, jnp.exp(lg * jnp.maximum(diff, 0.0)), 0.0))
        q_dec.append(jnp.exp(lg * (idx + 1.0)))
        k_dec.append(jnp.exp(lg * (c - 1.0 - idx)))
        s_dec.append(jnp.exp(lg * c))

    units = [(j, h) for j in range(nb) for h in range(heads)]
    qb, kb, kdb, vb = [], [], [], []
    for j, h in units:
        q = rotary(cols(j, h * dk, (h + 1) * dk))
        k = rotary(cols(j, qk_w + h * dk, qk_w + (h + 1) * dk)) * (dk ** -0.5)
        qb.append(q.astype(BF16))
        kb.append(k.astype(BF16))
        kdb.append((k * k_dec[h]).astype(BF16))
        vb.append(cols(j, 2 * qk_w + h * dv, 2 * qk_w + (h + 1) * dv).astype(BF16))
    qk = [(_dot_nt(qb[i], kb[i]) * decay[h]).astype(BF16) for i, (j, h) in enumerate(units)]
    s_old = [state_ref[j, h] for j, h in units]
    qs = [_dot(qb[i], s_old[i].astype(BF16)) for i in range(len(units))]
    o = [_dot(qk[i], vb[i]) + q_dec[h] * qs[i] for i, (j, h) in enumerate(units)]
    for i, (j, h) in enumerate(units):
        s_ref[j, h] = s_dec[h] * s_old[i] + _dot_tn(kdb[i], vb[i])
    for i, (j, h) in enumerate(units):
        put_out(j, h, _rms_scale(o[i], onw_ref[h:h + 1, :])
                * _silu(cols(j, g_off + h * dv, g_off + (h + 1) * dv)))


def _ret_side_chunk(ps_ref, cos_ref, sin_ref, s0_ref, onw_ref, o_ref, s_ref, *, heads, dk, dv,
                    nb):
    c = ps_ref.shape[1] // nb

    def cols(j, lo, hi):
        return ps_ref[0, j * c:(j + 1) * c, lo:hi]

    def put_out(j, h, value):
        o_ref[0, j * c:(j + 1) * c, h * dv:(h + 1) * dv] = value

    _ret_chunk(cols, put_out, cos_ref, sin_ref, s0_ref, onw_ref, s_ref,
               heads=heads, dk=dk, dv=dv, nb=nb, c=c, first_step=None)


def _ret_core_kernel(ps_ref, cos_ref, sin_ref, s0_ref, onw_ref, o_ref, s_ref,
                     *, heads, dk, dv, nb):
    _ret_side_chunk(ps_ref, cos_ref, sin_ref, s0_ref, onw_ref, o_ref, s_ref,
                    heads=heads, dk=dk, dv=dv, nb=nb)


def _ret_side_specs(p_s, s0_s, nbs, group):
    bs, ls, pw = p_s.shape
    heads, dk, dv = s0_s.shape[1:]
    assert bs % nbs == 0 and ls <= CHUNK and ls % SUBLANES == 0 and dk // 2 == LANES
    assert pw == 2 * heads * dk + 2 * heads * dv
    in_specs = [pl.BlockSpec((1, nbs * ls, pw), lambda *g: (group(*g), 0, 0)),
                pl.BlockSpec((ls, dk // 2), lambda *g: (0, 0)),
                pl.BlockSpec((ls, dk // 2), lambda *g: (0, 0)),
                pl.BlockSpec((nbs, heads, dk, dv), lambda *g: (group(*g), 0, 0, 0))]
    out_specs = [pl.BlockSpec((1, nbs * ls, heads * dv), lambda *g: (group(*g), 0, 0)),
                 pl.BlockSpec((nbs, heads, dk, dv), lambda *g: (group(*g), 0, 0, 0))]
    out_shape = [jax.ShapeDtypeStruct((bs // nbs, nbs * ls, heads * dv), F32),
                 jax.ShapeDtypeStruct(s0_s.shape, F32)]
    return in_specs, out_specs, out_shape


def _ret_core(p_s, cos_s, sin_s, s0_s, onorm_w):
    bs, ls, pw = p_s.shape
    heads, dk, dv = s0_s.shape[1:]
    nbs = RET_SEQS_PER_STEP if bs % RET_SEQS_PER_STEP == 0 else 1
    in_specs, out_specs, out_shape = _ret_side_specs(p_s, s0_s, nbs, lambda i: i)
    in_specs.append(pl.BlockSpec((heads, dv), lambda i: (0, 0)))
    o, s = pl.pallas_call(
        functools.partial(_ret_core_kernel, heads=heads, dk=dk, dv=dv, nb=nbs),
        grid=(bs // nbs,), in_specs=in_specs, out_specs=out_specs, out_shape=out_shape,
        compiler_params=_params("parallel"), name="ret_core",
    )(p_s.reshape(bs // nbs, nbs * ls, pw), cos_s, sin_s, s0_s, onorm_w)
    return o.reshape(bs, ls, heads * dv), s


def _ret_layer_kernel(*refs, heads, dk, dv, c, side_seqs):
    x_ref, nw_ref, win_ref, cos_ref, sin_ref, onw_ref, wout_ref, fw_ref = refs[:8]
    if side_seqs:
        ps_ref, cos_s_ref, sin_s_ref, s0s_ref, y_ref, s_ref, os_ref, ss_ref, o_scr = refs[8:]
    else:
        y_ref, s_ref, o_scr = refs[8:]

    x = x_ref[0]
    p = _dot(_rms_scale(x, nw_ref[...]).astype(BF16), win_ref[...])

    def cols(j, lo, hi):
        return p[:, lo:hi]

    def put_out(j, h, value):
        o_scr[:, h * dv:(h + 1) * dv] = value.astype(BF16)

    _ret_chunk(cols, put_out, cos_ref, sin_ref, None, onw_ref, s_ref,
               heads=heads, dk=dk, dv=dv, nb=1, c=c, first_step=pl.program_id(1) == 0)
    if side_seqs:
        _ret_side_chunk(ps_ref, cos_s_ref, sin_s_ref, s0s_ref, onw_ref, os_ref, ss_ref,
                        heads=heads, dk=dk, dv=dv, nb=side_seqs)
    y_ref[0] = _rms_scale(x + _dot(o_scr[...], wout_ref[...]), fw_ref[...])


def _ret_layer(x, norm_w, w_in, cos, sin, onorm_w, w_out, final_w, heads, side=None):
    b, l, d = x.shape
    pw = w_in.shape[1]
    dv = w_out.shape[0] // heads
    dk = (pw - 2 * heads * dv) // (2 * heads)
    c = RET_CHUNK
    steps = l // c
    assert l % c == 0 and dk // 2 == LANES
    resident = dict(pipeline_mode=pl.Buffered(1))
    in_specs = [pl.BlockSpec((1, c, d), lambda i, t: (i, t, 0)),
                pl.BlockSpec((1, d), lambda i, t: (0, 0)),
                pl.BlockSpec((d, pw), lambda i, t: (0, 0), **resident),
                pl.BlockSpec((c, dk // 2), lambda i, t: (t, 0)),
                pl.BlockSpec((c, dk // 2), lambda i, t: (t, 0)),
                pl.BlockSpec((heads, dv), lambda i, t: (0, 0)),
                pl.BlockSpec((heads * dv, d), lambda i, t: (0, 0), **resident),
                pl.BlockSpec((1, d), lambda i, t: (0, 0))]
    out_specs = [pl.BlockSpec((1, c, d), lambda i, t: (i, t, 0)),
                 pl.BlockSpec((1, heads, dk, dv), lambda i, t: (i, 0, 0, 0))]
    out_shape = [jax.ShapeDtypeStruct((b, l, d), F32),
                 jax.ShapeDtypeStruct((b, heads, dk, dv), F32)]
    args = [x, norm_w.reshape(1, d), w_in, cos, sin, onorm_w, w_out, final_w.reshape(1, d)]
    nbs = 0
    if side is not None:
        p_s, cos_s, sin_s, s0_s = side
        bs, ls, _ = p_s.shape
        nbs = RET_SEQS_PER_STEP
        assert bs == nbs * b * steps and p_s.shape[2] == pw
        assert s0_s.shape == (bs, heads, dk, dv)
        side_in, side_out, side_shape = _ret_side_specs(p_s, s0_s, nbs,
                                                        lambda i, t: i * steps + t)
        in_specs += side_in
        out_specs += side_out
        out_shape += side_shape
        args += [p_s.reshape(bs // nbs, nbs * ls, pw), cos_s, sin_s, s0_s]
    kern = functools.partial(_ret_layer_kernel, heads=heads, dk=dk, dv=dv, c=c, side_seqs=nbs)
    outs = pl.pallas_call(
        kern, grid=(b, steps), in_specs=in_specs, out_specs=out_specs, out_shape=out_shape,
        scratch_shapes=[pltpu.VMEM((c, heads * dv), BF16)],
        compiler_params=_params("parallel", "arbitrary"), name="ret_layer",
    )(*args)
    if side is None:
        return outs
    y, s, o_s, s_s = outs
    return y, s, o_s.reshape(bs, ls, heads * dv), s_s


def _row_tile(t):
    for tm in (256, 128, 64, 32, 16, 8):
        if t % tm == 0:
            return tm
    raise ValueError(f"token count {t} is not a multiple of {SUBLANES}")


def _rope_tables(l, half, pos0):
    inv = 1.0 / (ROPE_BASE ** jnp.linspace(0.0, 1.0, half, dtype=F32))
    pos = pos0 + jnp.arange(l, dtype=F32)
    ang = pos[:, None] * inv[None, :]
    return jnp.cos(ang), jnp.sin(ang)


def _beta_decay_weight(w, heads, main):
    zeros = jnp.zeros((w.shape[0], LANES - heads), w.dtype)
    return jnp.concatenate([w[:, main:main + heads], zeros, w[:, main + heads:], zeros],
                           axis=1).astype(BF16)


def kernel(x_prompt, x_sample, state_gdn_ssm, state_gdn_conv, state_ret, norm_w, w_in_a,
           conv_w_a, a_log_a, dt_bias_a, onorm_a, w_out_a, w_in_b, onorm_b, w_out_b,
           final_norm_w):
    assert state_gdn_ssm.shape[0] == 1 and state_ret.shape[0] == 1 and norm_w.shape[0] == 2
    bp, lp, d = x_prompt.shape
    bs, ls, _ = x_sample.shape
    heads_a, dk_a, dv_a = state_gdn_ssm.shape[2:]
    heads_b, dk_b, dv_b = state_ret.shape[2:]
    main_a = 2 * heads_a * dk_a + 2 * heads_a * dv_a
    win_a = w_in_a[0].astype(BF16)
    wba_a = _beta_decay_weight(w_in_a[0], heads_a, main_a)
    wout_a = w_out_a[0].astype(BF16)
    win_b = w_in_b[0].astype(BF16)
    wout_b = w_out_b[0].astype(BF16)

    ts = bs * ls
    tm = _row_tile(ts)
    xs = x_sample.reshape(ts, d)
    p, ba = _in_proj(xs, norm_w[0], win_a, main_a, side_w=wba_a)
    o, sa_s, ca_s = _gdn_core(p.reshape(bs, ls, -1), ba.reshape(bs, ls, -1), state_gdn_ssm[0],
                              jnp.swapaxes(state_gdn_conv[0], 0, 1), conv_w_a[0], a_log_a[0],
                              dt_bias_a[0], onorm_a[0])
    xs = _out_proj(o.reshape(ts, -1), xs, wout_a, final_norm_w, tm, False)
    (p_s,) = _in_proj(xs, norm_w[1], win_b, win_b.shape[1])
    p_s = p_s.reshape(bs, ls, -1)
    cos_s, sin_s = _rope_tables(ls, dk_b // 2, PAST_LEN)

    x1, sa_p, ca_p = _gdn_layer(x_prompt, norm_w[0], win_a, wba_a, conv_w_a[0], a_log_a[0],
                                dt_bias_a[0], onorm_a[0], wout_a, heads_a)
    cos_p, sin_p = _rope_tables(lp, dk_b // 2, 0.0)
    if bs == RET_SEQS_PER_STEP * bp * (lp // RET_CHUNK):
        y_p, sb_p, o, sb_s = _ret_layer(x1, norm_w[1], win_b, cos_p, sin_p, onorm_b[0], wout_b,
                                        final_norm_w, heads_b,
                                        side=(p_s, cos_s, sin_s, state_ret[0]))
    else:
        y_p, sb_p = _ret_layer(x1, norm_w[1], win_b, cos_p, sin_p, onorm_b[0], wout_b,
                               final_norm_w, heads_b)
        o, sb_s = _ret_core(p_s, cos_s, sin_s, state_ret[0], onorm_b[0])
    y_s = _out_proj(o.reshape(ts, -1), xs, wout_b, final_norm_w, tm, True).reshape(bs, ls, d)
    ca_s = jnp.swapaxes(ca_s, 0, 1)[None]
    return (y_p, y_s, sa_p[None], ca_p, sb_p[None], sa_s[None], ca_s, sb_s[None])
```

```python
import functools
import itertools

import jax
import jax.numpy as jnp
from jax import lax
from jax.experimental import pallas as pl
from jax.experimental.pallas import tpu as pltpu

F32 = jnp.float32
BF16 = jnp.bfloat16
EPS = 1e-6
CHUNK = 64
CONV_W = 4
ROPE_BASE = 10000.0
PAST_LEN = 16384.0
LANES = 128
SUBLANES = 8
HIST0 = SUBLANES - (CONV_W - 1)
VMEM_LIMIT_BYTES = 60 * 1024 * 1024
GDN_CHUNKS_PER_STEP = 4
GDN_BLOCKS_PER_STEP = 2
IN_PROJ_TILE = 256
SOLVE_BLOCK = 8
ROW_STRIDE = 4
RET_CHUNK = 256
RET_SEQS_PER_STEP = 2


def _dot(a, b):
    return jnp.dot(a, b, preferred_element_type=F32)


def _dot_nt(a, b):
    return lax.dot_general(a, b, (((1,), (1,)), ((), ())), preferred_element_type=F32)


def _dot_tn(a, b):
    return lax.dot_general(a, b, (((0,), (0,)), ((), ())), preferred_element_type=F32)


def _split3(x):
    hi = x.astype(BF16)
    r = x - hi.astype(F32)
    mid = r.astype(BF16)
    lo = (r - mid.astype(F32)).astype(BF16)
    return hi, mid, lo


def _sigmoid(x):
    return 1.0 / (1.0 + jnp.exp(-x))


def _silu(x):
    return x * _sigmoid(x)


def _softplus(x):
    return jnp.maximum(x, 0.0) + jnp.log1p(jnp.exp(-jnp.abs(x)))


def _rms_scale(x, w):
    return x * lax.rsqrt(jnp.mean(x * x, axis=-1, keepdims=True) + EPS) * w


def _params(*sem):
    return pltpu.CompilerParams(dimension_semantics=sem, vmem_limit_bytes=VMEM_LIMIT_BYTES)


def _lane_pad(vec):
    return jnp.pad(vec.astype(F32), (0, LANES - vec.shape[0])).reshape(1, LANES)


def _in_proj_kernel(x_ref, nw_ref, *refs):
    n = len(refs) // 2
    xn = _rms_scale(x_ref[...], nw_ref[...]).astype(BF16)
    for w_ref, o_ref in zip(refs[:n], refs[n:]):
        o_ref[...] = _dot(xn, w_ref[...])


def _in_proj(x, norm_w, weights, tm):
    t, d = x.shape
    assert t % tm == 0 and all(n % LANES == 0 for _, n in weights)
    in_specs = [pl.BlockSpec((tm, d), lambda i: (i, 0)),
                pl.BlockSpec((1, d), lambda i: (0, 0))]
    in_specs += [pl.BlockSpec((d, n), lambda i: (0, 0)) for _, n in weights]
    return pl.pallas_call(
        _in_proj_kernel, grid=(t // tm,), in_specs=in_specs,
        out_specs=[pl.BlockSpec((tm, n), lambda i: (i, 0)) for _, n in weights],
        out_shape=[jax.ShapeDtypeStruct((t, n), F32) for _, n in weights],
        compiler_params=_params("parallel"), name="in_proj",
    )(x, norm_w.reshape(1, d), *[w for w, _ in weights])


def _out_proj_kernel(o_ref, x_ref, w_ref, fw_ref, y_ref, *, final_norm):
    y = x_ref[...] + _dot(o_ref[...].astype(BF16), w_ref[...])
    y_ref[...] = _rms_scale(y, fw_ref[...]) if final_norm else y


def _out_proj(o, x, w, final_w, tm, final_norm):
    t, k = o.shape
    d = x.shape[1]
    assert t % tm == 0
    return pl.pallas_call(
        functools.partial(_out_proj_kernel, final_norm=final_norm),
        grid=(t // tm,),
        in_specs=[pl.BlockSpec((tm, k), lambda i: (i, 0)),
                  pl.BlockSpec((tm, d), lambda i: (i, 0)),
                  pl.BlockSpec((k, d), lambda i: (0, 0)),
                  pl.BlockSpec((1, d), lambda i: (0, 0))],
        out_specs=pl.BlockSpec((tm, d), lambda i: (i, 0)),
        out_shape=jax.ShapeDtypeStruct((t, d), F32),
        compiler_params=_params("parallel"), name="out_proj",
    )(o, x, w, final_w.reshape(1, d))


def _unit_lower_solves(lows, rhs, eye, cl, inner):
    bs = min(cl, SOLVE_BLOCK)
    diag = lows if cl == bs else [jnp.where(inner, low, 0.0) for low in lows]
    powers = [(-d).astype(BF16) for d in diag]
    invs = [eye - d for d in diag]
    p = 1
    while 2 * p < bs:
        sq = [_dot(pw, pw) for pw in powers]
        p *= 2
        powers = [s.astype(BF16) for s in sq]
        invs = [inv + _dot(inv.astype(BF16), pw) for inv, pw in zip(invs, powers)]
    if cl > bs:
        dinv = [inv.astype(BF16) for inv in invs]
        m = [_dot(di, (low - d).astype(BF16)) for di, low, d in zip(dinv, lows, diag)]
        mb = [x.astype(BF16) for x in m]
        series = [eye - x for x in m]
        power, q = mb, 1
        while 2 * q < cl // bs:
            power = [_dot(pw, pw).astype(BF16) for pw in power]
            q *= 2
            series = [s + _dot(s.astype(BF16), pw) for s, pw in zip(series, power)]
        invs = [_dot(s.astype(BF16), di) for s, di in zip(series, dinv)]
    return [_dot(inv.astype(BF16), r) for inv, r in zip(invs, rhs)]


def _gdn_step(front, tok, alog_ref, dtb_ref, onw_ref, s_scr, *, heads, dk, dv, nb, cl, nc,
              fill=lambda: None):
    c = nb * cl
    qk_w = heads * dk
    row = lax.broadcasted_iota(jnp.int32, (c, c), 0)
    col = lax.broadcasted_iota(jnp.int32, (c, c), 1)
    shift = cl.bit_length() - 1
    same = lax.shift_right_logical(row, shift) == lax.shift_right_logical(col, shift)
    incl = same & (tok(row) >= tok(col))
    strict = same & (tok(row) > tok(col))
    eye = (row == col).astype(F32)
    cum_masks = jnp.concatenate([incl.astype(BF16), same.astype(BF16)], axis=0)
    sel = (lax.broadcasted_iota(jnp.int32, (SUBLANES, LANES), 0)
           == lax.broadcasted_iota(jnp.int32, (SUBLANES, LANES), 1)).astype(BF16)
    neg_a = -jnp.exp(alog_ref[...])
    units = [(ci, h) for ci in range(nc) for h in range(heads)]
    seqs = range(nb)

    def stack(parts):
        return parts[0] if nb == 1 else jnp.concatenate(parts, axis=0)

    beta, gcum, gtot, gcum_t = [], [], [], []
    for ci in range(nc):
        beta.append(_sigmoid(front.gate(ci, 0)))
        g = neg_a * _softplus(front.gate(ci, 1) + dtb_ref[...])
        g3 = _split3(g)
        gg = _dot(cum_masks, g3[0]) + (_dot(cum_masks, g3[1]) + _dot(cum_masks, g3[2]))
        gcum.append(gg[:c])
        gtot.append(gg[c:])
        gc3 = _split3(gg[:c])
        gcum_t.append(_dot_nt(sel, gc3[0]) + (_dot_nt(sel, gc3[1]) + _dot_nt(sel, gc3[2])))

    q_l, eg_l, lows, rhs_l, qk_l, kd_l = [], [], [], [], [], []
    for ci, h in units:
        q = front.mixed(ci, h * dk, (h + 1) * dk)
        k = front.mixed(ci, qk_w + h * dk, qk_w + (h + 1) * dk)
        v = front.mixed(ci, 2 * qk_w + h * dv, 2 * qk_w + (h + 1) * dv)
        q = q * lax.rsqrt(jnp.sum(q * q, axis=-1, keepdims=True) + EPS) * (dk ** -0.5)
        k = k * lax.rsqrt(jnp.sum(k * k, axis=-1, keepdims=True) + EPS)
        gc = gcum[ci][:, h:h + 1]
        gr = gcum_t[ci][h:h + 1, :]
        bh = beta[ci][:, h:h + 1]
        dec = jnp.where(incl, jnp.exp(jnp.where(incl, gc - gr, 0.0)), 0.0)
        eg = jnp.exp(gc)
        qb = q.astype(BF16)
        kb = k.astype(BF16)
        lows.append(jnp.where(strict, bh * _dot_nt(kb, kb) * dec, 0.0))
        qk_l.append((_dot_nt(qb, kb) * dec).astype(BF16))
        rhs_l.append(jnp.concatenate([bh * v, (bh * eg) * k], axis=-1).astype(BF16))
        kd_l.append(k * jnp.exp(gtot[ci][:, h:h + 1] - gc))
        q_l.append(q)
        eg_l.append(eg)
        fill()
    blk = SOLVE_BLOCK.bit_length() - 1
    inner = same & (lax.shift_right_logical(tok(row), blk)
                    == lax.shift_right_logical(tok(col), blk))
    sols = _unit_lower_solves(lows, rhs_l, eye, cl, inner)

    for ci in range(nc):
        idx = [ci * heads + h for h in range(heads)]
        s_old = [[s_scr[j, h] for j in seqs] for h in range(heads)]
        res = [[_dot(jnp.concatenate([sols[i][j * cl:(j + 1) * cl, dv:],
                                      q_l[i][j * cl:(j + 1) * cl]], axis=0).astype(BF16),
                     s_old[h][j].astype(BF16)) for j in seqs]
               for h, i in enumerate(idx)]
        u = [stack([sols[i][j * cl:(j + 1) * cl, :dv] - res[h][j][:cl] for j in seqs])
             for h, i in enumerate(idx)]
        o = [eg_l[i] * stack([res[h][j][cl:] for j in seqs]) + _dot(qk_l[i], u[h].astype(BF16))
             for h, i in enumerate(idx)]
        for h, i in enumerate(idx):
            for j in seqs:
                r0, r1 = j * cl, (j + 1) * cl
                s_scr[j, h] = (jnp.exp(gtot[ci][r0:r0 + 1, h:h + 1]) * s_old[h][j]
                               + _dot_tn(kd_l[i][r0:r1].astype(BF16), u[h][r0:r1].astype(BF16)))
        for h in range(heads):
            front.put(ci, h, _rms_scale(o[h], onw_ref[...]) * _silu(front.z(ci, h)))


class _RollFront:
    def __init__(self, p_ref, ba_ref, o_ref, xe_scr, cw_ref, *, dv, nb):
        self.p_ref, self.ba_ref, self.o_ref, self.dv = p_ref, ba_ref, o_ref, dv
        self.conv_ch = xe_scr.shape[2]
        conv = []
        for j in range(nb):
            xall = xe_scr[j]
            acc = xall * cw_ref[0:1, :]
            for tap in range(1, CONV_W):
                acc = pltpu.roll(acc, 1, axis=0) + xall * cw_ref[tap:tap + 1, :]
            conv.append(acc[SUBLANES:])
        self.act = _silu(conv[0] if nb == 1 else jnp.concatenate(conv, axis=0))

    def mixed(self, ci, lo, hi):
        return self.act[:, lo:hi]

    def gate(self, ci, k):
        return self.ba_ref[0, :, k * LANES:(k + 1) * LANES]

    def z(self, ci, h):
        return self.p_ref[0, :, self.conv_ch + h * self.dv:self.conv_ch + (h + 1) * self.dv]

    def put(self, ci, h, value):
        self.o_ref[0, :, h * self.dv:(h + 1) * self.dv] = value


def _gdn_core_kernel(p_ref, ba_ref, s0_ref, c0_ref, cw_ref, alog_ref, dtb_ref, onw_ref,
                     o_ref, s_ref, c_ref, s_scr, xe_scr, *, heads, dk, dv, nb, cl):
    conv_ch = xe_scr.shape[2]
    s_scr[...] = s0_ref[...]
    for j in range(nb):
        xe_scr[j, 0:SUBLANES, :] = jnp.zeros((SUBLANES, conv_ch), F32)
        for r in range(CONV_W - 1):
            xe_scr[j, HIST0 + r:HIST0 + r + 1, :] = c0_ref[r, j:j + 1, :]
        xe_scr[j, SUBLANES:SUBLANES + cl, :] = p_ref[0, j * cl:(j + 1) * cl, 0:conv_ch]
    front = _RollFront(p_ref, ba_ref, o_ref, xe_scr, cw_ref, dv=dv, nb=nb)
    _gdn_step(front, lambda r: r & (cl - 1), alog_ref, dtb_ref, onw_ref, s_scr,
              heads=heads, dk=dk, dv=dv, nb=nb, cl=cl, nc=1)
    s_ref[...] = s_scr[...]
    for j in range(nb):
        for r in range(CONV_W - 1):
            c_ref[r, j:j + 1, :] = xe_scr[j, cl + HIST0 + r:cl + HIST0 + r + 1, :]


def _gdn_core(p, ba, s0, c0, conv_w, a_log, dt_bias, onorm_w):
    b, l, pw = p.shape
    heads, dk, dv = s0.shape[1:]
    conv_ch = c0.shape[2]
    assert c0.shape[:2] == (CONV_W - 1, b)
    assert heads <= SUBLANES and dk == LANES and dv == LANES
    assert pw == conv_ch + heads * dv and ba.shape[2] == 2 * LANES
    assert l <= CHUNK and l % SUBLANES == 0 and l & (l - 1) == 0
    nb = max(n for n in range(1, CHUNK // l + 1) if b % n == 0)
    groups, rows = b // nb, nb * l
    kern = functools.partial(_gdn_core_kernel, heads=heads, dk=dk, dv=dv, nb=nb, cl=l)
    o, s, cs = pl.pallas_call(
        kern, grid=(groups, 1),
        in_specs=[pl.BlockSpec((1, rows, pw), lambda i, t: (i, 0, 0)),
                  pl.BlockSpec((1, rows, 2 * LANES), lambda i, t: (i, 0, 0)),
                  pl.BlockSpec((nb, heads, dk, dv), lambda i, t: (i, 0, 0, 0)),
                  pl.BlockSpec((CONV_W - 1, nb, conv_ch), lambda i, t: (0, i, 0)),
                  pl.BlockSpec((CONV_W, conv_ch), lambda i, t: (0, 0)),
                  pl.BlockSpec((1, LANES), lambda i, t: (0, 0)),
                  pl.BlockSpec((1, LANES), lambda i, t: (0, 0)),
                  pl.BlockSpec((1, dv), lambda i, t: (0, 0))],
        out_specs=[pl.BlockSpec((1, rows, heads * dv), lambda i, t: (i, 0, 0)),
                   pl.BlockSpec((nb, heads, dk, dv), lambda i, t: (i, 0, 0, 0)),
                   pl.BlockSpec((CONV_W - 1, nb, conv_ch), lambda i, t: (0, i, 0))],
        out_shape=[jax.ShapeDtypeStruct((groups, rows, heads * dv), F32),
                   jax.ShapeDtypeStruct(s0.shape, F32),
                   jax.ShapeDtypeStruct(c0.shape, F32)],
        scratch_shapes=[pltpu.VMEM((nb, heads, dk, dv), F32),
                        pltpu.VMEM((nb, SUBLANES + l, conv_ch), F32)],
        compiler_params=_params("parallel", "arbitrary"), name="gdn_core",
    )(p.reshape(groups, rows, pw), ba.reshape(groups, rows, 2 * LANES), s0, c0, conv_w,
      _lane_pad(a_log), _lane_pad(dt_bias), onorm_w.reshape(1, dv))
    return o.reshape(b, l, heads * dv), s, cs


class _StridedFront:
    def __init__(self, xe_scr, zb_scr, o_scr, cw_ref, *, heads):
        self.xe, self.zb, self.o, self.cw, self.heads = xe_scr, zb_scr, o_scr, cw_ref, heads

    @staticmethod
    def tok(r):
        span = SUBLANES * ROW_STRIDE
        vreg = lax.shift_right_logical(r, SUBLANES.bit_length() - 1)
        return ((r & (CHUNK - span)) + (vreg & (ROW_STRIDE - 1))
                + (r & (SUBLANES - 1)) * ROW_STRIDE)

    @staticmethod
    def _starts(ci):
        return [ci * CHUNK + g * SUBLANES * ROW_STRIDE + s
                for g in range(CHUNK // (SUBLANES * ROW_STRIDE)) for s in range(ROW_STRIDE)]

    def _rows(self, ref, slab, base, ci):
        return jnp.concatenate([ref[slab, pl.ds(base + st, SUBLANES, stride=ROW_STRIDE), :]
                                for st in self._starts(ci)], axis=0)

    def mixed(self, ci, lo, hi):
        assert hi - lo == LANES and lo % LANES == 0
        acc = self._rows(self.xe, lo // LANES, HIST0, ci) * self.cw[0:1, lo:hi]
        for tap in range(1, CONV_W):
            acc = acc + self._rows(self.xe, lo // LANES, HIST0 + tap, ci) * self.cw[tap:tap + 1,
                                                                                   lo:hi]
        return _silu(acc)

    def gate(self, ci, k):
        return self._rows(self.zb, self.heads + k, 0, ci)

    def z(self, ci, h):
        return self._rows(self.zb, h, 0, ci)

    def put(self, ci, h, value):
        for n, st in enumerate(self._starts(ci)):
            self.o[h, pl.ds(st, SUBLANES, stride=ROW_STRIDE), :] = value[n * SUBLANES:
                                                                         (n + 1) * SUBLANES]


def _gdn_layer_kernel(x_ref, xnext_ref, nw_ref, win_ref, wba_ref, cw_ref, alog_ref, dtb_ref,
                      onw_ref, wout_ref, y_ref, s_ref, c_ref, s_scr, xe_scr, zb_scr, o_scr,
                      *, heads, dk, dv, nc):
    rows = nc * CHUNK
    blocks = xe_scr.shape[0]
    n_conv = xe_scr.shape[1]
    t = pl.program_id(1)
    step = pl.program_id(0) * pl.num_programs(1) + t

    def in_proj_tiles(x_rows, blk):
        xn = _rms_scale(x_rows(), nw_ref[...]).astype(BF16)
        for lo in range(0, (n_conv + heads) * LANES, IN_PROJ_TILE):
            r = _dot(xn, win_ref[:, lo:lo + IN_PROJ_TILE])
            for k in range(IN_PROJ_TILE // LANES):
                s = lo // LANES + k
                piece = r[:, k * LANES:(k + 1) * LANES]
                if s < n_conv:
                    xe_scr[blk, s, SUBLANES:SUBLANES + rows, :] = piece
                else:
                    zb_scr[blk, s - n_conv] = piece
            yield
        ba = _dot(xn, wba_ref[...])
        zb_scr[blk, heads] = ba[:, 0:LANES]
        zb_scr[blk, heads + 1] = ba[:, LANES:2 * LANES]
        yield

    def block_rows(blk):
        return lambda: x_ref[0, blk * rows:(blk + 1) * rows, :]

    @pl.when(step == 0)
    def _():
        for _ in in_proj_tiles(block_rows(0), 0):
            pass

    @pl.when(t == 0)
    def _():
        s_scr[...] = jnp.zeros(s_scr.shape, F32)
        for s in range(n_conv):
            xe_scr[0, s, 0:SUBLANES, :] = jnp.zeros((SUBLANES, LANES), F32)

    def out_proj_tiles(blk):
        r0, r1 = blk * rows, (blk + 1) * rows
        o = jnp.concatenate([o_scr[blk, h] for h in range(heads)], axis=-1).astype(BF16)
        for lo in range(0, y_ref.shape[2], IN_PROJ_TILE):
            y_ref[0, r0:r1, lo:lo + IN_PROJ_TILE] = (
                x_ref[0, r0:r1, lo:lo + IN_PROJ_TILE]
                + _dot(o, wout_ref[:, lo:lo + IN_PROJ_TILE]))
            yield

    pending = iter(())
    for blk in range(blocks):
        xe, zb, osc = xe_scr.at[blk], zb_scr.at[blk], o_scr.at[blk]
        if blk + 1 < blocks:
            ahead = itertools.chain(pending, in_proj_tiles(block_rows(blk + 1), blk + 1))
        else:
            ahead = itertools.chain(pending, in_proj_tiles(lambda: xnext_ref[0], 0))
        front = _StridedFront(xe, zb, osc, cw_ref, heads=heads)
        _gdn_step(front, front.tok, alog_ref, dtb_ref, onw_ref, s_scr, heads=heads, dk=dk,
                  dv=dv, nb=1, cl=CHUNK, nc=nc, fill=lambda: next(ahead, None))
        for _ in ahead:
            pass
        pending = out_proj_tiles(blk)
        hist = [xe[s, rows + HIST0:rows + SUBLANES, :] for s in range(n_conv)]
        nxt = xe_scr.at[(blk + 1) % blocks]
        for s in range(n_conv):
            nxt[s, HIST0:SUBLANES, :] = hist[s]
    for _ in pending:
        pass

    @pl.when(t == pl.num_programs(1) - 1)
    def _():
        s_ref[...] = s_scr[...]
        for s in range(n_conv):
            c_ref[0, 0, :, s * LANES:(s + 1) * LANES] = hist[s]


def _gdn_layer(x, norm_w, w_in, w_ba, conv_w, a_log, dt_bias, onorm_w, w_out, heads):
    b, l, d = x.shape
    dv = w_out.shape[0] // heads
    dk = dv
    conv_ch = conv_w.shape[1]
    pw = conv_ch + heads * dv
    nc, blocks = GDN_CHUNKS_PER_STEP, GDN_BLOCKS_PER_STEP
    rows = CHUNK * nc
    step_rows = rows * blocks
    steps = l // step_rows
    assert l % step_rows == 0 and heads <= SUBLANES and dk == LANES and dv == LANES
    assert conv_ch == 3 * heads * dk and w_in.shape[1] >= pw and blocks >= 2
    assert w_ba.shape[1] == 2 * LANES and CHUNK % (SUBLANES * ROW_STRIDE) == 0
    assert pw % IN_PROJ_TILE == 0
    resident = dict(pipeline_mode=pl.Buffered(1))

    def next_first_block(i, t):
        nxt = jnp.minimum(i * steps + t + 1, b * steps - 1)
        return (nxt // steps, (nxt % steps) * blocks, 0)

    kern = functools.partial(_gdn_layer_kernel, heads=heads, dk=dk, dv=dv, nc=nc)
    return pl.pallas_call(
        kern, grid=(b, steps),
        in_specs=[pl.BlockSpec((1, step_rows, d), lambda i, t: (i, t, 0)),
                  pl.BlockSpec((1, rows, d), next_first_block),
                  pl.BlockSpec((1, d), lambda i, t: (0, 0)),
                  pl.BlockSpec((d, pw), lambda i, t: (0, 0), **resident),
                  pl.BlockSpec((d, 2 * LANES), lambda i, t: (0, 0), **resident),
                  pl.BlockSpec((CONV_W, conv_ch), lambda i, t: (0, 0)),
                  pl.BlockSpec((1, LANES), lambda i, t: (0, 0)),
                  pl.BlockSpec((1, LANES), lambda i, t: (0, 0)),
                  pl.BlockSpec((1, dv), lambda i, t: (0, 0)),
                  pl.BlockSpec((heads * dv, d), lambda i, t: (0, 0), **resident)],
        out_specs=[pl.BlockSpec((1, step_rows, d), lambda i, t: (i, t, 0)),
                   pl.BlockSpec((1, heads, dk, dv), lambda i, t: (i, 0, 0, 0)),
                   pl.BlockSpec((1, 1, CONV_W - 1, conv_ch), lambda i, t: (0, i, 0, 0))],
        out_shape=[jax.ShapeDtypeStruct((b, l, d), F32),
                   jax.ShapeDtypeStruct((b, heads, dk, dv), F32),
                   jax.ShapeDtypeStruct((1, b, CONV_W - 1, conv_ch), F32)],
        scratch_shapes=[pltpu.VMEM((1, heads, dk, dv), F32),
                        pltpu.VMEM((blocks, conv_ch // LANES, SUBLANES + rows, LANES), F32),
                        pltpu.VMEM((blocks, heads + 2, rows, LANES), F32),
                        pltpu.VMEM((blocks, heads, rows, LANES), F32)],
        compiler_params=_params("arbitrary", "arbitrary"), name="gdn_layer",
    )(x, x, norm_w.reshape(1, d), w_in, w_ba, conv_w, _lane_pad(a_log), _lane_pad(dt_bias),
      onorm_w.reshape(1, dv), w_out)


def _ret_chunk(cols, put_out, cos_ref, sin_ref, s0_ref, onw_ref, s_ref,
               *, heads, dk, dv, nb, c, first_step):
    qk_w = heads * dk
    g_off = 2 * qk_w + heads * dv
    half = dk // 2

    if first_step is not None:
        @pl.when(first_step)
        def _():
            s_ref[...] = jnp.zeros(s_ref.shape, F32) if s0_ref is None else s0_ref[...]
    state_ref = s0_ref if first_step is None else s_ref

    cos = cos_ref[...]
    sin = sin_ref[...]
    row = lax.broadcasted_iota(jnp.int32, (c, c), 0)
    col = lax.broadcasted_iota(jnp.int32, (c, c), 1)
    diff = (row - col).astype(F32)
    idx = lax.broadcasted_iota(jnp.int32, (c, 1), 0).astype(F32)

    def rotary(x):
        x1, x2 = x[:, :half], x[:, half:]
        return jnp.concatenate([x1 * cos - x2 * sin, x1 * sin + x2 * cos], axis=-1)

    decay, q_dec, k_dec, s_dec = [], [], [], []
    for h in range(heads):
        lg = jnp.log(jnp.full((1, 1), 1.0 - 2.0 ** (-5.0 - h), F32))
        decay.append(jnp.where(diff >= 0, jnp.exp(lg * jnp.maximum(diff, 0.0)), 0.0))
        q_dec.append(jnp.exp(lg * (idx + 1.0)))
        k_dec.append(jnp.exp(lg * (c - 1.0 - idx)))
        s_dec.append(jnp.exp(lg * c))

    units = [(j, h) for j in range(nb) for h in range(heads)]
    qb, kb, kdb, vb = [], [], [], []
    for j, h in units:
        q = rotary(cols(j, h * dk, (h + 1) * dk))
        k = rotary(cols(j, qk_w + h * dk, qk_w + (h + 1) * dk)) * (dk ** -0.5)
        qb.append(q.astype(BF16))
        kb.append(k.astype(BF16))
        kdb.append((k * k_dec[h]).astype(BF16))
        vb.append(cols(j, 2 * qk_w + h * dv, 2 * qk_w + (h + 1) * dv).astype(BF16))
    qk = [(_dot_nt(qb[i], kb[i]) * decay[h]).astype(BF16) for i, (j, h) in enumerate(units)]
    s_old = [state_ref[j, h] for j, h in units]
    qs = [_dot(qb[i], s_old[i].astype(BF16)) for i in range(len(units))]
    o = [_dot(qk[i], vb[i]) + q_dec[h] * qs[i] for i, (j, h) in enumerate(units)]
    for i, (j, h) in enumerate(units):
        s_ref[j, h] = s_dec[h] * s_old[i] + _dot_tn(kdb[i], vb[i])
    for i, (j, h) in enumerate(units):
        put_out(j, h, _rms_scale(o[i], onw_ref[h:h + 1, :])
                * _silu(cols(j, g_off + h * dv, g_off + (h + 1) * dv)))


def _ret_side_chunk(ps_ref, cos_ref, sin_ref, s0_ref, onw_ref, o_ref, s_ref, *, heads, dk, dv,
                    nb):
    c = ps_ref.shape[1] // nb

    def cols(j, lo, hi):
        return ps_ref[0, j * c:(j + 1) * c, lo:hi]

    def put_out(j, h, value):
        o_ref[0, j * c:(j + 1) * c, h * dv:(h + 1) * dv] = value

    _ret_chunk(cols, put_out, cos_ref, sin_ref, s0_ref, onw_ref, s_ref,
               heads=heads, dk=dk, dv=dv, nb=nb, c=c, first_step=None)


def _ret_core_kernel(ps_ref, cos_ref, sin_ref, s0_ref, onw_ref, o_ref, s_ref,
                     *, heads, dk, dv, nb):
    _ret_side_chunk(ps_ref, cos_ref, sin_ref, s0_ref, onw_ref, o_ref, s_ref,
                    heads=heads, dk=dk, dv=dv, nb=nb)


def _ret_side_specs(p_s, s0_s, nbs, group):
    bs, ls, pw = p_s.shape
    heads, dk, dv = s0_s.shape[1:]
    assert bs % nbs == 0 and ls <= CHUNK and ls % SUBLANES == 0 and dk // 2 == LANES
    assert pw == 2 * heads * dk + 2 * heads * dv
    in_specs = [pl.BlockSpec((1, nbs * ls, pw), lambda *g: (group(*g), 0, 0)),
                pl.BlockSpec((ls, dk // 2), lambda *g: (0, 0)),
                pl.BlockSpec((ls, dk // 2), lambda *g: (0, 0)),
                pl.BlockSpec((nbs, heads, dk, dv), lambda *g: (group(*g), 0, 0, 0))]
    out_specs = [pl.BlockSpec((1, nbs * ls, heads * dv), lambda *g: (group(*g), 0, 0)),
                 pl.BlockSpec((nbs, heads, dk, dv), lambda *g: (group(*g), 0, 0, 0))]
    out_shape = [jax.ShapeDtypeStruct((bs // nbs, nbs * ls, heads * dv), F32),
                 jax.ShapeDtypeStruct(s0_s.shape, F32)]
    return in_specs, out_specs, out_shape


def _ret_core(p_s, cos_s, sin_s, s0_s, onorm_w):
    bs, ls, pw = p_s.shape
    heads, dk, dv = s0_s.shape[1:]
    nbs = RET_SEQS_PER_STEP if bs % RET_SEQS_PER_STEP == 0 else 1
    in_specs, out_specs, out_shape = _ret_side_specs(p_s, s0_s, nbs, lambda i: i)
    in_specs.append(pl.BlockSpec((heads, dv), lambda i: (0, 0)))
    o, s = pl.pallas_call(
        functools.partial(_ret_core_kernel, heads=heads, dk=dk, dv=dv, nb=nbs),
        grid=(bs // nbs,), in_specs=in_specs, out_specs=out_specs, out_shape=out_shape,
        compiler_params=_params("parallel"), name="ret_core",
    )(p_s.reshape(bs // nbs, nbs * ls, pw), cos_s, sin_s, s0_s, onorm_w)
    return o.reshape(bs, ls, heads * dv), s


def _ret_layer_kernel(*refs, heads, dk, dv, c, side_seqs):
    x_ref, nw_ref, win_ref, cos_ref, sin_ref, onw_ref, wout_ref, fw_ref = refs[:8]
    if side_seqs:
        ps_ref, cos_s_ref, sin_s_ref, s0s_ref, y_ref, s_ref, os_ref, ss_ref, o_scr = refs[8:]
    else:
        y_ref, s_ref, o_scr = refs[8:]

    x = x_ref[0]
    p = _dot(_rms_scale(x, nw_ref[...]).astype(BF16), win_ref[...])

    def cols(j, lo, hi):
        return p[:, lo:hi]

    def put_out(j, h, value):
        o_scr[:, h * dv:(h + 1) * dv] = value.astype(BF16)

    _ret_chunk(cols, put_out, cos_ref, sin_ref, None, onw_ref, s_ref,
               heads=heads, dk=dk, dv=dv, nb=1, c=c, first_step=pl.program_id(1) == 0)
    if side_seqs:
        _ret_side_chunk(ps_ref, cos_s_ref, sin_s_ref, s0s_ref, onw_ref, os_ref, ss_ref,
                        heads=heads, dk=dk, dv=dv, nb=side_seqs)
    y_ref[0] = _rms_scale(x + _dot(o_scr[...], wout_ref[...]), fw_ref[...])


def _ret_layer(x, norm_w, w_in, cos, sin, onorm_w, w_out, final_w, heads, side=None):
    b, l, d = x.shape
    pw = w_in.shape[1]
    dv = w_out.shape[0] // heads
    dk = (pw - 2 * heads * dv) // (2 * heads)
    c = RET_CHUNK
    steps = l // c
    assert l % c == 0 and dk // 2 == LANES
    resident = dict(pipeline_mode=pl.Buffered(1))
    in_specs = [pl.BlockSpec((1, c, d), lambda i, t: (i, t, 0)),
                pl.BlockSpec((1, d), lambda i, t: (0, 0)),
                pl.BlockSpec((d, pw), lambda i, t: (0, 0), **resident),
                pl.BlockSpec((c, dk // 2), lambda i, t: (t, 0)),
                pl.BlockSpec((c, dk // 2), lambda i, t: (t, 0)),
                pl.BlockSpec((heads, dv), lambda i, t: (0, 0)),
                pl.BlockSpec((heads * dv, d), lambda i, t: (0, 0), **resident),
                pl.BlockSpec((1, d), lambda i, t: (0, 0))]
    out_specs = [pl.BlockSpec((1, c, d), lambda i, t: (i, t, 0)),
                 pl.BlockSpec((1, heads, dk, dv), lambda i, t: (i, 0, 0, 0))]
    out_shape = [jax.ShapeDtypeStruct((b, l, d), F32),
                 jax.ShapeDtypeStruct((b, heads, dk, dv), F32)]
    args = [x, norm_w.reshape(1, d), w_in, cos, sin, onorm_w, w_out, final_w.reshape(1, d)]
    nbs = 0
    if side is not None:
        p_s, cos_s, sin_s, s0_s = side
        bs, ls, _ = p_s.shape
        nbs = RET_SEQS_PER_STEP
        assert bs == nbs * b * steps and p_s.shape[2] == pw
        assert s0_s.shape == (bs, heads, dk, dv)
        side_in, side_out, side_shape = _ret_side_specs(p_s, s0_s, nbs,
                                                        lambda i, t: i * steps + t)
        in_specs += side_in
        out_specs += side_out
        out_shape += side_shape
        args += [p_s.reshape(bs // nbs, nbs * ls, pw), cos_s, sin_s, s0_s]
    kern = functools.partial(_ret_layer_kernel, heads=heads, dk=dk, dv=dv, c=c, side_seqs=nbs)
    outs = pl.pallas_call(
        kern, grid=(b, steps), in_specs=in_specs, out_specs=out_specs, out_shape=out_shape,
        scratch_shapes=[pltpu.VMEM((c, heads * dv), BF16)],
        compiler_params=_params("parallel", "arbitrary"), name="ret_layer",
    )(*args)
    if side is None:
        return outs
    y, s, o_s, s_s = outs
    return y, s, o_s.reshape(bs, ls, heads * dv), s_s


def _row_tile(t):
    for tm in (256, 128, 64, 32, 16, 8):
        if t % tm == 0:
            return tm
    raise ValueError(f"token count {t} is not a multiple of {SUBLANES}")


def _rope_tables(l, half, pos0):
    inv = 1.0 / (ROPE_BASE ** jnp.linspace(0.0, 1.0, half, dtype=F32))
    pos = pos0 + jnp.arange(l, dtype=F32)
    ang = pos[:, None] * inv[None, :]
    return jnp.cos(ang), jnp.sin(ang)


def _beta_decay_weight(w, heads, main):
    zeros = jnp.zeros((w.shape[0], LANES - heads), w.dtype)
    return jnp.concatenate([w[:, main:main + heads], zeros, w[:, main + heads:], zeros],
                           axis=1).astype(BF16)


def kernel(x_prompt, x_sample, state_gdn_ssm, state_gdn_conv, state_ret, norm_w, w_in_a,
           conv_w_a, a_log_a, dt_bias_a, onorm_a, w_out_a, w_in_b, onorm_b, w_out_b,
           final_norm_w):
    assert state_gdn_ssm.shape[0] == 1 and state_ret.shape[0] == 1 and norm_w.shape[0] == 2
    bp, lp, d = x_prompt.shape
    bs, ls, _ = x_sample.shape
    heads_a, dk_a, dv_a = state_gdn_ssm.shape[2:]
    heads_b, dk_b, dv_b = state_ret.shape[2:]
    main_a = 2 * heads_a * dk_a + 2 * heads_a * dv_a
    win_a = w_in_a[0].astype(BF16)
    wba_a = _beta_decay_weight(w_in_a[0], heads_a, main_a)
    wout_a = w_out_a[0].astype(BF16)
    win_b = w_in_b[0].astype(BF16)
    wout_b = w_out_b[0].astype(BF16)

    ts = bs * ls
    tm = _row_tile(ts)
    xs = x_sample.reshape(ts, d)
    p, ba = _in_proj(xs, norm_w[0], [(win_a, main_a), (wba_a, 2 * LANES)], tm)
    o, sa_s, ca_s = _gdn_core(p.reshape(bs, ls, -1), ba.reshape(bs, ls, -1), state_gdn_ssm[0],
                              jnp.swapaxes(state_gdn_conv[0], 0, 1), conv_w_a[0], a_log_a[0],
                              dt_bias_a[0], onorm_a[0])
    xs = _out_proj(o.reshape(ts, -1), xs, wout_a, final_norm_w, tm, False)
    (p_s,) = _in_proj(xs, norm_w[1], [(win_b, win_b.shape[1])], tm)
    p_s = p_s.reshape(bs, ls, -1)
    cos_s, sin_s = _rope_tables(ls, dk_b // 2, PAST_LEN)

    x1, sa_p, ca_p = _gdn_layer(x_prompt, norm_w[0], win_a, wba_a, conv_w_a[0], a_log_a[0],
                                dt_bias_a[0], onorm_a[0], wout_a, heads_a)
    cos_p, sin_p = _rope_tables(lp, dk_b // 2, 0.0)
    if bs == RET_SEQS_PER_STEP * bp * (lp // RET_CHUNK):
        y_p, sb_p, o, sb_s = _ret_layer(x1, norm_w[1], win_b, cos_p, sin_p, onorm_b[0], wout_b,
                                        final_norm_w, heads_b,
                                        side=(p_s, cos_s, sin_s, state_ret[0]))
    else:
        y_p, sb_p = _ret_layer(x1, norm_w[1], win_b, cos_p, sin_p, onorm_b[0], wout_b,
                               final_norm_w, heads_b)
        o, sb_s = _ret_core(p_s, cos_s, sin_s, state_ret[0], onorm_b[0])
    y_s = _out_proj(o.reshape(ts, -1), xs, wout_b, final_norm_w, tm, True).reshape(bs, ls, d)
    ca_s = jnp.swapaxes(ca_s, 0, 1)[None]
    return (y_p, y_s, sa_p[None], ca_p, sb_p[None], sa_s[None], ca_s, sb_s[None])
```
